```python
import math
import jax, jax.numpy as jnp
from jax import lax
import numpy as np

D_MODEL = 1024
BATCH = 32
SEQ = 2048
DEPTH = 2

N_EVEN = (DEPTH + 1) // 2
N_ODD = DEPTH // 2

MLA_HEADS = 8
MLA_NOPE = 64
MLA_ROPE = 32
MLA_V = 64
MLA_Q_RANK = 256
MLA_KV_RANK = 128
MLA_WIDTH = MLA_HEADS * MLA_V
ROPE_BASE = 10000.0
Q_BLOCK = 128

S5_WIDTH = 256
S5_GROUP = 16
S5_GROUPS = S5_WIDTH // S5_GROUP
S5_STATE = 64
DT_MIN = 1e-3
DT_MAX = 1e-1

GLA_HEADS = 4
GLA_KEY_WIDTH = D_MODEL // 2
GLA_WIDTH = D_MODEL
GLA_DK = GLA_KEY_WIDTH // GLA_HEADS
GLA_DV = GLA_WIDTH // GLA_HEADS
GLA_GATE_RANK = 16
GLA_GATE_NORM = 16.0
GLA_CHUNK = 64

EVEN_SPLITS = (MLA_Q_RANK, MLA_KV_RANK, MLA_ROPE, MLA_WIDTH, S5_WIDTH, S5_WIDTH)
EVEN_IN = sum(EVEN_SPLITS)
EVEN_MIX = MLA_WIDTH + S5_WIDTH
ODD_SPLITS = (GLA_KEY_WIDTH, GLA_KEY_WIDTH, GLA_WIDTH, GLA_GATE_RANK, GLA_WIDTH)
ODD_IN = sum(ODD_SPLITS)
ODD_MIX = GLA_WIDTH

DEEPNORM_ALPHA = (2 * DEPTH) ** 0.25
DEEPNORM_BETA = (8 * DEPTH) ** -0.25
LN_EPS = 1e-5
RMS_EPS = 1e-6

kernel_name = "hybrid_mla_s5_gla_deepnorm"


def _split(h, sizes):
    idx = [int(i) for i in np.cumsum(sizes)[:-1]]
    return jnp.split(h, idx, axis=-1)


def rms_norm(x, g):
    xf = x.astype(jnp.float32)
    y = xf * lax.rsqrt(jnp.mean(xf * xf, axis=-1, keepdims=True) + RMS_EPS)
    return (y * g.astype(jnp.float32)).astype(x.dtype)


def layer_norm(x, g, b):
    xf = x.astype(jnp.float32)
    mu = jnp.mean(xf, axis=-1, keepdims=True)
    xc = xf - mu
    var = jnp.mean(xc * xc, axis=-1, keepdims=True)
    y = xc * lax.rsqrt(var + LN_EPS) * g.astype(jnp.float32) + b.astype(jnp.float32)
    return y.astype(x.dtype)


def rope_tables(pos, dtype):
    half = MLA_ROPE // 2
    inv_freq = ROPE_BASE ** (-jnp.arange(half, dtype=jnp.float32) / half)
    ang = pos.astype(jnp.float32)[..., None] * inv_freq
    return jnp.cos(ang).astype(dtype), jnp.sin(ang).astype(dtype)


def apply_rope(x, cos, sin):
    half = x.shape[-1] // 2
    x1, x2 = x[..., :half], x[..., half:]
    return jnp.concatenate([x1 * cos - x2 * sin, x2 * cos + x1 * sin], axis=-1)


def mla_attention(cq, ckv, k_rope_raw, pos, q_norm, kv_norm, w_uq, w_ukv):
    B, S, _ = cq.shape
    H = MLA_HEADS
    q = (rms_norm(cq, q_norm) @ w_uq).reshape(B, S, H, MLA_NOPE + MLA_ROPE)
    kv = (rms_norm(ckv, kv_norm) @ w_ukv).reshape(B, S, H, MLA_NOPE + MLA_V)
    q_nope, q_pe = q[..., :MLA_NOPE], q[..., MLA_NOPE:]
    k_nope, v = kv[..., :MLA_NOPE], kv[..., MLA_NOPE:]
    cos, sin = rope_tables(pos, q.dtype)
    q_pe = apply_rope(q_pe, cos[:, :, None, :], sin[:, :, None, :])
    k_pe = apply_rope(k_rope_raw, cos, sin)
    scale = (MLA_NOPE + MLA_ROPE) ** -0.5
    nb = S // Q_BLOCK
    qn_blocks = q_nope.reshape(B, nb, Q_BLOCK, H, MLA_NOPE).transpose(1, 0, 2, 3, 4)
    qp_blocks = q_pe.reshape(B, nb, Q_BLOCK, H, MLA_ROPE).transpose(1, 0, 2, 3, 4)
    k_idx = jnp.arange(S)

    def block(args):
        qn, qp, i = args
        s = (jnp.einsum('bqhd,bkhd->bhqk', qn, k_nope)
             + jnp.einsum('bqhr,bkr->bhqk', qp, k_pe)).astype(jnp.float32) * scale
        q_idx = i * Q_BLOCK + jnp.arange(Q_BLOCK)
        causal = k_idx[None, :] <= q_idx[:, None]
        s = jnp.where(causal, s, -jnp.inf)
        p = jax.nn.softmax(s, axis=-1).astype(v.dtype)
        return jnp.einsum('bhqk,bkhd->bqhd', p, v)

    o = lax.map(block, (qn_blocks, qp_blocks, jnp.arange(nb)))
    return o.transpose(1, 0, 2, 3, 4).reshape(B, S, H * MLA_V)


def s5_layer(u, a_re, a_im, log_dt, b_re, b_im, c_re, c_im, d_skip, w_glu, b_glu):
    B, S, W = u.shape
    f32 = jnp.float32
    uf = u.astype(f32).reshape(B, S, S5_GROUPS, S5_GROUP)
    dt = jnp.exp(log_dt.astype(f32))[:, None]
    ar, ai = a_re.astype(f32), a_im.astype(f32)
    mag = jnp.exp(ar * dt)
    lr, li = mag * jnp.cos(ai * dt), mag * jnp.sin(ai * dt)
    den = ar * ar + ai * ai
    nr, ni = lr - 1.0, li
    zr = (nr * ar + ni * ai) / den
    zi = (ni * ar - nr * ai) / den
    br, bi = b_re.astype(f32), b_im.astype(f32)
    bbr = zr[..., None] * br - zi[..., None] * bi
    bbi = zr[..., None] * bi + zi[..., None] * br
    xr = jnp.einsum('gph,bsgh->bsgp', bbr, uf)
    xi = jnp.einsum('gph,bsgh->bsgp', bbi, uf)
    a_seq_r = jnp.broadcast_to(lr, (1, S) + lr.shape)
    a_seq_i = jnp.broadcast_to(li, (1, S) + li.shape)

    def combine(e1, e2):
        a1r, a1i, b1r, b1i = e1
        a2r, a2i, b2r, b2i = e2
        return (a2r * a1r - a2i * a1i,
                a2r * a1i + a2i * a1r,
                a2r * b1r - a2i * b1i + b2r,
                a2r * b1i + a2i * b1r + b2i)

    _, _, hr, hi = lax.associative_scan(combine, (a_seq_r, a_seq_i, xr, xi), axis=1)
    y = (jnp.einsum('ghp,bsgp->bsgh', c_re.astype(f32), hr)
         - jnp.einsum('ghp,bsgp->bsgh', c_im.astype(f32), hi)).reshape(B, S, W)
    y = y + d_skip.astype(f32) * u.astype(f32)
    y = jax.nn.gelu(y)
    y = y * jax.nn.sigmoid(y @ w_glu.astype(f32) + b_glu.astype(f32))
    return y.astype(u.dtype)


def gla_layer(q, k, v, gk_low, w_gk2, b_gk, g_norm):
    B, S, _ = q.shape
    H, DK, DV, C = GLA_HEADS, GLA_DK, GLA_DV, GLA_CHUNK
    N = S // C
    f32 = jnp.float32
    log_a = jax.nn.log_sigmoid((gk_low @ w_gk2 + b_gk).astype(f32)) / GLA_GATE_NORM

    def heads(t, d):
        return t.reshape(B, N, C, H, d).transpose(0, 3, 1, 2, 4).astype(f32)

    qh = heads(q, DK) * (DK ** -0.5)
    kh, vh, gh = heads(k, DK), heads(v, DV), heads(log_a, DK)
    bcum = jnp.cumsum(gh, axis=3)
    b_last = bcum[:, :, :, -1:]
    q_dec = qh * jnp.exp(bcum)
    k_inv = kh * jnp.exp(-bcum)
    k_to_end = kh * jnp.exp(b_last - bcum)
    tri = jnp.tril(jnp.ones((C, C), dtype=bool))
    att = jnp.where(tri, jnp.einsum('bhncd,bhnsd->bhncs', q_dec, k_inv), 0.0)
    o_intra = jnp.einsum('bhncs,bhnsv->bhncv', att, vh)

    def step(state, xs):
        qd, kend, vc, decay = xs
        o = jnp.einsum('bhcd,bhdv->bhcv', qd, state)
        state = state * decay[..., None] + jnp.einsum('bhcd,bhcv->bhdv', kend, vc)
        return state, o

    xs = (q_dec.transpose(2, 0, 1, 3, 4), k_to_end.transpose(2, 0, 1, 3, 4),
          vh.transpose(2, 0, 1, 3, 4), jnp.exp(b_last[:, :, :, 0]).transpose(2, 0, 1, 3))
    state0 = jnp.zeros((B, H, DK, DV), f32)
    _, o_inter = lax.scan(step, state0, xs)
    o = o_intra + o_inter.transpose(1, 2, 0, 3, 4)
    o = o.transpose(0, 2, 3, 1, 4).reshape(B, S, H, DV)
    o = rms_norm(o, g_norm).reshape(B, S, H * DV)
    return o.astype(q.dtype)


def even_layer(x, pos, w_in, q_norm, kv_norm, w_uq, w_ukv, a_re, a_im, log_dt,
               b_re, b_im, c_re, c_im, d_skip, w_glu, b_glu, w_out):
    h = x @ w_in
    cq, ckv, k_rope, g_mla, u, g_s5 = _split(h, EVEN_SPLITS)
    o_mla = mla_attention(cq, ckv, k_rope, pos, q_norm, kv_norm, w_uq, w_ukv) * jax.nn.silu(g_mla)
    o_s5 = s5_layer(u, a_re, a_im, log_dt, b_re, b_im, c_re, c_im, d_skip, w_glu, b_glu) * jax.nn.silu(g_s5)
    return jnp.concatenate([o_mla, o_s5], axis=-1) @ w_out


def odd_layer(x, w_in, w_gk2, b_gk, g_norm, w_out):
    h = x @ w_in
    q, k, v, gk_low, g = _split(h, ODD_SPLITS)
    o = gla_layer(q, k, v, gk_low, w_gk2, b_gk, g_norm) * jax.nn.silu(g)
    return o @ w_out


def setup_inputs(seed: int = 0) -> dict:
    key = jax.random.key(seed)
    ks = jax.random.split(key, 32)
    nrm = lambda k, shape, s: jax.random.normal(k, shape, jnp.float32) * s
    G, P, GS = S5_GROUPS, S5_STATE, S5_GROUP
    x = jax.random.normal(ks[0], (BATCH, SEQ, D_MODEL), jnp.float32)
    positions = jnp.broadcast_to(jnp.arange(SEQ, dtype=jnp.int32)[None, :], (BATCH, SEQ))
    ln_g = 1.0 + nrm(ks[1], (DEPTH, D_MODEL), 0.02)
    ln_b = nrm(ks[2], (DEPTH, D_MODEL), 0.02)
    even_w_in = nrm(ks[3], (N_EVEN, D_MODEL, EVEN_IN), D_MODEL ** -0.5)
    mla_q_norm = 1.0 + nrm(ks[4], (N_EVEN, MLA_Q_RANK), 0.02)
    mla_kv_norm = 1.0 + nrm(ks[5], (N_EVEN, MLA_KV_RANK), 0.02)
    mla_w_uq = nrm(ks[6], (N_EVEN, MLA_Q_RANK, MLA_HEADS * (MLA_NOPE + MLA_ROPE)), MLA_Q_RANK ** -0.5)
    mla_w_ukv = nrm(ks[7], (N_EVEN, MLA_KV_RANK, MLA_HEADS * (MLA_NOPE + MLA_V)), MLA_KV_RANK ** -0.5)
    n_idx = jnp.arange(P, dtype=jnp.float32)
    s5_a_re = -0.5 * jnp.exp(nrm(ks[8], (N_EVEN, G, P), 0.01))
    s5_a_im = math.pi * n_idx + nrm(ks[9], (N_EVEN, G, P), 0.01)
    s5_log_dt = jax.random.uniform(ks[10], (N_EVEN, G), jnp.float32, math.log(DT_MIN), math.log(DT_MAX))
    s5_b_re = nrm(ks[11], (N_EVEN, G, P, GS), (2 * GS) ** -0.5)
    s5_b_im = nrm(ks[12], (N_EVEN, G, P, GS), (2 * GS) ** -0.5)
    s5_c_re = nrm(ks[13], (N_EVEN, G, GS, P), (2 * P) ** -0.5)
    s5_c_im = nrm(ks[14], (N_EVEN, G, GS, P), (2 * P) ** -0.5)
    s5_d = nrm(ks[15], (N_EVEN, S5_WIDTH), 1.0)
    s5_w_glu = nrm(ks[16], (N_EVEN, S5_WIDTH, S5_WIDTH), S5_WIDTH ** -0.5)
    s5_b_glu = nrm(ks[17], (N_EVEN, S5_WIDTH), 0.02)
    even_w_out = nrm(ks[18], (N_EVEN, EVEN_MIX, D_MODEL), EVEN_MIX ** -0.5 * DEEPNORM_BETA)
    odd_w_in = nrm(ks[19], (N_ODD, D_MODEL, ODD_IN), D_MODEL ** -0.5)
    gla_w_gk2 = nrm(ks[20], (N_ODD, GLA_GATE_RANK, GLA_KEY_WIDTH), GLA_GATE_RANK ** -0.5)
    gla_b_gk = nrm(ks[21], (N_ODD, GLA_KEY_WIDTH), 0.02)
    gla_g_norm = 1.0 + nrm(ks[22], (N_ODD, GLA_DV), 0.02)
    odd_w_out = nrm(ks[23], (N_ODD, ODD_MIX, D_MODEL), ODD_MIX ** -0.5 * DEEPNORM_BETA)
    return {"x": x, "positions": positions, "ln_g": ln_g, "ln_b": ln_b,
            "even_w_in": even_w_in, "mla_q_norm": mla_q_norm, "mla_kv_norm": mla_kv_norm,
            "mla_w_uq": mla_w_uq, "mla_w_ukv": mla_w_ukv,
            "s5_a_re": s5_a_re, "s5_a_im": s5_a_im, "s5_log_dt": s5_log_dt,
            "s5_b_re": s5_b_re, "s5_b_im": s5_b_im, "s5_c_re": s5_c_re, "s5_c_im": s5_c_im,
            "s5_d": s5_d, "s5_w_glu": s5_w_glu, "s5_b_glu": s5_b_glu, "even_w_out": even_w_out,
            "odd_w_in": odd_w_in, "gla_w_gk2": gla_w_gk2, "gla_b_gk": gla_b_gk,
            "gla_g_norm": gla_g_norm, "odd_w_out": odd_w_out}


def reference(x, positions, ln_g, ln_b, even_w_in, mla_q_norm, mla_kv_norm, mla_w_uq, mla_w_ukv,
              s5_a_re, s5_a_im, s5_log_dt, s5_b_re, s5_b_im, s5_c_re, s5_c_im, s5_d, s5_w_glu,
              s5_b_glu, even_w_out, odd_w_in, gla_w_gk2, gla_b_gk, gla_g_norm, odd_w_out):
    for layer in range(DEPTH):
        j = layer // 2
        if layer % 2 == 0:
            y = even_layer(x, positions, even_w_in[j], mla_q_norm[j], mla_kv_norm[j], mla_w_uq[j],
                           mla_w_ukv[j], s5_a_re[j], s5_a_im[j], s5_log_dt[j], s5_b_re[j], s5_b_im[j],
                           s5_c_re[j], s5_c_im[j], s5_d[j], s5_w_glu[j], s5_b_glu[j], even_w_out[j])
        else:
            y = odd_layer(x, odd_w_in[j], gla_w_gk2[j], gla_b_gk[j], gla_g_norm[j], odd_w_out[j])
        x = layer_norm(DEEPNORM_ALPHA * x + y, ln_g[layer], ln_b[layer])
    return x
```

```python
import functools
import math

import jax
import jax.numpy as jnp
from jax import lax
from jax.experimental import pallas as pl
from jax.experimental.pallas import tpu as pltpu

F32 = jnp.float32
BF16 = jnp.bfloat16

D_MODEL = 1024
DEPTH = 2

MLA_HEADS = 8
MLA_NOPE = 64
MLA_ROPE = 32
MLA_V = 64
MLA_Q_RANK = 256
MLA_KV_RANK = 128
MLA_WIDTH = MLA_HEADS * MLA_V
ROPE_BASE = 10000.0
HEAD_SLOT = 128

S5_WIDTH = 256
S5_GROUP = 16
S5_GROUPS = S5_WIDTH // S5_GROUP
S5_STATE = 64
S5_L = 8
S5_NB = 1

GLA_HEADS = 4
GLA_KEY_WIDTH = D_MODEL // 2
GLA_WIDTH = D_MODEL
GLA_DK = GLA_KEY_WIDTH // GLA_HEADS
GLA_DV = GLA_WIDTH // GLA_HEADS
GLA_GATE_RANK = 16
GLA_GATE_NORM = 16.0
GLA_CHUNK = 64

DEEPNORM_ALPHA = (2 * DEPTH) ** 0.25
LN_EPS = 1e-5
RMS_EPS = 1e-6

TOKEN_TILE = 512
ATTN_TQ = 256
ATTN_TK = 256
NEG_BIG = -1e30
VMEM_LIMIT = 56 * 1024 * 1024


def _cparams(n_axes):
    return pltpu.CompilerParams(dimension_semantics=("arbitrary",) * n_axes,
                                vmem_limit_bytes=VMEM_LIMIT)


def _const_spec(shape, single=False):
    nd = len(shape)
    if single:
        return pl.BlockSpec(shape, lambda *_: (0,) * nd, pipeline_mode=pl.Buffered(1))
    return pl.BlockSpec(shape, lambda *_: (0,) * nd)


def _sigmoid(x):
    return 1.0 / (1.0 + jnp.exp(-x))


def _silu(x):
    return x * _sigmoid(x)


def _rms(x, g):
    return (x * lax.rsqrt(jnp.mean(x * x, axis=-1, keepdims=True) + RMS_EPS)) * g


def _rope_kernel(pos_ref, invf_ref, cos_ref, sin_ref, nsin_ref):
    ang = pos_ref[...].astype(F32) * invf_ref[...]
    c = jnp.cos(ang)
    s = jnp.sin(ang)
    cos_ref[...] = c
    sin_ref[...] = s
    nsin_ref[...] = -s


def _rope_tables(positions):
    t = positions.size
    half = MLA_ROPE // 2
    rows = t * half // 128
    pos_rep = jnp.repeat(positions.reshape(t), half).reshape(rows, 128)
    inv_freq = ROPE_BASE ** (-jnp.arange(half, dtype=F32) / half)
    invf = jnp.tile(inv_freq, 128 // half).reshape(1, 128)
    rb = 1024
    out = jax.ShapeDtypeStruct((rows, 128), F32)
    cos, sin, nsin = pl.pallas_call(
        _rope_kernel,
        grid=(rows // rb,),
        in_specs=[pl.BlockSpec((rb, 128), lambda i: (i, 0)), _const_spec((1, 128))],
        out_specs=[pl.BlockSpec((rb, 128), lambda i: (i, 0))] * 3,
        out_shape=[out, out, out],
        compiler_params=_cparams(1),
        name="rope_tables",
    )(pos_rep, invf)
    cos, sin, nsin = (a.reshape(t, half) for a in (cos, sin, nsin))
    ones = jnp.ones((t, MLA_NOPE), F32)
    zeros = jnp.zeros((t, MLA_NOPE), F32)
    pad = HEAD_SLOT - MLA_NOPE - MLA_ROPE
    cos_full = jnp.concatenate([ones, cos, cos, ones[:, :pad]], axis=1)
    sin_full = jnp.concatenate([zeros, nsin, sin, zeros[:, :pad]], axis=1)
    return cos_full, sin_full


EV_CQ = 0
EV_CKV = EV_CQ + MLA_Q_RANK
EV_KPE = EV_CKV + MLA_KV_RANK
EV_GM = EV_KPE + HEAD_SLOT
EV_U = EV_GM + MLA_WIDTH
EV_GS = EV_U + S5_WIDTH
EV_COLS = EV_GS + S5_WIDTH
Q_SCALE = (MLA_NOPE + MLA_ROPE) ** -0.5 * math.log2(math.e)


def _rotate_pairs(x):
    n = x.shape[-1]
    lane = lax.broadcasted_iota(jnp.int32, x.shape, 1)
    lo = (lane & 112) == 64
    return jnp.where(lo, pltpu.roll(x, n - MLA_ROPE // 2, 1), pltpu.roll(x, MLA_ROPE // 2, 1))


def _even_in_kernel(x_ref, cos_ref, sin_ref, w_in_ref, qn_ref, kvn_ref, wuq_ref, wk_ref, wv_ref,
                    q_ref, k_ref, v_ref, gm_ref, u_ref, gs_ref):
    xb = x_ref[...].astype(BF16)
    h = jnp.dot(xb, w_in_ref[...], preferred_element_type=F32)
    cos = cos_ref[...]
    sin = sin_ref[...]

    cqn = _rms(h[:, EV_CQ:EV_CKV], qn_ref[...]).astype(BF16)
    q = jnp.dot(cqn, wuq_ref[...], preferred_element_type=F32)
    q = q * jnp.tile(cos, (1, MLA_HEADS)) + _rotate_pairs(q) * jnp.tile(sin, (1, MLA_HEADS))
    q_ref[...] = (q * Q_SCALE).astype(BF16)

    kpe = h[:, EV_KPE:EV_GM]
    kpe = kpe * cos + _rotate_pairs(kpe) * sin
    ckvn = _rms(h[:, EV_CKV:EV_KPE], kvn_ref[...]).astype(BF16)
    k = jnp.dot(ckvn, wk_ref[...], preferred_element_type=F32) + jnp.tile(kpe, (1, MLA_HEADS))
    k_ref[...] = k.astype(BF16)
    v_ref[...] = jnp.dot(ckvn, wv_ref[...], preferred_element_type=F32).astype(BF16)

    gm_ref[...] = _silu(h[:, EV_GM:EV_U]).astype(BF16)
    u_ref[...] = h[:, EV_U:EV_GS].astype(BF16)
    gs_ref[...] = _silu(h[:, EV_GS:EV_COLS]).astype(BF16)


def _even_in(xf, cos_full, sin_full, w_in, q_norm, kv_norm, w_uq, w_ukv):
    t = xf.shape[0]
    tm = TOKEN_TILE
    pad = HEAD_SLOT - MLA_NOPE - MLA_ROPE
    cq, ckv, kr, gm, u, gs = jnp.split(
        w_in, [MLA_Q_RANK, MLA_Q_RANK + MLA_KV_RANK, MLA_Q_RANK + MLA_KV_RANK + MLA_ROPE,
               MLA_Q_RANK + MLA_KV_RANK + MLA_ROPE + MLA_WIDTH,
               MLA_Q_RANK + MLA_KV_RANK + MLA_ROPE + MLA_WIDTH + S5_WIDTH], axis=1)
    kr = jnp.pad(kr, ((0, 0), (MLA_NOPE, pad)))
    w_in_p = jnp.concatenate([cq, ckv, kr, gm, u, gs], axis=1).astype(BF16)
    wuq_p = jnp.pad(w_uq.reshape(MLA_Q_RANK, MLA_HEADS, MLA_NOPE + MLA_ROPE),
                    ((0, 0), (0, 0), (0, pad))).reshape(MLA_Q_RANK, MLA_HEADS * HEAD_SLOT).astype(BF16)
    wkv = w_ukv.reshape(MLA_KV_RANK, MLA_HEADS, MLA_NOPE + MLA_V)
    wk_p = jnp.pad(wkv[:, :, :MLA_NOPE], ((0, 0), (0, 0), (0, HEAD_SLOT - MLA_NOPE))
                   ).reshape(MLA_KV_RANK, MLA_HEADS * HEAD_SLOT).astype(BF16)
    wv_p = wkv[:, :, MLA_NOPE:].reshape(MLA_KV_RANK, MLA_WIDTH).astype(BF16)

    def row(n):
        return pl.BlockSpec((tm, n), lambda i: (i, 0))

    qk_w = MLA_HEADS * HEAD_SLOT
    outs = [jax.ShapeDtypeStruct((t, qk_w), BF16), jax.ShapeDtypeStruct((t, qk_w), BF16),
            jax.ShapeDtypeStruct((t, MLA_WIDTH), BF16), jax.ShapeDtypeStruct((t, MLA_WIDTH), BF16),
            jax.ShapeDtypeStruct((t, S5_WIDTH), BF16), jax.ShapeDtypeStruct((t, S5_WIDTH), BF16)]
    return pl.pallas_call(
        _even_in_kernel,
        grid=(t // tm,),
        in_specs=[row(D_MODEL), row(HEAD_SLOT), row(HEAD_SLOT),
                  _const_spec((D_MODEL, EV_COLS)), _const_spec((1, MLA_Q_RANK)),
                  _const_spec((1, MLA_KV_RANK)), _const_spec((MLA_Q_RANK, qk_w)),
                  _const_spec((MLA_KV_RANK, qk_w)), _const_spec((MLA_KV_RANK, MLA_WIDTH))],
        out_specs=[row(qk_w), row(qk_w), row(MLA_WIDTH), row(MLA_WIDTH), row(S5_WIDTH), row(S5_WIDTH)],
        out_shape=outs,
        compiler_params=_cparams(1),
        name="even_in_proj",
    )(xf, cos_full, sin_full, w_in_p, q_norm.reshape(1, -1), kv_norm.reshape(1, -1), wuq_p, wk_p, wv_p)


def _attn_kernel(q_ref, k_ref, v_ref, g_ref, o_ref, s_scr):
    seq = q_ref.shape[1]
    tq, tk = ATTN_TQ, ATTN_TK
    nt = (((1,), (1,)), ((), ()))

    def q_body(qi, carry):
        row0 = pl.multiple_of(qi * tq, tq)
        heads = []
        for hh in range(2):
            lanes = slice(hh * HEAD_SLOT, (hh + 1) * HEAD_SLOT)
            q = q_ref[0, pl.ds(row0, tq), lanes]

            def scores(kj, mpart):
                k = k_ref[0, pl.ds(pl.multiple_of(kj * tk, tk), tk), lanes]
                s = lax.dot_general(q, k, nt, preferred_element_type=F32)
                s_scr[kj] = s
                return jnp.maximum(mpart, jnp.maximum(s[:, :128], s[:, 128:]))

            mpart = lax.fori_loop(0, qi, scores, jnp.full((tq, 128), NEG_BIG, F32))
            kd = k_ref[0, pl.ds(row0, tk), lanes]
            s = lax.dot_general(q, kd, nt, preferred_element_type=F32)
            keep = lax.broadcasted_iota(jnp.int32, (tq, tk), 1) <= lax.broadcasted_iota(jnp.int32, (tq, tk), 0)
            s = jnp.where(keep, s, NEG_BIG)
            s_scr[qi] = s
            mpart = jnp.maximum(mpart, jnp.maximum(s[:, :128], s[:, 128:]))
            m = jnp.max(mpart, axis=1, keepdims=True)

            def weighted(kj, c):
                lpart, acc = c
                p = jnp.exp2(s_scr[kj] - m)
                lpart = lpart + (p[:, :128] + p[:, 128:])
                v = v_ref[0, pl.ds(pl.multiple_of(kj * tk, tk), tk), :]
                acc = acc + jnp.dot(p.astype(BF16), v, preferred_element_type=F32)
                return lpart, acc

            zero = jnp.zeros((tq, 128), F32)
            lpart, acc = lax.fori_loop(0, qi + 1, weighted, (zero, zero))
            heads.append(acc / jnp.sum(lpart, axis=1, keepdims=True))
        lane = lax.broadcasted_iota(jnp.int32, (tq, 128), 1)
        o = jnp.where(lane < MLA_V, heads[0], heads[1])
        o = o * g_ref[0, pl.ds(row0, tq), :].astype(F32)
        o_ref[0, pl.ds(row0, tq), :] = o.astype(BF16)
        return carry

    lax.fori_loop(0, seq // tq, q_body, 0)


def _attention(q, k, v, g, batch, seq):
    q = q.reshape(batch, seq, -1)
    k = k.reshape(batch, seq, -1)
    v = v.reshape(batch, seq, -1)
    g = g.reshape(batch, seq, -1)
    pair = 2 * HEAD_SLOT
    vw = 2 * MLA_V
    o = pl.pallas_call(
        _attn_kernel,
        grid=(batch, MLA_HEADS // 2),
        in_specs=[pl.BlockSpec((1, seq, pair), lambda b, h: (b, 0, h)),
                  pl.BlockSpec((1, seq, pair), lambda b, h: (b, 0, h)),
                  pl.BlockSpec((1, seq, vw), lambda b, h: (b, 0, h)),
                  pl.BlockSpec((1, seq, vw), lambda b, h: (b, 0, h))],
        out_specs=pl.BlockSpec((1, seq, vw), lambda b, h: (b, 0, h)),
        out_shape=jax.ShapeDtypeStruct((batch, seq, MLA_WIDTH), BF16),
        scratch_shapes=[pltpu.VMEM((seq // ATTN_TK, ATTN_TQ, ATTN_TK), F32)],
        compiler_params=_cparams(2),
        name="mla_attention",
    )(q, k, v, g)
    return o.reshape(batch * seq, MLA_WIDTH)


def _s5_matrices(a_re, a_im, log_dt, b_re, b_im, c_re, c_im):
    L, G, P, H = S5_L, S5_GROUPS, S5_STATE, S5_GROUP
    dt = jnp.exp(log_dt.astype(F32))[:, None]
    ar, ai = a_re.astype(F32), a_im.astype(F32)
    mag = jnp.exp(ar * dt)
    lr, li = mag * jnp.cos(ai * dt), mag * jnp.sin(ai * dt)
    den = ar * ar + ai * ai
    nr, ni = lr - 1.0, li
    zr = (nr * ar + ni * ai) / den
    zi = (ni * ar - nr * ai) / den
    br, bi = b_re.astype(F32), b_im.astype(F32)
    bbr = zr[..., None] * br - zi[..., None] * bi
    bbi = zr[..., None] * bi + zi[..., None] * br
    cr, ci = c_re.astype(F32), c_im.astype(F32)

    def powers(j):
        j = j.astype(F32)[:, None, None]
        pmag = jnp.exp(j * (ar * dt))
        return pmag * jnp.cos(j * (ai * dt)), pmag * jnp.sin(j * (ai * dt))

    def times_b(qr, qi):
        return (qr[..., None] * bbr - qi[..., None] * bbi, qr[..., None] * bbi + qi[..., None] * bbr)

    eye = jnp.eye(G, dtype=F32)
    pr, pi = powers(jnp.arange(L + 1))
    pbr, pbi = times_b(pr[:L], pi[:L])
    kj = jnp.einsum('gep,jgpc->jgce', cr, pbr) - jnp.einsum('gep,jgpc->jgce', ci, pbi)
    kfull = jnp.einsum('jgce,gk->jgcke', kj, eye).reshape(L, G * H, G * H)
    s_idx = jnp.arange(L)[:, None]
    t_idx = jnp.arange(L)[None, :]
    tm = jnp.where((t_idx >= s_idx)[:, :, None, None], kfull[jnp.clip(t_idx - s_idx, 0, L - 1)], 0.0)
    tmat = tm.transpose(0, 2, 1, 3).reshape(L * G * H, L * G * H)

    rev_r, rev_i = times_b(*powers(L - 1 - jnp.arange(L)))
    bm_r = jnp.einsum('sgpc,gk->sgckp', rev_r, eye).reshape(L * G * H, G * P)
    bm_i = jnp.einsum('sgpc,gk->sgckp', rev_i, eye).reshape(L * G * H, G * P)
    bmat = jnp.concatenate([bm_r, bm_i], axis=1)

    gr = cr[None] * pr[1:, :, None, :] - ci[None] * pi[1:, :, None, :]
    gi = cr[None] * pi[1:, :, None, :] + ci[None] * pr[1:, :, None, :]
    cm_r = jnp.einsum('tgep,gk->gptke', gr, eye).reshape(G * P, L * G * H)
    cm_i = jnp.einsum('tgep,gk->gptke', -gi, eye).reshape(G * P, L * G * H)
    cmat = jnp.concatenate([cm_r, cm_i], axis=0)

    lam = jnp.stack([pr[L].reshape(G * P), pi[L].reshape(G * P)])
    return bmat.astype(BF16), tmat.astype(BF16), cmat.astype(BF16), lam


def _gelu_tanh(y):
    return 0.5 * y * (1.0 + jnp.tanh(math.sqrt(2.0 / math.pi) * (y + 0.044715 * (y * y * y))))


def _s5_kernel(u_ref, gs_ref, bmat_ref, tmat_ref, cmat_ref, lam_ref, dskip_ref, wglu_ref, bglu_ref,
               o_ref, x_scr, h_scr):
    w = S5_WIDTH
    gp = S5_GROUPS * S5_STATE
    rows = u_ref.shape[0]
    per_batch = rows // S5_NB
    x_scr[...] = jnp.dot(u_ref[...], bmat_ref[...], preferred_element_type=F32)
    lam_r = lam_ref[0:1, :]
    lam_i = lam_ref[1:2, :]

    def block(i, hs):
        out = []
        for b in range(S5_NB):
            hr, hi = hs[b]
            r0 = pl.multiple_of(b * per_batch + i * 8, 8)
            x8 = x_scr[pl.ds(r0, 8), :]
            starts_r, starts_i = [], []
            for jj in range(8):
                starts_r.append(hr)
                starts_i.append(hi)
                xr = x8[jj:jj + 1, :gp]
                xi = x8[jj:jj + 1, gp:]
                hr, hi = lam_r * hr - lam_i * hi + xr, lam_r * hi + lam_i * hr + xi
            h_scr[pl.ds(r0, 8), :gp] = jnp.concatenate(starts_r, axis=0)
            h_scr[pl.ds(r0, 8), gp:] = jnp.concatenate(starts_i, axis=0)
            out.append((hr, hi))
        return tuple(out)

    zero = jnp.zeros((1, gp), F32)
    lax.fori_loop(0, per_batch // 8, block, tuple((zero, zero) for _ in range(S5_NB)))

    hs = h_scr[...].astype(BF16)
    for t in range(S5_L):
        cols = slice(t * w, (t + 1) * w)
        y = jnp.dot(u_ref[:, :(t + 1) * w], tmat_ref[:(t + 1) * w, cols], preferred_element_type=F32)
        y = y + jnp.dot(hs, cmat_ref[:, cols], preferred_element_type=F32)
        y = y + dskip_ref[...] * u_ref[:, cols].astype(F32)
        y = _gelu_tanh(y)
        z = jnp.dot(y.astype(BF16), wglu_ref[...], preferred_element_type=F32) + bglu_ref[...]
        o = (y * _sigmoid(z)) * gs_ref[:, cols].astype(F32)
        o_ref[:, cols] = o.astype(BF16)


def _s5(u, gs, batch, seq, a_re, a_im, log_dt, b_re, b_im, c_re, c_im, d_skip, w_glu, b_glu):
    t = batch * seq
    lw = S5_L * S5_WIDTH
    gp2 = 2 * S5_GROUPS * S5_STATE
    bmat, tmat, cmat, lam = _s5_matrices(a_re, a_im, log_dt, b_re, b_im, c_re, c_im)
    rows = S5_NB * seq // S5_L
    u2 = u.reshape(t // S5_L, lw)
    gs2 = gs.reshape(t // S5_L, lw)
    blk = pl.BlockSpec((rows, lw), lambda i: (i, 0))
    o = pl.pallas_call(
        _s5_kernel,
        grid=(batch // S5_NB,),
        in_specs=[blk, blk, _const_spec((lw, gp2), True), _const_spec((lw, lw), True),
                  _const_spec((gp2, lw), True), _const_spec((2, gp2 // 2)),
                  _const_spec((1, S5_WIDTH)), _const_spec((S5_WIDTH, S5_WIDTH)), _const_spec((1, S5_WIDTH))],
        out_specs=blk,
        out_shape=jax.ShapeDtypeStruct((t // S5_L, lw), BF16),
        scratch_shapes=[pltpu.VMEM((rows, gp2), F32), pltpu.VMEM((rows, gp2), F32)],
        compiler_params=_cparams(1),
        name="s5_mixer",
    )(u2, gs2, bmat, tmat, cmat, lam, d_skip.reshape(1, -1).astype(F32), w_glu.astype(BF16),
      b_glu.reshape(1, -1).astype(F32))
    return o.reshape(t, S5_WIDTH)


def _out_ln_kernel(n_act, x_ref, *refs):
    acts = refs[:n_act]
    ws = refs[n_act:2 * n_act]
    g_ref, b_ref, o_ref = refs[2 * n_act:]
    y = jnp.dot(acts[0][...], ws[0][...], preferred_element_type=F32)
    for a, w in zip(acts[1:], ws[1:]):
        y = y + jnp.dot(a[...], w[...], preferred_element_type=F32)
    z = DEEPNORM_ALPHA * x_ref[...] + y
    mu = jnp.mean(z, axis=-1, keepdims=True)
    zc = z - mu
    var = jnp.mean(zc * zc, axis=-1, keepdims=True)
    o_ref[...] = zc * lax.rsqrt(var + LN_EPS) * g_ref[...] + b_ref[...]


def _out_ln(xf, acts, ws, ln_g, ln_b):
    t = xf.shape[0]
    tm = TOKEN_TILE
    n = len(acts)

    def row(c):
        return pl.BlockSpec((tm, c), lambda i: (i, 0))

    return pl.pallas_call(
        functools.partial(_out_ln_kernel, n),
        grid=(t // tm,),
        in_specs=[row(D_MODEL)] + [row(a.shape[1]) for a in acts] + [_const_spec(w.shape) for w in ws]
                 + [_const_spec((1, D_MODEL)), _const_spec((1, D_MODEL))],
        out_specs=row(D_MODEL),
        out_shape=jax.ShapeDtypeStruct((t, D_MODEL), F32),
        compiler_params=_cparams(1),
        name="out_proj_layernorm",
    )(xf, *acts, *ws, ln_g.reshape(1, -1), ln_b.reshape(1, -1))


OD_Q = 0
OD_K = OD_Q + GLA_KEY_WIDTH
OD_V = OD_K + GLA_KEY_WIDTH
OD_GL = OD_V + GLA_WIDTH
OD_G = OD_GL + 128
OD_COLS = OD_G + GLA_WIDTH


def _odd_in_kernel(x_ref, w_ref, wgk_ref, bgk_ref, q_ref, k_ref, v_ref, la_ref, g_ref):
    xb = x_ref[...].astype(BF16)

    def proj(lo, hi):
        return jnp.dot(xb, w_ref[:, lo:hi], preferred_element_type=F32)

    q_ref[...] = proj(OD_Q, OD_K).astype(BF16)
    k_ref[...] = proj(OD_K, OD_V).astype(BF16)
    v_ref[...] = proj(OD_V, OD_GL).astype(BF16)
    z = jnp.dot(proj(OD_GL, OD_G).astype(BF16), wgk_ref[...], preferred_element_type=F32) + bgk_ref[...]
    log_sig = -(jnp.maximum(-z, 0.0) + jnp.log1p(jnp.exp(-jnp.abs(z))))
    la_ref[...] = log_sig / GLA_GATE_NORM
    g_ref[...] = _silu(proj(OD_G, OD_COLS)).astype(BF16)


def _odd_in(xf, w_in, w_gk2, b_gk):
    t = xf.shape[0]
    tm = TOKEN_TILE
    q, k, v, gl, g = jnp.split(w_in, [OD_K, OD_V, OD_GL, OD_GL + GLA_GATE_RANK], axis=1)
    gl = jnp.pad(gl, ((0, 0), (0, 128 - GLA_GATE_RANK)))
    w_p = jnp.concatenate([q, k, v, gl, g], axis=1).astype(BF16)
    wgk_p = jnp.pad(w_gk2, ((0, 128 - GLA_GATE_RANK), (0, 0))).astype(BF16)

    def row(c):
        return pl.BlockSpec((tm, c), lambda i: (i, 0))

    outs = [jax.ShapeDtypeStruct((t, GLA_KEY_WIDTH), BF16), jax.ShapeDtypeStruct((t, GLA_KEY_WIDTH), BF16),
            jax.ShapeDtypeStruct((t, GLA_WIDTH), BF16), jax.ShapeDtypeStruct((t, GLA_KEY_WIDTH), F32),
            jax.ShapeDtypeStruct((t, GLA_WIDTH), BF16)]
    return pl.pallas_call(
        _odd_in_kernel,
        grid=(t // tm,),
        in_specs=[row(D_MODEL), _const_spec((D_MODEL, OD_COLS)), _const_spec((128, GLA_KEY_WIDTH)),
                  _const_spec((1, GLA_KEY_WIDTH))],
        out_specs=[row(GLA_KEY_WIDTH), row(GLA_KEY_WIDTH), row(GLA_WIDTH), row(GLA_KEY_WIDTH), row(GLA_WIDTH)],
        out_shape=outs,
        compiler_params=_cparams(1),
        name="odd_in_proj",
    )(xf, w_p, wgk_p, b_gk.reshape(1, -1).astype(F32))


def _gla_kernel(q_ref, k_ref, v_ref, la_ref, g_ref, gn_ref, o_ref, st_scr):
    c = GLA_CHUNK
    seq = q_ref.shape[1]
    nt = (((1,), (1,)), ((), ()))
    tn = (((0,), (0,)), ((), ()))
    st_scr[...] = jnp.zeros_like(st_scr)
    ri = lax.broadcasted_iota(jnp.int32, (c, c), 0)
    ci = lax.broadcasted_iota(jnp.int32, (c, c), 1)
    lower = ri >= ci
    tri = jnp.where(lower, 1.0, 0.0).astype(BF16)
    scale = GLA_DK ** -0.5

    def chunk(n, carry):
        r0 = pl.multiple_of(n * c, c)
        rows = pl.ds(r0, c)
        ga = la_ref[0, rows, :]
        ga_hi = ga.astype(BF16)
        ga_lo = (ga - ga_hi.astype(F32)).astype(BF16)
        bc = (jnp.dot(tri, ga_hi, preferred_element_type=F32)
              + jnp.dot(tri, ga_lo, preferred_element_type=F32))
        b_last = bc[c - 1:c, :]
        q = q_ref[0, rows, :].astype(F32) * scale
        k = k_ref[0, rows, :].astype(F32)
        qd = (q * jnp.exp(bc)).astype(BF16)
        k_inv = (k * jnp.exp(-bc)).astype(BF16)
        k_end = (k * jnp.exp(b_last - bc)).astype(BF16)
        v = v_ref[0, rows, :]
        att = lax.dot_general(qd, k_inv, nt, preferred_element_type=F32)
        att = jnp.where(lower, att, 0.0).astype(BF16)
        st = st_scr[...]
        o = (jnp.dot(att, v, preferred_element_type=F32)
             + lax.dot_general(qd, st.astype(BF16), nt, preferred_element_type=F32))
        st_scr[...] = st * jnp.exp(b_last) + lax.dot_general(v, k_end, tn, preferred_element_type=F32)
        o = _rms(o, gn_ref[...]) * g_ref[0, rows, :].astype(F32)
        o_ref[0, rows, :] = o.astype(BF16)
        return carry

    lax.fori_loop(0, seq // c, chunk, 0)


def _gla(q, k, v, la, g, g_norm, batch, seq):
    q = q.reshape(batch, seq, -1)
    k = k.reshape(batch, seq, -1)
    v = v.reshape(batch, seq, -1)
    la = la.reshape(batch, seq, -1)
    g = g.reshape(batch, seq, -1)

    def spec(c):
        return pl.BlockSpec((1, seq, c), lambda b, h: (b, 0, h))

    o = pl.pallas_call(
        _gla_kernel,
        grid=(batch, GLA_HEADS),
        in_specs=[spec(GLA_DK), spec(GLA_DK), spec(GLA_DV), spec(GLA_DK), spec(GLA_DV),
                  _const_spec((1, GLA_DV))],
        out_specs=spec(GLA_DV),
        out_shape=jax.ShapeDtypeStruct((batch, seq, GLA_WIDTH), BF16),
        scratch_shapes=[pltpu.VMEM((GLA_DV, GLA_DK), F32)],
        compiler_params=_cparams(2),
        name="gla_mixer",
    )(q, k, v, la, g, g_norm.reshape(1, -1).astype(F32))
    return o.reshape(batch * seq, GLA_WIDTH)


def kernel(x, positions, ln_g, ln_b, even_w_in, mla_q_norm, mla_kv_norm, mla_w_uq, mla_w_ukv, s5_a_re, s5_a_im, s5_log_dt, s5_b_re, s5_b_im, s5_c_re, s5_c_im, s5_d, s5_w_glu, s5_b_glu, even_w_out, odd_w_in, gla_w_gk2, gla_b_gk, gla_g_norm, odd_w_out):
    batch, seq, _ = x.shape
    xf = x.reshape(batch * seq, D_MODEL)
    cos_full, sin_full = _rope_tables(positions)
    for layer in range(DEPTH):
        j = layer // 2
        if layer % 2 == 0:
            q, k, v, gm, u, gs = _even_in(xf, cos_full, sin_full, even_w_in[j], mla_q_norm[j],
                                          mla_kv_norm[j], mla_w_uq[j], mla_w_ukv[j])
            o_mla = _attention(q, k, v, gm, batch, seq)
            o_s5 = _s5(u, gs, batch, seq, s5_a_re[j], s5_a_im[j], s5_log_dt[j], s5_b_re[j], s5_b_im[j],
                       s5_c_re[j], s5_c_im[j], s5_d[j], s5_w_glu[j], s5_b_glu[j])
            w_out = even_w_out[j].astype(BF16)
            xf = _out_ln(xf, [o_mla, o_s5], [w_out[:MLA_WIDTH], w_out[MLA_WIDTH:]], ln_g[layer], ln_b[layer])
        else:
            q, k, v, la, g = _odd_in(xf, odd_w_in[j], gla_w_gk2[j], gla_b_gk[j])
            o = _gla(q, k, v, la, g, gla_g_norm[j], batch, seq)
            xf = _out_ln(xf, [o], [odd_w_out[j].astype(BF16)], ln_g[layer], ln_b[layer])
    return xf.reshape(batch, seq, D_MODEL)
```

```python
import functools
import math

import jax
import jax.numpy as jnp
from jax import lax
from jax.experimental import pallas as pl
from jax.experimental.pallas import tpu as pltpu

F32 = jnp.float32
BF16 = jnp.bfloat16

D_MODEL = 1024
DEPTH = 2

MLA_HEADS = 8
MLA_NOPE = 64
MLA_ROPE = 32
MLA_V = 64
MLA_Q_RANK = 256
MLA_KV_RANK = 128
MLA_WIDTH = MLA_HEADS * MLA_V
ROPE_BASE = 10000.0
HEAD_SLOT = 128

S5_WIDTH = 256
S5_GROUP = 16
S5_GROUPS = S5_WIDTH // S5_GROUP
S5_STATE = 64
S5_L = 8
S5_NB = 1

GLA_HEADS = 4
GLA_KEY_WIDTH = D_MODEL // 2
GLA_WIDTH = D_MODEL
GLA_DK = GLA_KEY_WIDTH // GLA_HEADS
GLA_DV = GLA_WIDTH // GLA_HEADS
GLA_GATE_RANK = 16
GLA_GATE_NORM = 16.0
GLA_CHUNK = 64

DEEPNORM_ALPHA = (2 * DEPTH) ** 0.25
LN_EPS = 1e-5
RMS_EPS = 1e-6

TOKEN_TILE = 512
ATTN_T = 256
ATTN_GROUP = 4
NEG_BIG = -1e30
VMEM_LIMIT = 56 * 1024 * 1024


def _cparams(n_axes):
    return pltpu.CompilerParams(dimension_semantics=("arbitrary",) * n_axes,
                                vmem_limit_bytes=VMEM_LIMIT)


def _const_spec(shape, single=False):
    nd = len(shape)
    if single:
        return pl.BlockSpec(shape, lambda *_: (0,) * nd, pipeline_mode=pl.Buffered(1))
    return pl.BlockSpec(shape, lambda *_: (0,) * nd)


def _sigmoid(x):
    return 1.0 / (1.0 + jnp.exp(-x))


def _silu(x):
    return x * _sigmoid(x)


def _rms(x, g):
    return (x * lax.rsqrt(jnp.mean(x * x, axis=-1, keepdims=True) + RMS_EPS)) * g


def _rope_kernel(pos_ref, invf_ref, cos_ref, sin_ref, nsin_ref):
    ang = pos_ref[...].astype(F32) * invf_ref[...]
    c = jnp.cos(ang)
    s = jnp.sin(ang)
    cos_ref[...] = c
    sin_ref[...] = s
    nsin_ref[...] = -s


def _rope_tables(positions):
    t = positions.size
    half = MLA_ROPE // 2
    rows = t * half // 128
    pos_rep = jnp.repeat(positions.reshape(t), half).reshape(rows, 128)
    inv_freq = ROPE_BASE ** (-jnp.arange(half, dtype=F32) / half)
    invf = jnp.tile(inv_freq, 128 // half).reshape(1, 128)
    rb = 1024
    out = jax.ShapeDtypeStruct((rows, 128), F32)
    cos, sin, nsin = pl.pallas_call(
        _rope_kernel,
        grid=(rows // rb,),
        in_specs=[pl.BlockSpec((rb, 128), lambda i: (i, 0)), _const_spec((1, 128))],
        out_specs=[pl.BlockSpec((rb, 128), lambda i: (i, 0))] * 3,
        out_shape=[out, out, out],
        compiler_params=_cparams(1),
        name="rope_tables",
    )(pos_rep, invf)
    cos, sin, nsin = (a.reshape(t, half) for a in (cos, sin, nsin))
    ones = jnp.ones((t, MLA_NOPE), F32)
    zeros = jnp.zeros((t, MLA_NOPE), F32)
    pad = HEAD_SLOT - MLA_NOPE - MLA_ROPE
    cos_full = jnp.concatenate([ones, cos, cos, ones[:, :pad]], axis=1)
    sin_full = jnp.concatenate([zeros, nsin, sin, zeros[:, :pad]], axis=1)
    return cos_full, sin_full


EV_CQ = 0
EV_CKV = EV_CQ + MLA_Q_RANK
EV_KPE = EV_CKV + MLA_KV_RANK
EV_GM = EV_KPE + HEAD_SLOT
EV_U = EV_GM + MLA_WIDTH
EV_GS = EV_U + S5_WIDTH
EV_COLS = EV_GS + S5_WIDTH
Q_SCALE = (MLA_NOPE + MLA_ROPE) ** -0.5 * math.log2(math.e)


def _rotate_pairs(x):
    n = x.shape[-1]
    lane = lax.broadcasted_iota(jnp.int32, x.shape, 1)
    lo = (lane & 112) == 64
    return jnp.where(lo, pltpu.roll(x, n - MLA_ROPE // 2, 1), pltpu.roll(x, MLA_ROPE // 2, 1))


def _even_in_kernel(x_ref, cos_ref, sin_ref, w_in_ref, qn_ref, kvn_ref, wuq_ref, wk_ref, wv_ref,
                    q_ref, k_ref, v_ref, gm_ref, u_ref, gs_ref):
    xb = x_ref[...].astype(BF16)
    h = jnp.dot(xb, w_in_ref[...], preferred_element_type=F32)
    cos = cos_ref[...]
    sin = sin_ref[...]

    cqn = _rms(h[:, EV_CQ:EV_CKV], qn_ref[...]).astype(BF16)
    q = jnp.dot(cqn, wuq_ref[...], preferred_element_type=F32)
    q = q * jnp.tile(cos, (1, MLA_HEADS)) + _rotate_pairs(q) * jnp.tile(sin, (1, MLA_HEADS))
    q_ref[...] = (q * Q_SCALE).astype(BF16)

    kpe = h[:, EV_KPE:EV_GM]
    kpe = kpe * cos + _rotate_pairs(kpe) * sin
    ckvn = _rms(h[:, EV_CKV:EV_KPE], kvn_ref[...]).astype(BF16)
    k = jnp.dot(ckvn, wk_ref[...], preferred_element_type=F32) + jnp.tile(kpe, (1, MLA_HEADS))
    k_ref[...] = k.astype(BF16)
    v_ref[...] = jnp.dot(ckvn, wv_ref[...], preferred_element_type=F32).astype(BF16)

    gm_ref[...] = _silu(h[:, EV_GM:EV_U]).astype(BF16)
    u_ref[...] = h[:, EV_U:EV_GS].astype(BF16)
    gs_ref[...] = _silu(h[:, EV_GS:EV_COLS]).astype(BF16)


def _even_in(xf, cos_full, sin_full, w_in, q_norm, kv_norm, w_uq, w_ukv):
    t = xf.shape[0]
    tm = TOKEN_TILE
    pad = HEAD_SLOT - MLA_NOPE - MLA_ROPE
    cq, ckv, kr, gm, u, gs = jnp.split(
        w_in, [MLA_Q_RANK, MLA_Q_RANK + MLA_KV_RANK, MLA_Q_RANK + MLA_KV_RANK + MLA_ROPE,
               MLA_Q_RANK + MLA_KV_RANK + MLA_ROPE + MLA_WIDTH,
               MLA_Q_RANK + MLA_KV_RANK + MLA_ROPE + MLA_WIDTH + S5_WIDTH], axis=1)
    kr = jnp.pad(kr, ((0, 0), (MLA_NOPE, pad)))
    w_in_p = jnp.concatenate([cq, ckv, kr, gm, u, gs], axis=1).astype(BF16)
    wuq_p = jnp.pad(w_uq.reshape(MLA_Q_RANK, MLA_HEADS, MLA_NOPE + MLA_ROPE),
                    ((0, 0), (0, 0), (0, pad))).reshape(MLA_Q_RANK, MLA_HEADS * HEAD_SLOT).astype(BF16)
    wkv = w_ukv.reshape(MLA_KV_RANK, MLA_HEADS, MLA_NOPE + MLA_V)
    wk_p = jnp.pad(wkv[:, :, :MLA_NOPE], ((0, 0), (0, 0), (0, HEAD_SLOT - MLA_NOPE))
                   ).reshape(MLA_KV_RANK, MLA_HEADS * HEAD_SLOT).astype(BF16)
    wv_p = wkv[:, :, MLA_NOPE:].reshape(MLA_KV_RANK, MLA_WIDTH).astype(BF16)

    def row(n):
        return pl.BlockSpec((tm, n), lambda i: (i, 0))

    qk_w = MLA_HEADS * HEAD_SLOT
    outs = [jax.ShapeDtypeStruct((t, qk_w), BF16), jax.ShapeDtypeStruct((t, qk_w), BF16),
            jax.ShapeDtypeStruct((t, MLA_WIDTH), BF16), jax.ShapeDtypeStruct((t, MLA_WIDTH), BF16),
            jax.ShapeDtypeStruct((t, S5_WIDTH), BF16), jax.ShapeDtypeStruct((t, S5_WIDTH), BF16)]
    return pl.pallas_call(
        _even_in_kernel,
        grid=(t // tm,),
        in_specs=[row(D_MODEL), row(HEAD_SLOT), row(HEAD_SLOT),
                  _const_spec((D_MODEL, EV_COLS)), _const_spec((1, MLA_Q_RANK)),
                  _const_spec((1, MLA_KV_RANK)), _const_spec((MLA_Q_RANK, qk_w)),
                  _const_spec((MLA_KV_RANK, qk_w)), _const_spec((MLA_KV_RANK, MLA_WIDTH))],
        out_specs=[row(qk_w), row(qk_w), row(MLA_WIDTH), row(MLA_WIDTH), row(S5_WIDTH), row(S5_WIDTH)],
        out_shape=outs,
        compiler_params=_cparams(1),
        name="even_in_proj",
    )(xf, cos_full, sin_full, w_in_p, q_norm.reshape(1, -1), kv_norm.reshape(1, -1), wuq_p, wk_p, wv_p)


def _attn_kernel(q_ref, k_ref, v_ref, g_ref, o_ref, s_scr):
    seq = q_ref.shape[1]
    t = ATTN_T
    n = seq // t
    nt = (((1,), (1,)), ((), ()))
    hh = pl.program_id(2)

    @pl.when(hh == 0)
    def _():
        o_ref[...] = jnp.zeros_like(o_ref)

    keep = lax.broadcasted_iota(jnp.int32, (t, t), 1) <= lax.broadcasted_iota(jnp.int32, (t, t), 0)
    for kj in range(n):
        r0 = kj * t
        k = k_ref[0, r0:r0 + t, :]
        sd = lax.dot_general(q_ref[0, r0:r0 + t, :], k, nt, preferred_element_type=F32)
        s_scr[r0:r0 + t, r0:r0 + t] = jnp.where(keep, sd, NEG_BIG)
        if r0 + t < seq:
            s_scr[r0 + t:, r0:r0 + t] = lax.dot_general(q_ref[0, r0 + t:, :], k, nt,
                                                         preferred_element_type=F32)

    mine = (lax.broadcasted_iota(jnp.int32, (t, ATTN_GROUP * MLA_V), 1) // MLA_V) == hh
    for qi in range(n):
        r0 = qi * t
        lk = r0 + t
        s = s_scr[r0:r0 + t, :lk]
        m = jnp.max(s, axis=1, keepdims=True)
        p = jnp.exp2(s - m)
        l = jnp.sum(p, axis=1, keepdims=True)
        acc = jnp.dot(p.astype(BF16), v_ref[0, :lk, :], preferred_element_type=F32)
        o = (acc * (1.0 / l)) * g_ref[0, r0:r0 + t, :].astype(F32)
        o_ref[0, r0:r0 + t, :] = jnp.where(mine, o.astype(BF16), o_ref[0, r0:r0 + t, :])


def _attention(q, k, v, g, batch, seq):
    q = q.reshape(batch, seq, -1)
    k = k.reshape(batch, seq, -1)
    v = v.reshape(batch, seq, -1)
    g = g.reshape(batch, seq, -1)
    gw = ATTN_GROUP * MLA_V
    head = pl.BlockSpec((1, seq, HEAD_SLOT), lambda b, gi, hh: (b, 0, gi * ATTN_GROUP + hh))
    group = pl.BlockSpec((1, seq, gw), lambda b, gi, hh: (b, 0, gi))
    o = pl.pallas_call(
        _attn_kernel,
        grid=(batch, MLA_HEADS // ATTN_GROUP, ATTN_GROUP),
        in_specs=[head, head, group, group],
        out_specs=group,
        out_shape=jax.ShapeDtypeStruct((batch, seq, MLA_WIDTH), BF16),
        scratch_shapes=[pltpu.VMEM((seq, seq), F32)],
        compiler_params=_cparams(3),
        name="mla_attention",
    )(q, k, v, g)
    return o.reshape(batch * seq, MLA_WIDTH)


def _s5_matrices(a_re, a_im, log_dt, b_re, b_im, c_re, c_im):
    L, G, P, H = S5_L, S5_GROUPS, S5_STATE, S5_GROUP
    dt = jnp.exp(log_dt.astype(F32))[:, None]
    ar, ai = a_re.astype(F32), a_im.astype(F32)
    mag = jnp.exp(ar * dt)
    lr, li = mag * jnp.cos(ai * dt), mag * jnp.sin(ai * dt)
    den = ar * ar + ai * ai
    nr, ni = lr - 1.0, li
    zr = (nr * ar + ni * ai) / den
    zi = (ni * ar - nr * ai) / den
    br, bi = b_re.astype(F32), b_im.astype(F32)
    bbr = zr[..., None] * br - zi[..., None] * bi
    bbi = zr[..., None] * bi + zi[..., None] * br
    cr, ci = c_re.astype(F32), c_im.astype(F32)

    def powers(j):
        j = j.astype(F32)[:, None, None]
        pmag = jnp.exp(j * (ar * dt))
        return pmag * jnp.cos(j * (ai * dt)), pmag * jnp.sin(j * (ai * dt))

    def times_b(qr, qi):
        return (qr[..., None] * bbr - qi[..., None] * bbi, qr[..., None] * bbi + qi[..., None] * bbr)

    eye = jnp.eye(G, dtype=F32)
    pr, pi = powers(jnp.arange(L + 1))
    pbr, pbi = times_b(pr[:L], pi[:L])
    kj = jnp.einsum('gep,jgpc->jgce', cr, pbr) - jnp.einsum('gep,jgpc->jgce', ci, pbi)
    kfull = jnp.einsum('jgce,gk->jgcke', kj, eye).reshape(L, G * H, G * H)
    s_idx = jnp.arange(L)[:, None]
    t_idx = jnp.arange(L)[None, :]
    tm = jnp.where((t_idx >= s_idx)[:, :, None, None], kfull[jnp.clip(t_idx - s_idx, 0, L - 1)], 0.0)
    tmat = tm.transpose(0, 2, 1, 3).reshape(L * G * H, L * G * H)

    rev_r, rev_i = times_b(*powers(L - 1 - jnp.arange(L)))
    bm_r = jnp.einsum('sgpc,gk->sgckp', rev_r, eye).reshape(L * G * H, G * P)
    bm_i = jnp.einsum('sgpc,gk->sgckp', rev_i, eye).reshape(L * G * H, G * P)
    bmat = jnp.concatenate([bm_r, bm_i], axis=1)

    gr = cr[None] * pr[1:, :, None, :] - ci[None] * pi[1:, :, None, :]
    gi = cr[None] * pi[1:, :, None, :] + ci[None] * pr[1:, :, None, :]
    cm_r = jnp.einsum('tgep,gk->gptke', gr, eye).reshape(G * P, L * G * H)
    cm_i = jnp.einsum('tgep,gk->gptke', -gi, eye).reshape(G * P, L * G * H)
    cmat = jnp.concatenate([cm_r, cm_i], axis=0)

    lam = jnp.stack([pr[L].reshape(G * P), pi[L].reshape(G * P)])
    return bmat.astype(BF16), tmat.astype(BF16), cmat.astype(BF16), lam


def _gelu_tanh(y):
    return 0.5 * y * (1.0 + jnp.tanh(math.sqrt(2.0 / math.pi) * (y + 0.044715 * (y * y * y))))


def _s5_kernel(u_ref, gs_ref, bmat_ref, tmat_ref, cmat_ref, lam_ref, dskip_ref, wglu_ref, bglu_ref,
               o_ref, x_scr, h_scr):
    w = S5_WIDTH
    gp = S5_GROUPS * S5_STATE
    rows = u_ref.shape[0]
    per_batch = rows // S5_NB
    x_scr[...] = jnp.dot(u_ref[...], bmat_ref[...], preferred_element_type=F32)
    lam_r = lam_ref[0:1, :]
    lam_i = lam_ref[1:2, :]

    def block(i, hs):
        out = []
        for b in range(S5_NB):
            hr, hi = hs[b]
            r0 = pl.multiple_of(b * per_batch + i * 8, 8)
            x8 = x_scr[pl.ds(r0, 8), :]
            starts_r, starts_i = [], []
            for jj in range(8):
                starts_r.append(hr)
                starts_i.append(hi)
                xr = x8[jj:jj + 1, :gp]
                xi = x8[jj:jj + 1, gp:]
                hr, hi = lam_r * hr - lam_i * hi + xr, lam_r * hi + lam_i * hr + xi
            h_scr[pl.ds(r0, 8), :gp] = jnp.concatenate(starts_r, axis=0)
            h_scr[pl.ds(r0, 8), gp:] = jnp.concatenate(starts_i, axis=0)
            out.append((hr, hi))
        return tuple(out)

    zero = jnp.zeros((1, gp), F32)
    lax.fori_loop(0, per_batch // 8, block, tuple((zero, zero) for _ in range(S5_NB)))

    hs = h_scr[...].astype(BF16)
    for t in range(S5_L):
        cols = slice(t * w, (t + 1) * w)
        y = jnp.dot(u_ref[:, :(t + 1) * w], tmat_ref[:(t + 1) * w, cols], preferred_element_type=F32)
        y = y + jnp.dot(hs, cmat_ref[:, cols], preferred_element_type=F32)
        y = y + dskip_ref[...] * u_ref[:, cols].astype(F32)
        y = _gelu_tanh(y)
        z = jnp.dot(y.astype(BF16), wglu_ref[...], preferred_element_type=F32) + bglu_ref[...]
        o = (y * _sigmoid(z)) * gs_ref[:, cols].astype(F32)
        o_ref[:, cols] = o.astype(BF16)


def _s5(u, gs, batch, seq, a_re, a_im, log_dt, b_re, b_im, c_re, c_im, d_skip, w_glu, b_glu):
    t = batch * seq
    lw = S5_L * S5_WIDTH
    gp2 = 2 * S5_GROUPS * S5_STATE
    bmat, tmat, cmat, lam = _s5_matrices(a_re, a_im, log_dt, b_re, b_im, c_re, c_im)
    rows = S5_NB * seq // S5_L
    u2 = u.reshape(t // S5_L, lw)
    gs2 = gs.reshape(t // S5_L, lw)
    blk = pl.BlockSpec((rows, lw), lambda i: (i, 0))
    o = pl.pallas_call(
        _s5_kernel,
        grid=(batch // S5_NB,),
        in_specs=[blk, blk, _const_spec((lw, gp2), True), _const_spec((lw, lw), True),
                  _const_spec((gp2, lw), True), _const_spec((2, gp2 // 2)),
                  _const_spec((1, S5_WIDTH)), _const_spec((S5_WIDTH, S5_WIDTH)), _const_spec((1, S5_WIDTH))],
        out_specs=blk,
        out_shape=jax.ShapeDtypeStruct((t // S5_L, lw), BF16),
        scratch_shapes=[pltpu.VMEM((rows, gp2), F32), pltpu.VMEM((rows, gp2), F32)],
        compiler_params=_cparams(1),
        name="s5_mixer",
    )(u2, gs2, bmat, tmat, cmat, lam, d_skip.reshape(1, -1).astype(F32), w_glu.astype(BF16),
      b_glu.reshape(1, -1).astype(F32))
    return o.reshape(t, S5_WIDTH)


def _out_ln_kernel(n_act, x_ref, *refs):
    acts = refs[:n_act]
    ws = refs[n_act:2 * n_act]
    g_ref, b_ref, o_ref = refs[2 * n_act:]
    y = jnp.dot(acts[0][...], ws[0][...], preferred_element_type=F32)
    for a, w in zip(acts[1:], ws[1:]):
        y = y + jnp.dot(a[...], w[...], preferred_element_type=F32)
    z = DEEPNORM_ALPHA * x_ref[...] + y
    mu = jnp.mean(z, axis=-1, keepdims=True)
    zc = z - mu
    var = jnp.mean(zc * zc, axis=-1, keepdims=True)
    o_ref[...] = zc * lax.rsqrt(var + LN_EPS) * g_ref[...] + b_ref[...]


def _out_ln(xf, acts, ws, ln_g, ln_b):
    t = xf.shape[0]
    tm = TOKEN_TILE
    n = len(acts)

    def row(c):
        return pl.BlockSpec((tm, c), lambda i: (i, 0))

    return pl.pallas_call(
        functools.partial(_out_ln_kernel, n),
        grid=(t // tm,),
        in_specs=[row(D_MODEL)] + [row(a.shape[1]) for a in acts] + [_const_spec(w.shape) for w in ws]
                 + [_const_spec((1, D_MODEL)), _const_spec((1, D_MODEL))],
        out_specs=row(D_MODEL),
        out_shape=jax.ShapeDtypeStruct((t, D_MODEL), F32),
        compiler_params=_cparams(1),
        name="out_proj_layernorm",
    )(xf, *acts, *ws, ln_g.reshape(1, -1), ln_b.reshape(1, -1))


OD_Q = 0
OD_K = OD_Q + GLA_KEY_WIDTH
OD_V = OD_K + GLA_KEY_WIDTH
OD_GL = OD_V + GLA_WIDTH
OD_G = OD_GL + 128
OD_COLS = OD_G + GLA_WIDTH


def _odd_in_kernel(x_ref, w_ref, wgk_ref, bgk_ref, q_ref, k_ref, v_ref, la_ref, g_ref):
    xb = x_ref[...].astype(BF16)

    def proj(lo, hi):
        return jnp.dot(xb, w_ref[:, lo:hi], preferred_element_type=F32)

    q_ref[...] = proj(OD_Q, OD_K).astype(BF16)
    k_ref[...] = proj(OD_K, OD_V).astype(BF16)
    v_ref[...] = proj(OD_V, OD_GL).astype(BF16)
    z = jnp.dot(proj(OD_GL, OD_G).astype(BF16), wgk_ref[...], preferred_element_type=F32) + bgk_ref[...]
    log_sig = -(jnp.maximum(-z, 0.0) + jnp.log1p(jnp.exp(-jnp.abs(z))))
    la_ref[...] = log_sig / GLA_GATE_NORM
    g_ref[...] = _silu(proj(OD_G, OD_COLS)).astype(BF16)


def _odd_in(xf, w_in, w_gk2, b_gk):
    t = xf.shape[0]
    tm = TOKEN_TILE
    q, k, v, gl, g = jnp.split(w_in, [OD_K, OD_V, OD_GL, OD_GL + GLA_GATE_RANK], axis=1)
    gl = jnp.pad(gl, ((0, 0), (0, 128 - GLA_GATE_RANK)))
    w_p = jnp.concatenate([q, k, v, gl, g], axis=1).astype(BF16)
    wgk_p = jnp.pad(w_gk2, ((0, 128 - GLA_GATE_RANK), (0, 0))).astype(BF16)

    def row(c):
        return pl.BlockSpec((tm, c), lambda i: (i, 0))

    outs = [jax.ShapeDtypeStruct((t, GLA_KEY_WIDTH), BF16), jax.ShapeDtypeStruct((t, GLA_KEY_WIDTH), BF16),
            jax.ShapeDtypeStruct((t, GLA_WIDTH), BF16), jax.ShapeDtypeStruct((t, GLA_KEY_WIDTH), F32),
            jax.ShapeDtypeStruct((t, GLA_WIDTH), BF16)]
    return pl.pallas_call(
        _odd_in_kernel,
        grid=(t // tm,),
        in_specs=[row(D_MODEL), _const_spec((D_MODEL, OD_COLS)), _const_spec((128, GLA_KEY_WIDTH)),
                  _const_spec((1, GLA_KEY_WIDTH))],
        out_specs=[row(GLA_KEY_WIDTH), row(GLA_KEY_WIDTH), row(GLA_WIDTH), row(GLA_KEY_WIDTH), row(GLA_WIDTH)],
        out_shape=outs,
        compiler_params=_cparams(1),
        name="odd_in_proj",
    )(xf, w_p, wgk_p, b_gk.reshape(1, -1).astype(F32))


def _gla_kernel(q_ref, k_ref, v_ref, la_ref, g_ref, gn_ref, o_ref, st_scr):
    c = GLA_CHUNK
    seq = q_ref.shape[1]
    nt = (((1,), (1,)), ((), ()))
    tn = (((0,), (0,)), ((), ()))
    st_scr[...] = jnp.zeros_like(st_scr)
    ri = lax.broadcasted_iota(jnp.int32, (c, c), 0)
    ci = lax.broadcasted_iota(jnp.int32, (c, c), 1)
    lower = ri >= ci
    tri = jnp.where(lower, 1.0, 0.0).astype(BF16)
    scale = GLA_DK ** -0.5

    def chunk(n, carry):
        r0 = pl.multiple_of(n * c, c)
        rows = pl.ds(r0, c)
        ga = la_ref[0, rows, :]
        ga_hi = ga.astype(BF16)
        ga_lo = (ga - ga_hi.astype(F32)).astype(BF16)
        bc = (jnp.dot(tri, ga_hi, preferred_element_type=F32)
              + jnp.dot(tri, ga_lo, preferred_element_type=F32))
        b_last = bc[c - 1:c, :]
        q = q_ref[0, rows, :].astype(F32) * scale
        k = k_ref[0, rows, :].astype(F32)
        qd = (q * jnp.exp(bc)).astype(BF16)
        k_inv = (k * jnp.exp(-bc)).astype(BF16)
        k_end = (k * jnp.exp(b_last - bc)).astype(BF16)
        v = v_ref[0, rows, :]
        att = lax.dot_general(qd, k_inv, nt, preferred_element_type=F32)
        att = jnp.where(lower, att, 0.0).astype(BF16)
        st = st_scr[...]
        o = (jnp.dot(att, v, preferred_element_type=F32)
             + lax.dot_general(qd, st.astype(BF16), nt, preferred_element_type=F32))
        st_scr[...] = st * jnp.exp(b_last) + lax.dot_general(v, k_end, tn, preferred_element_type=F32)
        o = _rms(o, gn_ref[...]) * g_ref[0, rows, :].astype(F32)
        o_ref[0, rows, :] = o.astype(BF16)
        return carry

    lax.fori_loop(0, seq // c, chunk, 0)


def _gla(q, k, v, la, g, g_norm, batch, seq):
    q = q.reshape(batch, seq, -1)
    k = k.reshape(batch, seq, -1)
    v = v.reshape(batch, seq, -1)
    la = la.reshape(batch, seq, -1)
    g = g.reshape(batch, seq, -1)

    def spec(c):
        return pl.BlockSpec((1, seq, c), lambda b, h: (b, 0, h))

    o = pl.pallas_call(
        _gla_kernel,
        grid=(batch, GLA_HEADS),
        in_specs=[spec(GLA_DK), spec(GLA_DK), spec(GLA_DV), spec(GLA_DK), spec(GLA_DV),
                  _const_spec((1, GLA_DV))],
        out_specs=spec(GLA_DV),
        out_shape=jax.ShapeDtypeStruct((batch, seq, GLA_WIDTH), BF16),
        scratch_shapes=[pltpu.VMEM((GLA_DV, GLA_DK), F32)],
        compiler_params=_cparams(2),
        name="gla_mixer",
    )(q, k, v, la, g, g_norm.reshape(1, -1).astype(F32))
    return o.reshape(batch * seq, GLA_WIDTH)


def kernel(x, positions, ln_g, ln_b, even_w_in, mla_q_norm, mla_kv_norm, mla_w_uq, mla_w_ukv, s5_a_re, s5_a_im, s5_log_dt, s5_b_re, s5_b_im, s5_c_re, s5_c_im, s5_d, s5_w_glu, s5_b_glu, even_w_out, odd_w_in, gla_w_gk2, gla_b_gk, gla_g_norm, odd_w_out):
    batch, seq, _ = x.shape
    xf = x.reshape(batch * seq, D_MODEL)
    cos_full, sin_full = _rope_tables(positions)
    for layer in range(DEPTH):
        j = layer // 2
        if layer % 2 == 0:
            q, k, v, gm, u, gs = _even_in(xf, cos_full, sin_full, even_w_in[j], mla_q_norm[j],
                                          mla_kv_norm[j], mla_w_uq[j], mla_w_ukv[j])
            o_mla = _attention(q, k, v, gm, batch, seq)
            o_s5 = _s5(u, gs, batch, seq, s5_a_re[j], s5_a_im[j], s5_log_dt[j], s5_b_re[j], s5_b_im[j],
                       s5_c_re[j], s5_c_im[j], s5_d[j], s5_w_glu[j], s5_b_glu[j])
            w_out = even_w_out[j].astype(BF16)
            xf = _out_ln(xf, [o_mla, o_s5], [w_out[:MLA_WIDTH], w_out[MLA_WIDTH:]], ln_g[layer], ln_b[layer])
        else:
            q, k, v, la, g = _odd_in(xf, odd_w_in[j], gla_w_gk2[j], gla_b_gk[j])
            o = _gla(q, k, v, la, g, gla_g_norm[j], batch, seq)
            xf = _out_ln(xf, [o], [odd_w_out[j].astype(BF16)], ln_g[layer], ln_b[layer])
    return xf.reshape(batch, seq, D_MODEL)
```

```python
import functools
import math

import jax
import jax.numpy as jnp
from jax import lax
from jax.experimental import pallas as pl
from jax.experimental.pallas import tpu as pltpu

F32 = jnp.float32
BF16 = jnp.bfloat16

D_MODEL = 1024
DEPTH = 2

MLA_HEADS = 8
MLA_NOPE = 64
MLA_ROPE = 32
MLA_V = 64
MLA_Q_RANK = 256
MLA_KV_RANK = 128
MLA_WIDTH = MLA_HEADS * MLA_V
ROPE_BASE = 10000.0
HEAD_SLOT = 128

S5_WIDTH = 256
S5_GROUP = 16
S5_GROUPS = S5_WIDTH // S5_GROUP
S5_STATE = 64
S5_L = 8
S5_NB = 2

GLA_HEADS = 4
GLA_KEY_WIDTH = D_MODEL // 2
GLA_WIDTH = D_MODEL
GLA_DK = GLA_KEY_WIDTH // GLA_HEADS
GLA_DV = GLA_WIDTH // GLA_HEADS
GLA_GATE_RANK = 16
GLA_GATE_NORM = 16.0
GLA_CHUNK = 64
GLA_SUPER = 256

DEEPNORM_ALPHA = (2 * DEPTH) ** 0.25
LN_EPS = 1e-5
RMS_EPS = 1e-6

TOKEN_TILE = 512
ATTN_T = 256
ATTN_GROUP = 4
NEG_BIG = -1e30
VMEM_LIMIT = 56 * 1024 * 1024


def _cparams(n_axes):
    return pltpu.CompilerParams(dimension_semantics=("arbitrary",) * n_axes,
                                vmem_limit_bytes=VMEM_LIMIT)


def _const_spec(shape, single=False):
    nd = len(shape)
    if single:
        return pl.BlockSpec(shape, lambda *_: (0,) * nd, pipeline_mode=pl.Buffered(1))
    return pl.BlockSpec(shape, lambda *_: (0,) * nd)


def _sigmoid(x):
    return 1.0 / (1.0 + jnp.exp(-x))


def _silu(x):
    return x * _sigmoid(x)


def _rms(x, g):
    return (x * lax.rsqrt(jnp.mean(x * x, axis=-1, keepdims=True) + RMS_EPS)) * g


def _rope_kernel(pos_ref, invf_ref, cos_ref, sin_ref, nsin_ref):
    ang = pos_ref[...].astype(F32) * invf_ref[...]
    c = jnp.cos(ang)
    s = jnp.sin(ang)
    cos_ref[...] = c
    sin_ref[...] = s
    nsin_ref[...] = -s


def _rope_tables(positions):
    t = positions.size
    half = MLA_ROPE // 2
    rows = t * half // 128
    pos_rep = jnp.repeat(positions.reshape(t), half).reshape(rows, 128)
    inv_freq = ROPE_BASE ** (-jnp.arange(half, dtype=F32) / half)
    invf = jnp.tile(inv_freq, 128 // half).reshape(1, 128)
    rb = 1024
    out = jax.ShapeDtypeStruct((rows, 128), F32)
    cos, sin, nsin = pl.pallas_call(
        _rope_kernel,
        grid=(rows // rb,),
        in_specs=[pl.BlockSpec((rb, 128), lambda i: (i, 0)), _const_spec((1, 128))],
        out_specs=[pl.BlockSpec((rb, 128), lambda i: (i, 0))] * 3,
        out_shape=[out, out, out],
        compiler_params=_cparams(1),
        name="rope_tables",
    )(pos_rep, invf)
    cos, sin, nsin = (a.reshape(t, half) for a in (cos, sin, nsin))
    ones = jnp.ones((t, MLA_NOPE), F32)
    zeros = jnp.zeros((t, MLA_NOPE), F32)
    pad = HEAD_SLOT - MLA_NOPE - MLA_ROPE
    cos_full = jnp.concatenate([ones, cos, cos, ones[:, :pad]], axis=1)
    sin_full = jnp.concatenate([zeros, nsin, sin, zeros[:, :pad]], axis=1)
    return cos_full, sin_full


EV_CQ = 0
EV_CKV = EV_CQ + MLA_Q_RANK
EV_KPE = EV_CKV + MLA_KV_RANK
EV_GM = EV_KPE + HEAD_SLOT
EV_U = EV_GM + MLA_WIDTH
EV_GS = EV_U + S5_WIDTH
EV_COLS = EV_GS + S5_WIDTH
Q_SCALE = (MLA_NOPE + MLA_ROPE) ** -0.5 * math.log2(math.e)


def _rotate_pairs(x):
    n = x.shape[-1]
    lane = lax.broadcasted_iota(jnp.int32, x.shape, 1)
    lo = (lane & 112) == 64
    return jnp.where(lo, pltpu.roll(x, n - MLA_ROPE // 2, 1), pltpu.roll(x, MLA_ROPE // 2, 1))


def _even_in_kernel(x_ref, cos_ref, sin_ref, w_in_ref, qn_ref, kvn_ref, wuq_ref, wk_ref, wv_ref,
                    q_ref, k_ref, v_ref, gm_ref, u_ref, gs_ref):
    xb = x_ref[...].astype(BF16)
    h = jnp.dot(xb, w_in_ref[...], preferred_element_type=F32)
    cos = cos_ref[...]
    sin = sin_ref[...]

    cqn = _rms(h[:, EV_CQ:EV_CKV], qn_ref[...]).astype(BF16)
    q = jnp.dot(cqn, wuq_ref[...], preferred_element_type=F32)
    q = q * jnp.tile(cos, (1, MLA_HEADS)) + _rotate_pairs(q) * jnp.tile(sin, (1, MLA_HEADS))
    q_ref[...] = (q * Q_SCALE).astype(BF16)

    kpe = h[:, EV_KPE:EV_GM]
    kpe = kpe * cos + _rotate_pairs(kpe) * sin
    ckvn = _rms(h[:, EV_CKV:EV_KPE], kvn_ref[...]).astype(BF16)
    k = jnp.dot(ckvn, wk_ref[...], preferred_element_type=F32) + jnp.tile(kpe, (1, MLA_HEADS))
    k_ref[...] = k.astype(BF16)
    v_ref[...] = jnp.dot(ckvn, wv_ref[...], preferred_element_type=F32).astype(BF16)

    gm_ref[...] = _silu(h[:, EV_GM:EV_U]).astype(BF16)
    u_ref[...] = h[:, EV_U:EV_GS].astype(BF16)
    gs_ref[...] = _silu(h[:, EV_GS:EV_COLS]).astype(BF16)


def _even_in(xf, cos_full, sin_full, w_in, q_norm, kv_norm, w_uq, w_ukv):
    t = xf.shape[0]
    tm = TOKEN_TILE
    pad = HEAD_SLOT - MLA_NOPE - MLA_ROPE
    cq, ckv, kr, gm, u, gs = jnp.split(
        w_in, [MLA_Q_RANK, MLA_Q_RANK + MLA_KV_RANK, MLA_Q_RANK + MLA_KV_RANK + MLA_ROPE,
               MLA_Q_RANK + MLA_KV_RANK + MLA_ROPE + MLA_WIDTH,
               MLA_Q_RANK + MLA_KV_RANK + MLA_ROPE + MLA_WIDTH + S5_WIDTH], axis=1)
    kr = jnp.pad(kr, ((0, 0), (MLA_NOPE, pad)))
    w_in_p = jnp.concatenate([cq, ckv, kr, gm, u, gs], axis=1).astype(BF16)
    wuq_p = jnp.pad(w_uq.reshape(MLA_Q_RANK, MLA_HEADS, MLA_NOPE + MLA_ROPE),
                    ((0, 0), (0, 0), (0, pad))).reshape(MLA_Q_RANK, MLA_HEADS * HEAD_SLOT).astype(BF16)
    wkv = w_ukv.reshape(MLA_KV_RANK, MLA_HEADS, MLA_NOPE + MLA_V)
    wk_p = jnp.pad(wkv[:, :, :MLA_NOPE], ((0, 0), (0, 0), (0, HEAD_SLOT - MLA_NOPE))
                   ).reshape(MLA_KV_RANK, MLA_HEADS * HEAD_SLOT).astype(BF16)
    wv_p = wkv[:, :, MLA_NOPE:].reshape(MLA_KV_RANK, MLA_WIDTH).astype(BF16)

    def row(n):
        return pl.BlockSpec((tm, n), lambda i: (i, 0))

    qk_w = MLA_HEADS * HEAD_SLOT
    outs = [jax.ShapeDtypeStruct((t, qk_w), BF16), jax.ShapeDtypeStruct((t, qk_w), BF16),
            jax.ShapeDtypeStruct((t, MLA_WIDTH), BF16), jax.ShapeDtypeStruct((t, MLA_WIDTH), BF16),
            jax.ShapeDtypeStruct((t, S5_WIDTH), BF16), jax.ShapeDtypeStruct((t, S5_WIDTH), BF16)]
    return pl.pallas_call(
        _even_in_kernel,
        grid=(t // tm,),
        in_specs=[row(D_MODEL), row(HEAD_SLOT), row(HEAD_SLOT),
                  _const_spec((D_MODEL, EV_COLS)), _const_spec((1, MLA_Q_RANK)),
                  _const_spec((1, MLA_KV_RANK)), _const_spec((MLA_Q_RANK, qk_w)),
                  _const_spec((MLA_KV_RANK, qk_w)), _const_spec((MLA_KV_RANK, MLA_WIDTH))],
        out_specs=[row(qk_w), row(qk_w), row(MLA_WIDTH), row(MLA_WIDTH), row(S5_WIDTH), row(S5_WIDTH)],
        out_shape=outs,
        compiler_params=_cparams(1),
        name="even_in_proj",
    )(xf, cos_full, sin_full, w_in_p, q_norm.reshape(1, -1), kv_norm.reshape(1, -1), wuq_p, wk_p, wv_p)


def _attn_kernel(q_ref, k_ref, v_ref, g_ref, o_ref, s_scr):
    seq = q_ref.shape[1]
    t = ATTN_T
    n = seq // t
    nt = (((1,), (1,)), ((), ()))
    hh = pl.program_id(2)

    @pl.when(hh == 0)
    def _():
        o_ref[...] = jnp.zeros_like(o_ref)

    keep = lax.broadcasted_iota(jnp.int32, (t, t), 1) <= lax.broadcasted_iota(jnp.int32, (t, t), 0)
    for kj in range(n):
        r0 = kj * t
        k = k_ref[0, r0:r0 + t, :]
        sd = lax.dot_general(q_ref[0, r0:r0 + t, :], k, nt, preferred_element_type=F32)
        s_scr[r0:r0 + t, r0:r0 + t] = jnp.where(keep, sd, NEG_BIG)
        if r0 + t < seq:
            s_scr[r0 + t:, r0:r0 + t] = lax.dot_general(q_ref[0, r0 + t:, :], k, nt,
                                                         preferred_element_type=F32)

    mine = (lax.broadcasted_iota(jnp.int32, (t, ATTN_GROUP * MLA_V), 1) // MLA_V) == hh
    for qi in range(n):
        r0 = qi * t
        lk = r0 + t
        s = s_scr[r0:r0 + t, :lk]
        m = jnp.max(s, axis=1, keepdims=True)
        p = jnp.exp2(s - m)
        l = jnp.sum(p, axis=1, keepdims=True)
        acc = jnp.dot(p.astype(BF16), v_ref[0, :lk, :], preferred_element_type=F32)
        o = (acc * (1.0 / l)) * g_ref[0, r0:r0 + t, :].astype(F32)
        o_ref[0, r0:r0 + t, :] = jnp.where(mine, o.astype(BF16), o_ref[0, r0:r0 + t, :])


def _attention(q, k, v, g, batch, seq):
    q = q.reshape(batch, seq, -1)
    k = k.reshape(batch, seq, -1)
    v = v.reshape(batch, seq, -1)
    g = g.reshape(batch, seq, -1)
    gw = ATTN_GROUP * MLA_V
    head = pl.BlockSpec((1, seq, HEAD_SLOT), lambda b, gi, hh: (b, 0, gi * ATTN_GROUP + hh))
    group = pl.BlockSpec((1, seq, gw), lambda b, gi, hh: (b, 0, gi))
    o = pl.pallas_call(
        _attn_kernel,
        grid=(batch, MLA_HEADS // ATTN_GROUP, ATTN_GROUP),
        in_specs=[head, head, group, group],
        out_specs=group,
        out_shape=jax.ShapeDtypeStruct((batch, seq, MLA_WIDTH), BF16),
        scratch_shapes=[pltpu.VMEM((seq, seq), F32)],
        compiler_params=_cparams(3),
        name="mla_attention",
    )(q, k, v, g)
    return o.reshape(batch * seq, MLA_WIDTH)


def _s5_matrices(a_re, a_im, log_dt, b_re, b_im, c_re, c_im):
    L, G, P, H = S5_L, S5_GROUPS, S5_STATE, S5_GROUP
    dt = jnp.exp(log_dt.astype(F32))[:, None]
    ar, ai = a_re.astype(F32), a_im.astype(F32)
    mag = jnp.exp(ar * dt)
    lr, li = mag * jnp.cos(ai * dt), mag * jnp.sin(ai * dt)
    den = ar * ar + ai * ai
    nr, ni = lr - 1.0, li
    zr = (nr * ar + ni * ai) / den
    zi = (ni * ar - nr * ai) / den
    br, bi = b_re.astype(F32), b_im.astype(F32)
    bbr = zr[..., None] * br - zi[..., None] * bi
    bbi = zr[..., None] * bi + zi[..., None] * br
    cr, ci = c_re.astype(F32), c_im.astype(F32)

    def powers(j):
        j = j.astype(F32)[:, None, None]
        pmag = jnp.exp(j * (ar * dt))
        return pmag * jnp.cos(j * (ai * dt)), pmag * jnp.sin(j * (ai * dt))

    gp, w = G * P, G * H
    pr, pi = powers(jnp.arange(L + 1))
    prow = jnp.concatenate([pr.reshape(L + 1, gp), pi.reshape(L + 1, gp)], axis=0)
    prow3 = prow.reshape(2 * (L + 1), 1, gp)
    pcol3 = prow.reshape(2 * (L + 1), gp, 1)

    same_b = (jnp.arange(w)[:, None] // H) == (jnp.arange(gp)[None, :] // P)
    same_c = (jnp.arange(gp)[:, None] // P) == (jnp.arange(w)[None, :] // H)
    bd_br = jnp.where(same_b, jnp.tile(bbr.transpose(2, 0, 1).reshape(H, gp), (G, 1)), 0.0)
    bd_bi = jnp.where(same_b, jnp.tile(bbi.transpose(2, 0, 1).reshape(H, gp), (G, 1)), 0.0)
    bd_cr = jnp.where(same_c, jnp.tile(cr.transpose(0, 2, 1).reshape(gp, H), (1, G)), 0.0)
    bd_ci = jnp.where(same_c, jnp.tile(ci.transpose(0, 2, 1).reshape(gp, H), (1, G)), 0.0)

    lw = L * w
    bmat, tmat, cmat = pl.pallas_call(
        _s5_prepare_kernel,
        grid=(L,),
        in_specs=[_const_spec((w, gp)), _const_spec((w, gp)), _const_spec((gp, w)), _const_spec((gp, w)),
                  _const_spec((2 * (L + 1), gp)),
                  pl.BlockSpec((1, 1, gp), lambda i: (L - 1 - i, 0, 0)),
                  pl.BlockSpec((1, 1, gp), lambda i: (2 * L - i, 0, 0)),
                  pl.BlockSpec((1, gp, 1), lambda i: (i + 1, 0, 0)),
                  pl.BlockSpec((1, gp, 1), lambda i: (L + 2 + i, 0, 0))],
        out_specs=[pl.BlockSpec((w, 2 * gp), lambda i: (i, 0)),
                   pl.BlockSpec((w, lw), lambda i: (i, 0)),
                   pl.BlockSpec((2 * gp, w), lambda i: (0, i))],
        out_shape=[jax.ShapeDtypeStruct((lw, 2 * gp), BF16), jax.ShapeDtypeStruct((lw, lw), BF16),
                   jax.ShapeDtypeStruct((2 * gp, lw), BF16)],
        scratch_shapes=[pltpu.VMEM((L, w, w), F32)],
        compiler_params=_cparams(1),
        name="s5_prepare",
    )(bd_br, bd_bi, bd_cr, bd_ci, prow, prow3, prow3, pcol3, pcol3)
    lam = jnp.stack([pr[L].reshape(gp), pi[L].reshape(gp)])
    return bmat, tmat, cmat, lam


def _s5_prepare_kernel(br_ref, bi_ref, cr_ref, ci_ref, prow_ref, rr_ref, ri_ref, cr_pow_ref, ci_pow_ref,
                       bmat_ref, tmat_ref, cmat_ref, k_scr):
    L = S5_L
    w = S5_WIDTH
    gp = S5_GROUPS * S5_STATE
    i = pl.program_id(0)
    br, bi = br_ref[...], bi_ref[...]
    cr, ci = cr_ref[...], ci_ref[...]

    @pl.when(i == 0)
    def _():
        for j in range(L):
            pr = prow_ref[j:j + 1, :]
            pi = prow_ref[L + 1 + j:L + 2 + j, :]
            k_scr[j] = (jnp.dot(br * pr - bi * pi, cr, precision=lax.Precision.HIGHEST,
                                preferred_element_type=F32)
                        - jnp.dot(bi * pr + br * pi, ci, precision=lax.Precision.HIGHEST,
                                  preferred_element_type=F32))

    pr, pi = rr_ref[0], ri_ref[0]
    bmat_ref[:, :gp] = (br * pr - bi * pi).astype(BF16)
    bmat_ref[:, gp:] = (bi * pr + br * pi).astype(BF16)
    for t in range(L):
        lag = jnp.maximum(t - i, 0)
        tmat_ref[:, t * w:(t + 1) * w] = jnp.where(t >= i, k_scr[lag], 0.0).astype(BF16)
    pcr, pci = cr_pow_ref[0], ci_pow_ref[0]
    cmat_ref[:gp, :] = (cr * pcr - ci * pci).astype(BF16)
    cmat_ref[gp:, :] = (-(cr * pci + ci * pcr)).astype(BF16)


def _gelu_tanh(y):
    return 0.5 * y * (1.0 + jnp.tanh(math.sqrt(2.0 / math.pi) * (y + 0.044715 * (y * y * y))))


def _s5_kernel(u_ref, gs_ref, bmat_ref, tmat_ref, cmat_ref, lam_ref, dskip_ref, wglu_ref, bglu_ref,
               o_ref, x_scr, h_scr):
    w = S5_WIDTH
    gp = S5_GROUPS * S5_STATE
    rows = u_ref.shape[0]
    per_batch = rows // S5_NB
    x_scr[...] = jnp.dot(u_ref[...], bmat_ref[...], preferred_element_type=F32)
    lam_r = lam_ref[0:1, :]
    lam_i = lam_ref[1:2, :]

    def block(i, hs):
        out = []
        for b in range(S5_NB):
            hr, hi = hs[b]
            r0 = pl.multiple_of(b * per_batch + i * 8, 8)
            x8 = x_scr[pl.ds(r0, 8), :]
            starts_r, starts_i = [], []
            for jj in range(8):
                starts_r.append(hr)
                starts_i.append(hi)
                xr = x8[jj:jj + 1, :gp]
                xi = x8[jj:jj + 1, gp:]
                hr, hi = lam_r * hr - lam_i * hi + xr, lam_r * hi + lam_i * hr + xi
            h_scr[pl.ds(r0, 8), :gp] = jnp.concatenate(starts_r, axis=0)
            h_scr[pl.ds(r0, 8), gp:] = jnp.concatenate(starts_i, axis=0)
            out.append((hr, hi))
        return tuple(out)

    zero = jnp.zeros((1, gp), F32)
    lax.fori_loop(0, per_batch // 8, block, tuple((zero, zero) for _ in range(S5_NB)))

    hs = h_scr[...].astype(BF16)
    for t in range(S5_L):
        cols = slice(t * w, (t + 1) * w)
        y = jnp.dot(u_ref[:, :(t + 1) * w], tmat_ref[:(t + 1) * w, cols], preferred_element_type=F32)
        y = y + jnp.dot(hs, cmat_ref[:, cols], preferred_element_type=F32)
        y = y + dskip_ref[...] * u_ref[:, cols].astype(F32)
        y = _gelu_tanh(y)
        z = jnp.dot(y.astype(BF16), wglu_ref[...], preferred_element_type=F32) + bglu_ref[...]
        o = (y * _sigmoid(z)) * gs_ref[:, cols].astype(F32)
        o_ref[:, cols] = o.astype(BF16)


def _s5(u, gs, batch, seq, a_re, a_im, log_dt, b_re, b_im, c_re, c_im, d_skip, w_glu, b_glu):
    t = batch * seq
    lw = S5_L * S5_WIDTH
    gp2 = 2 * S5_GROUPS * S5_STATE
    bmat, tmat, cmat, lam = _s5_matrices(a_re, a_im, log_dt, b_re, b_im, c_re, c_im)
    rows = S5_NB * seq // S5_L
    u2 = u.reshape(t // S5_L, lw)
    gs2 = gs.reshape(t // S5_L, lw)
    blk = pl.BlockSpec((rows, lw), lambda i: (i, 0))
    o = pl.pallas_call(
        _s5_kernel,
        grid=(batch // S5_NB,),
        in_specs=[blk, blk, _const_spec((lw, gp2), True), _const_spec((lw, lw), True),
                  _const_spec((gp2, lw), True), _const_spec((2, gp2 // 2)),
                  _const_spec((1, S5_WIDTH)), _const_spec((S5_WIDTH, S5_WIDTH)), _const_spec((1, S5_WIDTH))],
        out_specs=blk,
        out_shape=jax.ShapeDtypeStruct((t // S5_L, lw), BF16),
        scratch_shapes=[pltpu.VMEM((rows, gp2), F32), pltpu.VMEM((rows, gp2), F32)],
        compiler_params=_cparams(1),
        name="s5_mixer",
    )(u2, gs2, bmat, tmat, cmat, lam, d_skip.reshape(1, -1).astype(F32), w_glu.astype(BF16),
      b_glu.reshape(1, -1).astype(F32))
    return o.reshape(t, S5_WIDTH)


def _out_ln_kernel(n_act, x_ref, *refs):
    acts = refs[:n_act]
    ws = refs[n_act:2 * n_act]
    g_ref, b_ref, o_ref = refs[2 * n_act:]
    y = jnp.dot(acts[0][...], ws[0][...], preferred_element_type=F32)
    for a, w in zip(acts[1:], ws[1:]):
        y = y + jnp.dot(a[...], w[...], preferred_element_type=F32)
    z = DEEPNORM_ALPHA * x_ref[...] + y
    mu = jnp.mean(z, axis=-1, keepdims=True)
    zc = z - mu
    var = jnp.mean(zc * zc, axis=-1, keepdims=True)
    o_ref[...] = zc * lax.rsqrt(var + LN_EPS) * g_ref[...] + b_ref[...]


def _out_ln(xf, acts, ws, ln_g, ln_b):
    t = xf.shape[0]
    tm = TOKEN_TILE
    n = len(acts)

    def row(c):
        return pl.BlockSpec((tm, c), lambda i: (i, 0))

    return pl.pallas_call(
        functools.partial(_out_ln_kernel, n),
        grid=(t // tm,),
        in_specs=[row(D_MODEL)] + [row(a.shape[1]) for a in acts] + [_const_spec(w.shape) for w in ws]
                 + [_const_spec((1, D_MODEL)), _const_spec((1, D_MODEL))],
        out_specs=row(D_MODEL),
        out_shape=jax.ShapeDtypeStruct((t, D_MODEL), F32),
        compiler_params=_cparams(1),
        name="out_proj_layernorm",
    )(xf, *acts, *ws, ln_g.reshape(1, -1), ln_b.reshape(1, -1))


OD_Q = 0
OD_K = OD_Q + GLA_KEY_WIDTH
OD_V = OD_K + GLA_KEY_WIDTH
OD_GL = OD_V + GLA_WIDTH
OD_G = OD_GL + 128
OD_COLS = OD_G + GLA_WIDTH


def _odd_in_kernel(x_ref, w_ref, wgk_ref, bgk_ref, q_ref, k_ref, v_ref, la_ref, g_ref):
    xb = x_ref[...].astype(BF16)

    def proj(lo, hi):
        return jnp.dot(xb, w_ref[:, lo:hi], preferred_element_type=F32)

    q_ref[...] = proj(OD_Q, OD_K).astype(BF16)
    k_ref[...] = proj(OD_K, OD_V).astype(BF16)
    v_ref[...] = proj(OD_V, OD_GL).astype(BF16)
    z = jnp.dot(proj(OD_GL, OD_G).astype(BF16), wgk_ref[...], preferred_element_type=F32) + bgk_ref[...]
    log_sig = -(jnp.maximum(-z, 0.0) + jnp.log1p(jnp.exp(-jnp.abs(z))))
    la_ref[...] = log_sig / GLA_GATE_NORM
    g_ref[...] = _silu(proj(OD_G, OD_COLS)).astype(BF16)


def _odd_in(xf, w_in, w_gk2, b_gk):
    t = xf.shape[0]
    tm = TOKEN_TILE
    q, k, v, gl, g = jnp.split(w_in, [OD_K, OD_V, OD_GL, OD_GL + GLA_GATE_RANK], axis=1)
    gl = jnp.pad(gl, ((0, 0), (0, 128 - GLA_GATE_RANK)))
    w_p = jnp.concatenate([q, k, v, gl, g], axis=1).astype(BF16)
    wgk_p = jnp.pad(w_gk2, ((0, 128 - GLA_GATE_RANK), (0, 0))).astype(BF16)

    def row(c):
        return pl.BlockSpec((tm, c), lambda i: (i, 0))

    outs = [jax.ShapeDtypeStruct((t, GLA_KEY_WIDTH), BF16), jax.ShapeDtypeStruct((t, GLA_KEY_WIDTH), BF16),
            jax.ShapeDtypeStruct((t, GLA_WIDTH), BF16), jax.ShapeDtypeStruct((t, GLA_KEY_WIDTH), F32),
            jax.ShapeDtypeStruct((t, GLA_WIDTH), BF16)]
    return pl.pallas_call(
        _odd_in_kernel,
        grid=(t // tm,),
        in_specs=[row(D_MODEL), _const_spec((D_MODEL, OD_COLS)), _const_spec((128, GLA_KEY_WIDTH)),
                  _const_spec((1, GLA_KEY_WIDTH))],
        out_specs=[row(GLA_KEY_WIDTH), row(GLA_KEY_WIDTH), row(GLA_WIDTH), row(GLA_KEY_WIDTH), row(GLA_WIDTH)],
        out_shape=outs,
        compiler_params=_cparams(1),
        name="odd_in_proj",
    )(xf, w_p, wgk_p, b_gk.reshape(1, -1).astype(F32))


def _gla_kernel(q_ref, k_ref, v_ref, la_ref, g_ref, gn_ref, o_ref, bc_scr):
    c, sb = GLA_CHUNK, GLA_SUPER
    cps = sb // c
    seq = q_ref.shape[1]
    nch = seq // c
    nt = (((1,), (1,)), ((), ()))
    tn = (((0,), (0,)), ((), ()))
    ri = lax.broadcasted_iota(jnp.int32, (sb, sb), 0)
    ci = lax.broadcasted_iota(jnp.int32, (sb, sb), 1)
    valid = jnp.logical_and(ci <= ri, ci >= (ri // c) * c)
    tri = jnp.where(valid, 1.0, 0.0).astype(BF16)
    scale = GLA_DK ** -0.5

    qds, k_ends, o_intras = [], [], []
    for s in range(seq // sb):
        rows = slice(s * sb, (s + 1) * sb)
        ga = la_ref[0, rows, :]
        ga_hi = ga.astype(BF16)
        ga_lo = (ga - ga_hi.astype(F32)).astype(BF16)
        cs = jnp.dot(tri, jnp.concatenate([ga_hi, ga_lo], axis=1), preferred_element_type=F32)
        bc = cs[:, :GLA_DK] + cs[:, GLA_DK:]
        bc_scr[rows, :] = bc
        bc3 = bc.reshape(cps, c, GLA_DK)
        bl3 = bc3[:, c - 1:c, :]
        q3 = q_ref[0, rows, :].astype(F32).reshape(cps, c, GLA_DK) * scale
        k3 = k_ref[0, rows, :].astype(F32).reshape(cps, c, GLA_DK)
        qd = (q3 * jnp.exp(bc3)).reshape(sb, GLA_DK).astype(BF16)
        k_inv = (k3 * jnp.exp(-bc3)).reshape(sb, GLA_DK).astype(BF16)
        k_end = (k3 * jnp.exp(bl3 - bc3)).reshape(sb, GLA_DK).astype(BF16)
        att = lax.dot_general(qd, k_inv, nt, preferred_element_type=F32)
        att = jnp.where(valid, att, 0.0).astype(BF16)
        o_intras.append(jnp.dot(att, v_ref[0, rows, :], preferred_element_type=F32))
        qds.append(qd)
        k_ends.append(k_end)

    b_last = bc_scr[pl.ds(c - 1, nch, stride=c), :]
    dec = jnp.concatenate([jnp.exp(b_last), jnp.zeros((GLA_DK - nch, GLA_DK), F32)], axis=0)
    dec_t = dec.T

    st = jnp.zeros((GLA_DK, GLA_DV), F32)
    for n in range(nch):
        s, j = divmod(n, cps)
        rows = slice(n * c, (n + 1) * c)
        sub = slice(j * c, (j + 1) * c)
        o = o_intras[s][sub] + jnp.dot(qds[s][sub], st.astype(BF16), preferred_element_type=F32)
        if n + 1 < nch:
            kv = lax.dot_general(k_ends[s][sub], v_ref[0, rows, :], tn, preferred_element_type=F32)
            st = st * dec_t[:, n:n + 1] + kv
        o = _rms(o, gn_ref[...]) * g_ref[0, rows, :].astype(F32)
        o_ref[0, rows, :] = o.astype(BF16)


def _gla(q, k, v, la, g, g_norm, batch, seq):
    q = q.reshape(batch, seq, -1)
    k = k.reshape(batch, seq, -1)
    v = v.reshape(batch, seq, -1)
    la = la.reshape(batch, seq, -1)
    g = g.reshape(batch, seq, -1)

    def spec(c):
        return pl.BlockSpec((1, seq, c), lambda b, h: (b, 0, h))

    o = pl.pallas_call(
        _gla_kernel,
        grid=(batch, GLA_HEADS),
        in_specs=[spec(GLA_DK), spec(GLA_DK), spec(GLA_DV), spec(GLA_DK), spec(GLA_DV),
                  _const_spec((1, GLA_DV))],
        out_specs=spec(GLA_DV),
        out_shape=jax.ShapeDtypeStruct((batch, seq, GLA_WIDTH), BF16),
        scratch_shapes=[pltpu.VMEM((seq, GLA_DK), F32)],
        compiler_params=_cparams(2),
        name="gla_mixer",
    )(q, k, v, la, g, g_norm.reshape(1, -1).astype(F32))
    return o.reshape(batch * seq, GLA_WIDTH)


def kernel(x, positions, ln_g, ln_b, even_w_in, mla_q_norm, mla_kv_norm, mla_w_uq, mla_w_ukv, s5_a_re, s5_a_im, s5_log_dt, s5_b_re, s5_b_im, s5_c_re, s5_c_im, s5_d, s5_w_glu, s5_b_glu, even_w_out, odd_w_in, gla_w_gk2, gla_b_gk, gla_g_norm, odd_w_out):
    batch, seq, _ = x.shape
    xf = x.reshape(batch * seq, D_MODEL)
    cos_full, sin_full = _rope_tables(positions)
    for layer in range(DEPTH):
        j = layer // 2
        if layer % 2 == 0:
            q, k, v, gm, u, gs = _even_in(xf, cos_full, sin_full, even_w_in[j], mla_q_norm[j],
                                          mla_kv_norm[j], mla_w_uq[j], mla_w_ukv[j])
            o_mla = _attention(q, k, v, gm, batch, seq)
            o_s5 = _s5(u, gs, batch, seq, s5_a_re[j], s5_a_im[j], s5_log_dt[j], s5_b_re[j], s5_b_im[j],
                       s5_c_re[j], s5_c_im[j], s5_d[j], s5_w_glu[j], s5_b_glu[j])
            w_out = even_w_out[j].astype(BF16)
            xf = _out_ln(xf, [o_mla, o_s5], [w_out[:MLA_WIDTH], w_out[MLA_WIDTH:]], ln_g[layer], ln_b[layer])
        else:
            q, k, v, la, g = _odd_in(xf, odd_w_in[j], gla_w_gk2[j], gla_b_gk[j])
            o = _gla(q, k, v, la, g, gla_g_norm[j], batch, seq)
            xf = _out_ln(xf, [o], [odd_w_out[j].astype(BF16)], ln_g[layer], ln_b[layer])
    return xf.reshape(batch, seq, D_MODEL)
```

```python
import functools
import math

import jax
import jax.numpy as jnp
from jax import lax
from jax.experimental import pallas as pl
from jax.experimental.pallas import tpu as pltpu

F32 = jnp.float32
BF16 = jnp.bfloat16

D_MODEL = 1024
DEPTH = 2

MLA_HEADS = 8
MLA_NOPE = 64
MLA_ROPE = 32
MLA_V = 64
MLA_Q_RANK = 256
MLA_KV_RANK = 128
MLA_WIDTH = MLA_HEADS * MLA_V
ROPE_BASE = 10000.0
HEAD_SLOT = 128

S5_WIDTH = 256
S5_GROUP = 16
S5_GROUPS = S5_WIDTH // S5_GROUP
S5_STATE = 64
S5_L = 8
S5_NB = 2

GLA_HEADS = 4
GLA_KEY_WIDTH = D_MODEL // 2
GLA_WIDTH = D_MODEL
GLA_DK = GLA_KEY_WIDTH // GLA_HEADS
GLA_DV = GLA_WIDTH // GLA_HEADS
GLA_GATE_RANK = 16
GLA_GATE_NORM = 16.0
GLA_CHUNK = 64
GLA_SUPER = 256

DEEPNORM_ALPHA = (2 * DEPTH) ** 0.25
LN_EPS = 1e-5
RMS_EPS = 1e-6

TOKEN_TILE = 512
ATTN_T = 256
ATTN_GROUP = 4
NEG_BIG = -1e30
VMEM_LIMIT = 56 * 1024 * 1024


def _cparams(n_axes):
    return pltpu.CompilerParams(dimension_semantics=("arbitrary",) * n_axes,
                                vmem_limit_bytes=VMEM_LIMIT)


def _const_spec(shape, single=False):
    nd = len(shape)
    if single:
        return pl.BlockSpec(shape, lambda *_: (0,) * nd, pipeline_mode=pl.Buffered(1))
    return pl.BlockSpec(shape, lambda *_: (0,) * nd)


def _sigmoid(x):
    return 1.0 / (1.0 + jnp.exp(-x))


def _silu(x):
    return x * _sigmoid(x)


def _rms(x, g):
    return (x * lax.rsqrt(jnp.mean(x * x, axis=-1, keepdims=True) + RMS_EPS)) * g


ROPE_PACK = 128 // (MLA_ROPE // 2)


def _rope_kernel(pos_ref, invf_ref, cos_ref, sin_ref):
    half = MLA_ROPE // 2
    rb = pos_ref.shape[0]
    pos = pos_ref[...].astype(F32)
    lane = lax.broadcasted_iota(jnp.int32, (rb, 128), 1)
    owner = lane // half
    posx = jnp.zeros((rb, 128), F32)
    for s in range(ROPE_PACK):
        posx = jnp.where(owner == s, pos[:, s:s + 1], posx)
    ang = posx * invf_ref[...]
    c = jnp.cos(ang)
    sn = jnp.sin(ang)
    first = (lane & 112) == MLA_NOPE
    second = (lane & 112) == MLA_NOPE + half
    for s in range(ROPE_PACK):
        k1 = (MLA_NOPE - half * s) % 128
        k2 = (MLA_NOPE + half - half * s) % 128
        cf = jnp.where(first, pltpu.roll(c, k1, 1), jnp.where(second, pltpu.roll(c, k2, 1), 1.0))
        sf = jnp.where(first, -pltpu.roll(sn, k1, 1), jnp.where(second, pltpu.roll(sn, k2, 1), 0.0))
        cos_ref[pl.ds(s, rb, stride=ROPE_PACK), :] = cf
        sin_ref[pl.ds(s, rb, stride=ROPE_PACK), :] = sf


def _rope_tables(positions):
    t = positions.size
    half = MLA_ROPE // 2
    pos = positions.reshape(t // ROPE_PACK, ROPE_PACK)
    inv_freq = ROPE_BASE ** (-jnp.arange(half, dtype=F32) / half)
    invf = jnp.tile(inv_freq, ROPE_PACK).reshape(1, 128)
    rb = 512
    out = jax.ShapeDtypeStruct((t, HEAD_SLOT), F32)
    return pl.pallas_call(
        _rope_kernel,
        grid=(t // ROPE_PACK // rb,),
        in_specs=[pl.BlockSpec((rb, ROPE_PACK), lambda i: (i, 0)), _const_spec((1, 128))],
        out_specs=[pl.BlockSpec((rb * ROPE_PACK, HEAD_SLOT), lambda i: (i, 0))] * 2,
        out_shape=[out, out],
        compiler_params=_cparams(1),
        name="rope_tables",
    )(pos, invf)


EV_CQ = 0
EV_CKV = EV_CQ + MLA_Q_RANK
EV_KPE = EV_CKV + MLA_KV_RANK
EV_GM = EV_KPE + HEAD_SLOT
EV_U = EV_GM + MLA_WIDTH
EV_GS = EV_U + S5_WIDTH
EV_COLS = EV_GS + S5_WIDTH
Q_SCALE = (MLA_NOPE + MLA_ROPE) ** -0.5 * math.log2(math.e)


def _rotate_pairs(x):
    n = x.shape[-1]
    lane = lax.broadcasted_iota(jnp.int32, x.shape, 1)
    lo = (lane & 112) == 64
    return jnp.where(lo, pltpu.roll(x, n - MLA_ROPE // 2, 1), pltpu.roll(x, MLA_ROPE // 2, 1))


def _even_in_kernel(x_ref, cos_ref, sin_ref, w_in_ref, qn_ref, kvn_ref, wuq_ref, wk_ref, wv_ref,
                    q_ref, k_ref, v_ref, gm_ref, u_ref, gs_ref):
    xb = x_ref[...].astype(BF16)
    h = jnp.dot(xb, w_in_ref[...], preferred_element_type=F32)
    cos = cos_ref[...]
    sin = sin_ref[...]

    cqn = _rms(h[:, EV_CQ:EV_CKV], qn_ref[...]).astype(BF16)
    q = jnp.dot(cqn, wuq_ref[...], preferred_element_type=F32)
    q = q * jnp.tile(cos, (1, MLA_HEADS)) + _rotate_pairs(q) * jnp.tile(sin, (1, MLA_HEADS))
    q_ref[...] = (q * Q_SCALE).astype(BF16)

    kpe = h[:, EV_KPE:EV_GM]
    kpe = kpe * cos + _rotate_pairs(kpe) * sin
    ckvn = _rms(h[:, EV_CKV:EV_KPE], kvn_ref[...]).astype(BF16)
    k = jnp.dot(ckvn, wk_ref[...], preferred_element_type=F32) + jnp.tile(kpe, (1, MLA_HEADS))
    k_ref[...] = k.astype(BF16)
    v_ref[...] = jnp.dot(ckvn, wv_ref[...], preferred_element_type=F32).astype(BF16)

    gm_ref[...] = _silu(h[:, EV_GM:EV_U]).astype(BF16)
    u_ref[...] = h[:, EV_U:EV_GS].astype(BF16)
    gs_ref[...] = _silu(h[:, EV_GS:EV_COLS]).astype(BF16)


def _even_in(xf, cos_full, sin_full, w_in, q_norm, kv_norm, w_uq, w_ukv):
    t = xf.shape[0]
    tm = TOKEN_TILE
    pad = HEAD_SLOT - MLA_NOPE - MLA_ROPE
    cq, ckv, kr, gm, u, gs = jnp.split(
        w_in, [MLA_Q_RANK, MLA_Q_RANK + MLA_KV_RANK, MLA_Q_RANK + MLA_KV_RANK + MLA_ROPE,
               MLA_Q_RANK + MLA_KV_RANK + MLA_ROPE + MLA_WIDTH,
               MLA_Q_RANK + MLA_KV_RANK + MLA_ROPE + MLA_WIDTH + S5_WIDTH], axis=1)
    kr = jnp.pad(kr, ((0, 0), (MLA_NOPE, pad)))
    w_in_p = jnp.concatenate([cq, ckv, kr, gm, u, gs], axis=1).astype(BF16)
    wuq_p = jnp.pad(w_uq.reshape(MLA_Q_RANK, MLA_HEADS, MLA_NOPE + MLA_ROPE),
                    ((0, 0), (0, 0), (0, pad))).reshape(MLA_Q_RANK, MLA_HEADS * HEAD_SLOT).astype(BF16)
    wkv = w_ukv.reshape(MLA_KV_RANK, MLA_HEADS, MLA_NOPE + MLA_V)
    wk_p = jnp.pad(wkv[:, :, :MLA_NOPE], ((0, 0), (0, 0), (0, HEAD_SLOT - MLA_NOPE))
                   ).reshape(MLA_KV_RANK, MLA_HEADS * HEAD_SLOT).astype(BF16)
    wv_p = wkv[:, :, MLA_NOPE:].reshape(MLA_KV_RANK, MLA_WIDTH).astype(BF16)

    def row(n):
        return pl.BlockSpec((tm, n), lambda i: (i, 0))

    qk_w = MLA_HEADS * HEAD_SLOT
    outs = [jax.ShapeDtypeStruct((t, qk_w), BF16), jax.ShapeDtypeStruct((t, qk_w), BF16),
            jax.ShapeDtypeStruct((t, MLA_WIDTH), BF16), jax.ShapeDtypeStruct((t, MLA_WIDTH), BF16),
            jax.ShapeDtypeStruct((t, S5_WIDTH), BF16), jax.ShapeDtypeStruct((t, S5_WIDTH), BF16)]
    return pl.pallas_call(
        _even_in_kernel,
        grid=(t // tm,),
        in_specs=[row(D_MODEL), row(HEAD_SLOT), row(HEAD_SLOT),
                  _const_spec((D_MODEL, EV_COLS)), _const_spec((1, MLA_Q_RANK)),
                  _const_spec((1, MLA_KV_RANK)), _const_spec((MLA_Q_RANK, qk_w)),
                  _const_spec((MLA_KV_RANK, qk_w)), _const_spec((MLA_KV_RANK, MLA_WIDTH))],
        out_specs=[row(qk_w), row(qk_w), row(MLA_WIDTH), row(MLA_WIDTH), row(S5_WIDTH), row(S5_WIDTH)],
        out_shape=outs,
        compiler_params=_cparams(1),
        name="even_in_proj",
    )(xf, cos_full, sin_full, w_in_p, q_norm.reshape(1, -1), kv_norm.reshape(1, -1), wuq_p, wk_p, wv_p)


def _attn_kernel(q_ref, k_ref, v_ref, g_ref, o_ref, s_scr):
    seq = q_ref.shape[1]
    t = ATTN_T
    n = seq // t
    nt = (((1,), (1,)), ((), ()))
    hh = pl.program_id(2)

    @pl.when(hh == 0)
    def _():
        o_ref[...] = jnp.zeros_like(o_ref)

    keep = lax.broadcasted_iota(jnp.int32, (t, t), 1) <= lax.broadcasted_iota(jnp.int32, (t, t), 0)
    for kj in range(n):
        r0 = kj * t
        k = k_ref[0, r0:r0 + t, :]
        sd = lax.dot_general(q_ref[0, r0:r0 + t, :], k, nt, preferred_element_type=F32)
        s_scr[r0:r0 + t, r0:r0 + t] = jnp.where(keep, sd, NEG_BIG)
        if r0 + t < seq:
            s_scr[r0 + t:, r0:r0 + t] = lax.dot_general(q_ref[0, r0 + t:, :], k, nt,
                                                         preferred_element_type=F32)

    mine = (lax.broadcasted_iota(jnp.int32, (t, ATTN_GROUP * MLA_V), 1) // MLA_V) == hh
    ps, ls = [], []
    for qi in range(n):
        r0 = qi * t
        s = s_scr[r0:r0 + t, :r0 + t]
        p = jnp.exp2(s - jnp.max(s, axis=1, keepdims=True))
        ls.append(jnp.sum(p, axis=1, keepdims=True))
        ps.append(p.astype(BF16))
    accs = [jnp.dot(p, v_ref[0, :(qi + 1) * t, :], preferred_element_type=F32) for qi, p in enumerate(ps)]
    for qi in range(n):
        rows = slice(qi * t, (qi + 1) * t)
        o = (accs[qi] * (1.0 / ls[qi])) * g_ref[0, rows, :].astype(F32)
        o_ref[0, rows, :] = jnp.where(mine, o.astype(BF16), o_ref[0, rows, :])


def _attention(q, k, v, g, batch, seq):
    q = q.reshape(batch, seq, -1)
    k = k.reshape(batch, seq, -1)
    v = v.reshape(batch, seq, -1)
    g = g.reshape(batch, seq, -1)
    gw = ATTN_GROUP * MLA_V
    head = pl.BlockSpec((1, seq, HEAD_SLOT), lambda b, gi, hh: (b, 0, gi * ATTN_GROUP + hh))
    group = pl.BlockSpec((1, seq, gw), lambda b, gi, hh: (b, 0, gi))
    o = pl.pallas_call(
        _attn_kernel,
        grid=(batch, MLA_HEADS // ATTN_GROUP, ATTN_GROUP),
        in_specs=[head, head, group, group],
        out_specs=group,
        out_shape=jax.ShapeDtypeStruct((batch, seq, MLA_WIDTH), BF16),
        scratch_shapes=[pltpu.VMEM((seq, seq), F32)],
        compiler_params=_cparams(3),
        name="mla_attention",
    )(q, k, v, g)
    return o.reshape(batch * seq, MLA_WIDTH)


def _s5_matrices(a_re, a_im, log_dt, b_re, b_im, c_re, c_im):
    L, G, P, H = S5_L, S5_GROUPS, S5_STATE, S5_GROUP
    dt = jnp.exp(log_dt.astype(F32))[:, None]
    ar, ai = a_re.astype(F32), a_im.astype(F32)
    mag = jnp.exp(ar * dt)
    lr, li = mag * jnp.cos(ai * dt), mag * jnp.sin(ai * dt)
    den = ar * ar + ai * ai
    nr, ni = lr - 1.0, li
    zr = (nr * ar + ni * ai) / den
    zi = (ni * ar - nr * ai) / den
    br, bi = b_re.astype(F32), b_im.astype(F32)
    bbr = zr[..., None] * br - zi[..., None] * bi
    bbi = zr[..., None] * bi + zi[..., None] * br
    cr, ci = c_re.astype(F32), c_im.astype(F32)

    def powers(j):
        j = j.astype(F32)[:, None, None]
        pmag = jnp.exp(j * (ar * dt))
        return pmag * jnp.cos(j * (ai * dt)), pmag * jnp.sin(j * (ai * dt))

    gp, w = G * P, G * H
    pr, pi = powers(jnp.arange(L + 1))
    prow = jnp.concatenate([pr.reshape(L + 1, gp), pi.reshape(L + 1, gp)], axis=0)
    prow3 = prow.reshape(2 * (L + 1), 1, gp)
    pcol3 = prow.reshape(2 * (L + 1), gp, 1)

    same_b = (jnp.arange(w)[:, None] // H) == (jnp.arange(gp)[None, :] // P)
    same_c = (jnp.arange(gp)[:, None] // P) == (jnp.arange(w)[None, :] // H)
    bd_br = jnp.where(same_b, jnp.tile(bbr.transpose(2, 0, 1).reshape(H, gp), (G, 1)), 0.0)
    bd_bi = jnp.where(same_b, jnp.tile(bbi.transpose(2, 0, 1).reshape(H, gp), (G, 1)), 0.0)
    bd_cr = jnp.where(same_c, jnp.tile(cr.transpose(0, 2, 1).reshape(gp, H), (1, G)), 0.0)
    bd_ci = jnp.where(same_c, jnp.tile(ci.transpose(0, 2, 1).reshape(gp, H), (1, G)), 0.0)

    lw = L * w
    bmat, tmat, cmat = pl.pallas_call(
        _s5_prepare_kernel,
        grid=(L,),
        in_specs=[_const_spec((w, gp)), _const_spec((w, gp)), _const_spec((gp, w)), _const_spec((gp, w)),
                  _const_spec((2 * (L + 1), gp)),
                  pl.BlockSpec((1, 1, gp), lambda i: (L - 1 - i, 0, 0)),
                  pl.BlockSpec((1, 1, gp), lambda i: (2 * L - i, 0, 0)),
                  pl.BlockSpec((1, gp, 1), lambda i: (i + 1, 0, 0)),
                  pl.BlockSpec((1, gp, 1), lambda i: (L + 2 + i, 0, 0))],
        out_specs=[pl.BlockSpec((w, 2 * gp), lambda i: (i, 0)),
                   pl.BlockSpec((w, lw), lambda i: (i, 0)),
                   pl.BlockSpec((2 * gp, w), lambda i: (0, i))],
        out_shape=[jax.ShapeDtypeStruct((lw, 2 * gp), BF16), jax.ShapeDtypeStruct((lw, lw), BF16),
                   jax.ShapeDtypeStruct((2 * gp, lw), BF16)],
        scratch_shapes=[pltpu.VMEM((L, w, w), F32)],
        compiler_params=_cparams(1),
        name="s5_prepare",
    )(bd_br, bd_bi, bd_cr, bd_ci, prow, prow3, prow3, pcol3, pcol3)
    lam = jnp.stack([pr[L].reshape(gp), pi[L].reshape(gp)])
    return bmat, tmat, cmat, lam


def _s5_prepare_kernel(br_ref, bi_ref, cr_ref, ci_ref, prow_ref, rr_ref, ri_ref, cr_pow_ref, ci_pow_ref,
                       bmat_ref, tmat_ref, cmat_ref, k_scr):
    L = S5_L
    w = S5_WIDTH
    gp = S5_GROUPS * S5_STATE
    i = pl.program_id(0)
    br, bi = br_ref[...], bi_ref[...]
    cr, ci = cr_ref[...], ci_ref[...]

    @pl.when(i == 0)
    def _():
        for j in range(L):
            pr = prow_ref[j:j + 1, :]
            pi = prow_ref[L + 1 + j:L + 2 + j, :]
            k_scr[j] = (jnp.dot(br * pr - bi * pi, cr, precision=lax.Precision.HIGHEST,
                                preferred_element_type=F32)
                        - jnp.dot(bi * pr + br * pi, ci, precision=lax.Precision.HIGHEST,
                                  preferred_element_type=F32))

    pr, pi = rr_ref[0], ri_ref[0]
    bmat_ref[:, :gp] = (br * pr - bi * pi).astype(BF16)
    bmat_ref[:, gp:] = (bi * pr + br * pi).astype(BF16)
    for t in range(L):
        lag = jnp.maximum(t - i, 0)
        tmat_ref[:, t * w:(t + 1) * w] = jnp.where(t >= i, k_scr[lag], 0.0).astype(BF16)
    pcr, pci = cr_pow_ref[0], ci_pow_ref[0]
    cmat_ref[:gp, :] = (cr * pcr - ci * pci).astype(BF16)
    cmat_ref[gp:, :] = (-(cr * pci + ci * pcr)).astype(BF16)


def _gelu_tanh(y):
    return 0.5 * y * (1.0 + jnp.tanh(math.sqrt(2.0 / math.pi) * (y + 0.044715 * (y * y * y))))


def _s5_kernel(u_ref, gs_ref, bmat_ref, tmat_ref, cmat_ref, lam_ref, dskip_ref, wglu_ref, bglu_ref,
               o_ref, x_scr, h_scr):
    w = S5_WIDTH
    gp = S5_GROUPS * S5_STATE
    rows = u_ref.shape[0]
    per_batch = rows // S5_NB
    x_scr[...] = jnp.dot(u_ref[...], bmat_ref[...], preferred_element_type=F32)
    lam_r = lam_ref[0:1, :]
    lam_i = lam_ref[1:2, :]

    def block(i, hs):
        out = []
        for b in range(S5_NB):
            hr, hi = hs[b]
            r0 = pl.multiple_of(b * per_batch + i * 8, 8)
            x8 = x_scr[pl.ds(r0, 8), :]
            starts_r, starts_i = [], []
            for jj in range(8):
                starts_r.append(hr)
                starts_i.append(hi)
                xr = x8[jj:jj + 1, :gp]
                xi = x8[jj:jj + 1, gp:]
                hr, hi = lam_r * hr - lam_i * hi + xr, lam_r * hi + lam_i * hr + xi
            h_scr[pl.ds(r0, 8), :gp] = jnp.concatenate(starts_r, axis=0)
            h_scr[pl.ds(r0, 8), gp:] = jnp.concatenate(starts_i, axis=0)
            out.append((hr, hi))
        return tuple(out)

    zero = jnp.zeros((1, gp), F32)
    lax.fori_loop(0, per_batch // 8, block, tuple((zero, zero) for _ in range(S5_NB)))

    hs = h_scr[...].astype(BF16)
    cols = [slice(t * w, (t + 1) * w) for t in range(S5_L)]
    ys = [jnp.dot(u_ref[:, :(t + 1) * w], tmat_ref[:(t + 1) * w, cols[t]], preferred_element_type=F32)
          + jnp.dot(hs, cmat_ref[:, cols[t]], preferred_element_type=F32) for t in range(S5_L)]
    ys = [_gelu_tanh(y + dskip_ref[...] * u_ref[:, cl].astype(F32)) for y, cl in zip(ys, cols)]
    zs = [jnp.dot(y.astype(BF16), wglu_ref[...], preferred_element_type=F32) for y in ys]
    for y, z, cl in zip(ys, zs, cols):
        o = (y * _sigmoid(z + bglu_ref[...])) * gs_ref[:, cl].astype(F32)
        o_ref[:, cl] = o.astype(BF16)


def _s5(u, gs, batch, seq, a_re, a_im, log_dt, b_re, b_im, c_re, c_im, d_skip, w_glu, b_glu):
    t = batch * seq
    lw = S5_L * S5_WIDTH
    gp2 = 2 * S5_GROUPS * S5_STATE
    bmat, tmat, cmat, lam = _s5_matrices(a_re, a_im, log_dt, b_re, b_im, c_re, c_im)
    rows = S5_NB * seq // S5_L
    u2 = u.reshape(t // S5_L, lw)
    gs2 = gs.reshape(t // S5_L, lw)
    blk = pl.BlockSpec((rows, lw), lambda i: (i, 0))
    o = pl.pallas_call(
        _s5_kernel,
        grid=(batch // S5_NB,),
        in_specs=[blk, blk, _const_spec((lw, gp2), True), _const_spec((lw, lw), True),
                  _const_spec((gp2, lw), True), _const_spec((2, gp2 // 2)),
                  _const_spec((1, S5_WIDTH)), _const_spec((S5_WIDTH, S5_WIDTH)), _const_spec((1, S5_WIDTH))],
        out_specs=blk,
        out_shape=jax.ShapeDtypeStruct((t // S5_L, lw), BF16),
        scratch_shapes=[pltpu.VMEM((rows, gp2), F32), pltpu.VMEM((rows, gp2), F32)],
        compiler_params=_cparams(1),
        name="s5_mixer",
    )(u2, gs2, bmat, tmat, cmat, lam, d_skip.reshape(1, -1).astype(F32), w_glu.astype(BF16),
      b_glu.reshape(1, -1).astype(F32))
    return o.reshape(t, S5_WIDTH)


def _out_ln_kernel(n_act, x_ref, *refs):
    acts = refs[:n_act]
    ws = refs[n_act:2 * n_act]
    g_ref, b_ref, o_ref = refs[2 * n_act:]
    y = jnp.dot(acts[0][...], ws[0][...], preferred_element_type=F32)
    for a, w in zip(acts[1:], ws[1:]):
        y = y + jnp.dot(a[...], w[...], preferred_element_type=F32)
    z = DEEPNORM_ALPHA * x_ref[...] + y
    mu = jnp.mean(z, axis=-1, keepdims=True)
    zc = z - mu
    var = jnp.mean(zc * zc, axis=-1, keepdims=True)
    o_ref[...] = zc * lax.rsqrt(var + LN_EPS) * g_ref[...] + b_ref[...]


def _out_ln(xf, acts, ws, ln_g, ln_b):
    t = xf.shape[0]
    tm = TOKEN_TILE
    n = len(acts)

    def row(c):
        return pl.BlockSpec((tm, c), lambda i: (i, 0))

    return pl.pallas_call(
        functools.partial(_out_ln_kernel, n),
        grid=(t // tm,),
        in_specs=[row(D_MODEL)] + [row(a.shape[1]) for a in acts] + [_const_spec(w.shape) for w in ws]
                 + [_const_spec((1, D_MODEL)), _const_spec((1, D_MODEL))],
        out_specs=row(D_MODEL),
        out_shape=jax.ShapeDtypeStruct((t, D_MODEL), F32),
        compiler_params=_cparams(1),
        name="out_proj_layernorm",
    )(xf, *acts, *ws, ln_g.reshape(1, -1), ln_b.reshape(1, -1))


OD_Q = 0
OD_K = OD_Q + GLA_KEY_WIDTH
OD_V = OD_K + GLA_KEY_WIDTH
OD_GL = OD_V + GLA_WIDTH
OD_G = OD_GL + 128
OD_COLS = OD_G + GLA_WIDTH


def _odd_in_kernel(x_ref, w_ref, wgk_ref, bgk_ref, q_ref, k_ref, v_ref, la_ref, g_ref):
    xb = x_ref[...].astype(BF16)

    def proj(lo, hi):
        return jnp.dot(xb, w_ref[:, lo:hi], preferred_element_type=F32)

    q_ref[...] = proj(OD_Q, OD_K).astype(BF16)
    k_ref[...] = proj(OD_K, OD_V).astype(BF16)
    v_ref[...] = proj(OD_V, OD_GL).astype(BF16)
    z = jnp.dot(proj(OD_GL, OD_G).astype(BF16), wgk_ref[...], preferred_element_type=F32) + bgk_ref[...]
    log_sig = -(jnp.maximum(-z, 0.0) + jnp.log1p(jnp.exp(-jnp.abs(z))))
    la_ref[...] = log_sig / GLA_GATE_NORM
    g_ref[...] = _silu(proj(OD_G, OD_COLS)).astype(BF16)


def _odd_in(xf, w_in, w_gk2, b_gk):
    t = xf.shape[0]
    tm = TOKEN_TILE
    q, k, v, gl, g = jnp.split(w_in, [OD_K, OD_V, OD_GL, OD_GL + GLA_GATE_RANK], axis=1)
    gl = jnp.pad(gl, ((0, 0), (0, 128 - GLA_GATE_RANK)))
    w_p = jnp.concatenate([q, k, v, gl, g], axis=1).astype(BF16)
    wgk_p = jnp.pad(w_gk2, ((0, 128 - GLA_GATE_RANK), (0, 0))).astype(BF16)

    def row(c):
        return pl.BlockSpec((tm, c), lambda i: (i, 0))

    outs = [jax.ShapeDtypeStruct((t, GLA_KEY_WIDTH), BF16), jax.ShapeDtypeStruct((t, GLA_KEY_WIDTH), BF16),
            jax.ShapeDtypeStruct((t, GLA_WIDTH), BF16), jax.ShapeDtypeStruct((t, GLA_KEY_WIDTH), F32),
            jax.ShapeDtypeStruct((t, GLA_WIDTH), BF16)]
    return pl.pallas_call(
        _odd_in_kernel,
        grid=(t // tm,),
        in_specs=[row(D_MODEL), _const_spec((D_MODEL, OD_COLS)), _const_spec((128, GLA_KEY_WIDTH)),
                  _const_spec((1, GLA_KEY_WIDTH))],
        out_specs=[row(GLA_KEY_WIDTH), row(GLA_KEY_WIDTH), row(GLA_WIDTH), row(GLA_KEY_WIDTH), row(GLA_WIDTH)],
        out_shape=outs,
        compiler_params=_cparams(1),
        name="odd_in_proj",
    )(xf, w_p, wgk_p, b_gk.reshape(1, -1).astype(F32))


def _gla_kernel(q_ref, k_ref, v_ref, la_ref, g_ref, gn_ref, o_ref, bc_scr):
    c, sb = GLA_CHUNK, GLA_SUPER
    cps = sb // c
    seq = q_ref.shape[1]
    nch = seq // c
    nt = (((1,), (1,)), ((), ()))
    tn = (((0,), (0,)), ((), ()))
    ri = lax.broadcasted_iota(jnp.int32, (sb, sb), 0)
    ci = lax.broadcasted_iota(jnp.int32, (sb, sb), 1)
    valid = jnp.logical_and(ci <= ri, ci >= (ri // c) * c)
    tri = jnp.where(valid, 1.0, 0.0).astype(BF16)
    scale = GLA_DK ** -0.5

    blocks = [slice(s * sb, (s + 1) * sb) for s in range(seq // sb)]
    chunks = [slice(n * c, (n + 1) * c) for n in range(nch)]

    bcs = []
    for rows in blocks:
        ga = la_ref[0, rows, :]
        ga_hi = ga.astype(BF16)
        ga_lo = (ga - ga_hi.astype(F32)).astype(BF16)
        cs = jnp.dot(tri, jnp.concatenate([ga_hi, ga_lo], axis=1), preferred_element_type=F32)
        bc = cs[:, :GLA_DK] + cs[:, GLA_DK:]
        bc_scr[rows, :] = bc
        bcs.append(bc.reshape(cps, c, GLA_DK))

    qds, k_invs, k_ends = [], [], []
    for rows, bc3 in zip(blocks, bcs):
        bl3 = bc3[:, c - 1:c, :]
        q3 = q_ref[0, rows, :].astype(F32).reshape(cps, c, GLA_DK) * scale
        k3 = k_ref[0, rows, :].astype(F32).reshape(cps, c, GLA_DK)
        qds.append((q3 * jnp.exp(bc3)).reshape(sb, GLA_DK).astype(BF16))
        k_invs.append((k3 * jnp.exp(-bc3)).reshape(sb, GLA_DK).astype(BF16))
        k_ends.append((k3 * jnp.exp(bl3 - bc3)).reshape(sb, GLA_DK).astype(BF16))

    atts = [lax.dot_general(qd, k_inv, nt, preferred_element_type=F32) for qd, k_inv in zip(qds, k_invs)]
    atts = [jnp.where(valid, att, 0.0).astype(BF16) for att in atts]
    o_intras = [jnp.dot(att, v_ref[0, rows, :], preferred_element_type=F32)
                for att, rows in zip(atts, blocks)]

    def chunk_of(vals, n):
        s, j = divmod(n, cps)
        return vals[s][j * c:(j + 1) * c]

    kvs = [lax.dot_general(chunk_of(k_ends, n), v_ref[0, chunks[n], :], tn, preferred_element_type=F32)
           for n in range(nch - 1)]

    b_last = bc_scr[pl.ds(c - 1, nch, stride=c), :]
    dec = jnp.concatenate([jnp.exp(b_last), jnp.zeros((GLA_DK - nch, GLA_DK), F32)], axis=0)
    dec_t = dec.T

    st = jnp.zeros((GLA_DK, GLA_DV), F32)
    states = [st.astype(BF16)]
    for n in range(nch - 1):
        st = st * dec_t[:, n:n + 1] + kvs[n]
        states.append(st.astype(BF16))

    o_inters = [jnp.dot(chunk_of(qds, n), states[n], preferred_element_type=F32) for n in range(nch)]
    for n in range(nch):
        o = chunk_of(o_intras, n) + o_inters[n]
        o = _rms(o, gn_ref[...]) * g_ref[0, chunks[n], :].astype(F32)
        o_ref[0, chunks[n], :] = o.astype(BF16)


def _gla(q, k, v, la, g, g_norm, batch, seq):
    q = q.reshape(batch, seq, -1)
    k = k.reshape(batch, seq, -1)
    v = v.reshape(batch, seq, -1)
    la = la.reshape(batch, seq, -1)
    g = g.reshape(batch, seq, -1)

    def spec(c):
        return pl.BlockSpec((1, seq, c), lambda b, h: (b, 0, h))

    o = pl.pallas_call(
        _gla_kernel,
        grid=(batch, GLA_HEADS),
        in_specs=[spec(GLA_DK), spec(GLA_DK), spec(GLA_DV), spec(GLA_DK), spec(GLA_DV),
                  _const_spec((1, GLA_DV))],
        out_specs=spec(GLA_DV),
        out_shape=jax.ShapeDtypeStruct((batch, seq, GLA_WIDTH), BF16),
        scratch_shapes=[pltpu.VMEM((seq, GLA_DK), F32)],
        compiler_params=_cparams(2),
        name="gla_mixer",
    )(q, k, v, la, g, g_norm.reshape(1, -1).astype(F32))
    return o.reshape(batch * seq, GLA_WIDTH)


def kernel(x, positions, ln_g, ln_b, even_w_in, mla_q_norm, mla_kv_norm, mla_w_uq, mla_w_ukv, s5_a_re, s5_a_im, s5_log_dt, s5_b_re, s5_b_im, s5_c_re, s5_c_im, s5_d, s5_w_glu, s5_b_glu, even_w_out, odd_w_in, gla_w_gk2, gla_b_gk, gla_g_norm, odd_w_out):
    batch, seq, _ = x.shape
    xf = x.reshape(batch * seq, D_MODEL)
    cos_full, sin_full = _rope_tables(positions)
    for layer in range(DEPTH):
        j = layer // 2
        if layer % 2 == 0:
            q, k, v, gm, u, gs = _even_in(xf, cos_full, sin_full, even_w_in[j], mla_q_norm[j],
                                          mla_kv_norm[j], mla_w_uq[j], mla_w_ukv[j])
            o_mla = _attention(q, k, v, gm, batch, seq)
            o_s5 = _s5(u, gs, batch, seq, s5_a_re[j], s5_a_im[j], s5_log_dt[j], s5_b_re[j], s5_b_im[j],
                       s5_c_re[j], s5_c_im[j], s5_d[j], s5_w_glu[j], s5_b_glu[j])
            w_out = even_w_out[j].astype(BF16)
            xf = _out_ln(xf, [o_mla, o_s5], [w_out[:MLA_WIDTH], w_out[MLA_WIDTH:]], ln_g[layer], ln_b[layer])
        else:
            q, k, v, la, g = _odd_in(xf, odd_w_in[j], gla_w_gk2[j], gla_b_gk[j])
            o = _gla(q, k, v, la, g, gla_g_norm[j], batch, seq)
            xf = _out_ln(xf, [o], [odd_w_out[j].astype(BF16)], ln_g[layer], ln_b[layer])
    return xf.reshape(batch, seq, D_MODEL)
```

```python
import functools
import math

import jax
import jax.numpy as jnp
from jax import lax
from jax.experimental import pallas as pl
from jax.experimental.pallas import tpu as pltpu

F32 = jnp.float32
BF16 = jnp.bfloat16

D_MODEL = 1024
DEPTH = 2

MLA_HEADS = 8
MLA_NOPE = 64
MLA_ROPE = 32
MLA_V = 64
MLA_Q_RANK = 256
MLA_KV_RANK = 128
MLA_WIDTH = MLA_HEADS * MLA_V
ROPE_BASE = 10000.0
HEAD_SLOT = 128

S5_WIDTH = 256
S5_GROUP = 16
S5_GROUPS = S5_WIDTH // S5_GROUP
S5_STATE = 64
S5_L = 8
S5_NB = 2

GLA_HEADS = 4
GLA_KEY_WIDTH = D_MODEL // 2
GLA_WIDTH = D_MODEL
GLA_DK = GLA_KEY_WIDTH // GLA_HEADS
GLA_DV = GLA_WIDTH // GLA_HEADS
GLA_GATE_RANK = 16
GLA_GATE_NORM = 16.0
GLA_CHUNK = 64
GLA_SUPER = 256

DEEPNORM_ALPHA = (2 * DEPTH) ** 0.25
LN_EPS = 1e-5
RMS_EPS = 1e-6

TOKEN_TILE = 1024
ATTN_T = 256
ATTN_GROUP = 4
NEG_BIG = -1e30
VMEM_LIMIT = 56 * 1024 * 1024


def _cparams(n_axes):
    return pltpu.CompilerParams(dimension_semantics=("arbitrary",) * n_axes,
                                vmem_limit_bytes=VMEM_LIMIT)


def _const_spec(shape, single=False):
    nd = len(shape)
    if single:
        return pl.BlockSpec(shape, lambda *_: (0,) * nd, pipeline_mode=pl.Buffered(1))
    return pl.BlockSpec(shape, lambda *_: (0,) * nd)


def _sigmoid(x):
    return 1.0 / (1.0 + jnp.exp(-x))


def _silu(x):
    return x * _sigmoid(x)


def _rms(x, g):
    return (x * lax.rsqrt(jnp.mean(x * x, axis=-1, keepdims=True) + RMS_EPS)) * g


ROPE_PACK = 128 // (MLA_ROPE // 2)


def _rope_kernel(pos_ref, invf_ref, cos_ref, sin_ref):
    half = MLA_ROPE // 2
    rb = pos_ref.shape[0]
    pos = pos_ref[...].astype(F32)
    lane = lax.broadcasted_iota(jnp.int32, (rb, 128), 1)
    owner = lane // half
    posx = jnp.zeros((rb, 128), F32)
    for s in range(ROPE_PACK):
        posx = jnp.where(owner == s, pos[:, s:s + 1], posx)
    ang = posx * invf_ref[...]
    c = jnp.cos(ang)
    sn = jnp.sin(ang)
    first = (lane & 112) == MLA_NOPE
    second = (lane & 112) == MLA_NOPE + half
    for s in range(ROPE_PACK):
        k1 = (MLA_NOPE - half * s) % 128
        k2 = (MLA_NOPE + half - half * s) % 128
        cf = jnp.where(first, pltpu.roll(c, k1, 1), jnp.where(second, pltpu.roll(c, k2, 1), 1.0))
        sf = jnp.where(first, -pltpu.roll(sn, k1, 1), jnp.where(second, pltpu.roll(sn, k2, 1), 0.0))
        cos_ref[pl.ds(s, rb, stride=ROPE_PACK), :] = cf
        sin_ref[pl.ds(s, rb, stride=ROPE_PACK), :] = sf


def _rope_tables(positions):
    t = positions.size
    half = MLA_ROPE // 2
    pos = positions.reshape(t // ROPE_PACK, ROPE_PACK)
    inv_freq = ROPE_BASE ** (-jnp.arange(half, dtype=F32) / half)
    invf = jnp.tile(inv_freq, ROPE_PACK).reshape(1, 128)
    rb = 512
    out = jax.ShapeDtypeStruct((t, HEAD_SLOT), F32)
    return pl.pallas_call(
        _rope_kernel,
        grid=(t // ROPE_PACK // rb,),
        in_specs=[pl.BlockSpec((rb, ROPE_PACK), lambda i: (i, 0)), _const_spec((1, 128))],
        out_specs=[pl.BlockSpec((rb * ROPE_PACK, HEAD_SLOT), lambda i: (i, 0))] * 2,
        out_shape=[out, out],
        compiler_params=_cparams(1),
        name="rope_tables",
    )(pos, invf)


EV_CQ = 0
EV_CKV = EV_CQ + MLA_Q_RANK
EV_KPE = EV_CKV + MLA_KV_RANK
EV_GM = EV_KPE + HEAD_SLOT
EV_U = EV_GM + MLA_WIDTH
EV_GS = EV_U + S5_WIDTH
EV_COLS = EV_GS + S5_WIDTH
Q_SCALE = (MLA_NOPE + MLA_ROPE) ** -0.5 * math.log2(math.e)


def _rotate_pairs(x):
    n = x.shape[-1]
    lane = lax.broadcasted_iota(jnp.int32, x.shape, 1)
    lo = (lane & 112) == 64
    return jnp.where(lo, pltpu.roll(x, n - MLA_ROPE // 2, 1), pltpu.roll(x, MLA_ROPE // 2, 1))


def _even_in_kernel(x_ref, cos_ref, sin_ref, w_in_ref, qn_ref, kvn_ref, wuq_ref, wk_ref, wv_ref,
                    q_ref, k_ref, v_ref, gm_ref, u_ref, gs_ref, fold_scr):
    xb = x_ref[...].astype(BF16)
    h = jnp.dot(xb, w_in_ref[...], preferred_element_type=F32)
    cos = cos_ref[...]
    sin = sin_ref[...]

    cqn = _rms(h[:, EV_CQ:EV_CKV], qn_ref[...]).astype(BF16)
    q = jnp.dot(cqn, wuq_ref[...], preferred_element_type=F32)
    q = q * jnp.tile(cos, (1, MLA_HEADS)) + _rotate_pairs(q) * jnp.tile(sin, (1, MLA_HEADS))
    q_ref[...] = (q * Q_SCALE).astype(BF16)

    kpe = h[:, EV_KPE:EV_GM]
    kpe = kpe * cos + _rotate_pairs(kpe) * sin
    ckvn = _rms(h[:, EV_CKV:EV_KPE], kvn_ref[...]).astype(BF16)
    k = jnp.dot(ckvn, wk_ref[...], preferred_element_type=F32) + jnp.tile(kpe, (1, MLA_HEADS))
    k_ref[...] = k.astype(BF16)
    v_ref[...] = jnp.dot(ckvn, wv_ref[...], preferred_element_type=F32).astype(BF16)

    gm_ref[...] = _silu(h[:, EV_GM:EV_U]).astype(BF16)
    rows = x_ref.shape[0] // S5_L
    for ref, val in ((u_ref, h[:, EV_U:EV_GS]), (gs_ref, _silu(h[:, EV_GS:EV_COLS]))):
        for half in range(S5_WIDTH // 128):
            fold_scr[half] = val[:, half * 128:(half + 1) * 128]
        for s in range(S5_L):
            for half in range(S5_WIDTH // 128):
                lo = s * S5_WIDTH + half * 128
                ref[:, lo:lo + 128] = fold_scr[half, pl.ds(s, rows, stride=S5_L), :].astype(BF16)


def _even_in(xf, cos_full, sin_full, w_in, q_norm, kv_norm, w_uq, w_ukv):
    t = xf.shape[0]
    tm = TOKEN_TILE
    pad = HEAD_SLOT - MLA_NOPE - MLA_ROPE
    cq, ckv, kr, gm, u, gs = jnp.split(
        w_in, [MLA_Q_RANK, MLA_Q_RANK + MLA_KV_RANK, MLA_Q_RANK + MLA_KV_RANK + MLA_ROPE,
               MLA_Q_RANK + MLA_KV_RANK + MLA_ROPE + MLA_WIDTH,
               MLA_Q_RANK + MLA_KV_RANK + MLA_ROPE + MLA_WIDTH + S5_WIDTH], axis=1)
    kr = jnp.pad(kr, ((0, 0), (MLA_NOPE, pad)))
    w_in_p = jnp.concatenate([cq, ckv, kr, gm, u, gs], axis=1).astype(BF16)
    wuq_p = jnp.pad(w_uq.reshape(MLA_Q_RANK, MLA_HEADS, MLA_NOPE + MLA_ROPE),
                    ((0, 0), (0, 0), (0, pad))).reshape(MLA_Q_RANK, MLA_HEADS * HEAD_SLOT).astype(BF16)
    wkv = w_ukv.reshape(MLA_KV_RANK, MLA_HEADS, MLA_NOPE + MLA_V)
    wk_p = jnp.pad(wkv[:, :, :MLA_NOPE], ((0, 0), (0, 0), (0, HEAD_SLOT - MLA_NOPE))
                   ).reshape(MLA_KV_RANK, MLA_HEADS * HEAD_SLOT).astype(BF16)
    wv_p = wkv[:, :, MLA_NOPE:].reshape(MLA_KV_RANK, MLA_WIDTH).astype(BF16)

    def row(n):
        return pl.BlockSpec((tm, n), lambda i: (i, 0))

    qk_w = MLA_HEADS * HEAD_SLOT
    lw = S5_L * S5_WIDTH
    folded = jax.ShapeDtypeStruct((t // S5_L, lw), BF16)
    folded_spec = pl.BlockSpec((tm // S5_L, lw), lambda i: (i, 0))
    outs = [jax.ShapeDtypeStruct((t, qk_w), BF16), jax.ShapeDtypeStruct((t, qk_w), BF16),
            jax.ShapeDtypeStruct((t, MLA_WIDTH), BF16), jax.ShapeDtypeStruct((t, MLA_WIDTH), BF16),
            folded, folded]
    return pl.pallas_call(
        _even_in_kernel,
        grid=(t // tm,),
        in_specs=[row(D_MODEL), row(HEAD_SLOT), row(HEAD_SLOT),
                  _const_spec((D_MODEL, EV_COLS)), _const_spec((1, MLA_Q_RANK)),
                  _const_spec((1, MLA_KV_RANK)), _const_spec((MLA_Q_RANK, qk_w)),
                  _const_spec((MLA_KV_RANK, qk_w)), _const_spec((MLA_KV_RANK, MLA_WIDTH))],
        out_specs=[row(qk_w), row(qk_w), row(MLA_WIDTH), row(MLA_WIDTH), folded_spec, folded_spec],
        out_shape=outs,
        scratch_shapes=[pltpu.VMEM((S5_WIDTH // 128, tm, 128), F32)],
        compiler_params=_cparams(1),
        name="even_in_proj",
    )(xf, cos_full, sin_full, w_in_p, q_norm.reshape(1, -1), kv_norm.reshape(1, -1), wuq_p, wk_p, wv_p)


def _attn_kernel(q_ref, k_ref, v_ref, g_ref, o_ref, s_scr):
    seq = q_ref.shape[1]
    t = ATTN_T
    n = seq // t
    nt = (((1,), (1,)), ((), ()))
    hh = pl.program_id(2)

    @pl.when(hh == 0)
    def _():
        o_ref[...] = jnp.zeros_like(o_ref)

    keep = lax.broadcasted_iota(jnp.int32, (t, t), 1) <= lax.broadcasted_iota(jnp.int32, (t, t), 0)
    for kj in range(n):
        r0 = kj * t
        k = k_ref[0, r0:r0 + t, :]
        sd = lax.dot_general(q_ref[0, r0:r0 + t, :], k, nt, preferred_element_type=F32)
        s_scr[r0:r0 + t, r0:r0 + t] = jnp.where(keep, sd, NEG_BIG)
        if r0 + t < seq:
            s_scr[r0 + t:, r0:r0 + t] = lax.dot_general(q_ref[0, r0 + t:, :], k, nt,
                                                         preferred_element_type=F32)

    mine = (lax.broadcasted_iota(jnp.int32, (t, ATTN_GROUP * MLA_V), 1) // MLA_V) == hh
    ps, ls = [], []
    for qi in range(n):
        r0 = qi * t
        s = s_scr[r0:r0 + t, :r0 + t]
        p = jnp.exp2(s - jnp.max(s, axis=1, keepdims=True))
        ls.append(jnp.sum(p, axis=1, keepdims=True))
        ps.append(p.astype(BF16))
    accs = [jnp.dot(p, v_ref[0, :(qi + 1) * t, :], preferred_element_type=F32) for qi, p in enumerate(ps)]
    for qi in range(n):
        rows = slice(qi * t, (qi + 1) * t)
        o = (accs[qi] * (1.0 / ls[qi])) * g_ref[0, rows, :].astype(F32)
        o_ref[0, rows, :] = jnp.where(mine, o.astype(BF16), o_ref[0, rows, :])


def _attention(q, k, v, g, batch, seq):
    q = q.reshape(batch, seq, -1)
    k = k.reshape(batch, seq, -1)
    v = v.reshape(batch, seq, -1)
    g = g.reshape(batch, seq, -1)
    gw = ATTN_GROUP * MLA_V
    head = pl.BlockSpec((1, seq, HEAD_SLOT), lambda b, gi, hh: (b, 0, gi * ATTN_GROUP + hh))
    group = pl.BlockSpec((1, seq, gw), lambda b, gi, hh: (b, 0, gi))
    o = pl.pallas_call(
        _attn_kernel,
        grid=(batch, MLA_HEADS // ATTN_GROUP, ATTN_GROUP),
        in_specs=[head, head, group, group],
        out_specs=group,
        out_shape=jax.ShapeDtypeStruct((batch, seq, MLA_WIDTH), BF16),
        scratch_shapes=[pltpu.VMEM((seq, seq), F32)],
        compiler_params=_cparams(3),
        name="mla_attention",
    )(q, k, v, g)
    return o.reshape(batch * seq, MLA_WIDTH)


def _s5_matrices(a_re, a_im, log_dt, b_re, b_im, c_re, c_im):
    L, G, P, H = S5_L, S5_GROUPS, S5_STATE, S5_GROUP
    dt = jnp.exp(log_dt.astype(F32))[:, None]
    ar, ai = a_re.astype(F32), a_im.astype(F32)
    mag = jnp.exp(ar * dt)
    lr, li = mag * jnp.cos(ai * dt), mag * jnp.sin(ai * dt)
    den = ar * ar + ai * ai
    nr, ni = lr - 1.0, li
    zr = (nr * ar + ni * ai) / den
    zi = (ni * ar - nr * ai) / den
    br, bi = b_re.astype(F32), b_im.astype(F32)
    bbr = zr[..., None] * br - zi[..., None] * bi
    bbi = zr[..., None] * bi + zi[..., None] * br
    cr, ci = c_re.astype(F32), c_im.astype(F32)

    def powers(j):
        j = j.astype(F32)[:, None, None]
        pmag = jnp.exp(j * (ar * dt))
        return pmag * jnp.cos(j * (ai * dt)), pmag * jnp.sin(j * (ai * dt))

    gp, w = G * P, G * H
    pr, pi = powers(jnp.arange(L + 1))
    prow = jnp.concatenate([pr.reshape(L + 1, gp), pi.reshape(L + 1, gp)], axis=0)
    prow3 = prow.reshape(2 * (L + 1), 1, gp)
    pcol3 = prow.reshape(2 * (L + 1), gp, 1)

    same_b = (jnp.arange(w)[:, None] // H) == (jnp.arange(gp)[None, :] // P)
    same_c = (jnp.arange(gp)[:, None] // P) == (jnp.arange(w)[None, :] // H)
    bd_br = jnp.where(same_b, jnp.tile(bbr.transpose(2, 0, 1).reshape(H, gp), (G, 1)), 0.0)
    bd_bi = jnp.where(same_b, jnp.tile(bbi.transpose(2, 0, 1).reshape(H, gp), (G, 1)), 0.0)
    bd_cr = jnp.where(same_c, jnp.tile(cr.transpose(0, 2, 1).reshape(gp, H), (1, G)), 0.0)
    bd_ci = jnp.where(same_c, jnp.tile(ci.transpose(0, 2, 1).reshape(gp, H), (1, G)), 0.0)

    lw = L * w
    bmat, tmat, cmat = pl.pallas_call(
        _s5_prepare_kernel,
        grid=(L,),
        in_specs=[_const_spec((w, gp)), _const_spec((w, gp)), _const_spec((gp, w)), _const_spec((gp, w)),
                  _const_spec((2 * (L + 1), gp)),
                  pl.BlockSpec((1, 1, gp), lambda i: (L - 1 - i, 0, 0)),
                  pl.BlockSpec((1, 1, gp), lambda i: (2 * L - i, 0, 0)),
                  pl.BlockSpec((1, gp, 1), lambda i: (i + 1, 0, 0)),
                  pl.BlockSpec((1, gp, 1), lambda i: (L + 2 + i, 0, 0))],
        out_specs=[pl.BlockSpec((w, 2 * gp), lambda i: (i, 0)),
                   pl.BlockSpec((w, lw), lambda i: (i, 0)),
                   pl.BlockSpec((2 * gp, w), lambda i: (0, i))],
        out_shape=[jax.ShapeDtypeStruct((lw, 2 * gp), BF16), jax.ShapeDtypeStruct((lw, lw), BF16),
                   jax.ShapeDtypeStruct((2 * gp, lw), BF16)],
        scratch_shapes=[pltpu.VMEM((L, w, w), F32)],
        compiler_params=_cparams(1),
        name="s5_prepare",
    )(bd_br, bd_bi, bd_cr, bd_ci, prow, prow3, prow3, pcol3, pcol3)
    lam = jnp.stack([pr[L].reshape(gp), pi[L].reshape(gp)])
    return bmat, tmat, cmat, lam


def _s5_prepare_kernel(br_ref, bi_ref, cr_ref, ci_ref, prow_ref, rr_ref, ri_ref, cr_pow_ref, ci_pow_ref,
                       bmat_ref, tmat_ref, cmat_ref, k_scr):
    L = S5_L
    w = S5_WIDTH
    gp = S5_GROUPS * S5_STATE
    i = pl.program_id(0)
    br, bi = br_ref[...], bi_ref[...]
    cr, ci = cr_ref[...], ci_ref[...]

    @pl.when(i == 0)
    def _():
        for j in range(L):
            pr = prow_ref[j:j + 1, :]
            pi = prow_ref[L + 1 + j:L + 2 + j, :]
            k_scr[j] = (jnp.dot(br * pr - bi * pi, cr, precision=lax.Precision.HIGHEST,
                                preferred_element_type=F32)
                        - jnp.dot(bi * pr + br * pi, ci, precision=lax.Precision.HIGHEST,
                                  preferred_element_type=F32))

    pr, pi = rr_ref[0], ri_ref[0]
    bmat_ref[:, :gp] = (br * pr - bi * pi).astype(BF16)
    bmat_ref[:, gp:] = (bi * pr + br * pi).astype(BF16)
    for t in range(L):
        lag = jnp.maximum(t - i, 0)
        tmat_ref[:, t * w:(t + 1) * w] = jnp.where(t >= i, k_scr[lag], 0.0).astype(BF16)
    pcr, pci = cr_pow_ref[0], ci_pow_ref[0]
    cmat_ref[:gp, :] = (cr * pcr - ci * pci).astype(BF16)
    cmat_ref[gp:, :] = (-(cr * pci + ci * pcr)).astype(BF16)


def _gelu_tanh(y):
    return 0.5 * y * (1.0 + jnp.tanh(math.sqrt(2.0 / math.pi) * (y + 0.044715 * (y * y * y))))


def _s5_kernel(u_ref, gs_ref, bmat_ref, tmat_ref, cmat_ref, lam_ref, dskip_ref, wglu_ref, bglu_ref,
               o_ref, x_scr, h_scr):
    w = S5_WIDTH
    gp = S5_GROUPS * S5_STATE
    rows = u_ref.shape[0]
    per_batch = rows // S5_NB
    x_scr[...] = jnp.dot(u_ref[...], bmat_ref[...], preferred_element_type=F32)
    lam_r = lam_ref[0:1, :]
    lam_i = lam_ref[1:2, :]

    def block(i, hs):
        out = []
        for b in range(S5_NB):
            hr, hi = hs[b]
            r0 = pl.multiple_of(b * per_batch + i * 8, 8)
            x8 = x_scr[pl.ds(r0, 8), :]
            starts_r, starts_i = [], []
            for jj in range(8):
                starts_r.append(hr)
                starts_i.append(hi)
                xr = x8[jj:jj + 1, :gp]
                xi = x8[jj:jj + 1, gp:]
                hr, hi = lam_r * hr - lam_i * hi + xr, lam_r * hi + lam_i * hr + xi
            h_scr[pl.ds(r0, 8), :gp] = jnp.concatenate(starts_r, axis=0)
            h_scr[pl.ds(r0, 8), gp:] = jnp.concatenate(starts_i, axis=0)
            out.append((hr, hi))
        return tuple(out)

    zero = jnp.zeros((1, gp), F32)
    lax.fori_loop(0, per_batch // 8, block, tuple((zero, zero) for _ in range(S5_NB)))

    hs = h_scr[...].astype(BF16)
    cols = [slice(t * w, (t + 1) * w) for t in range(S5_L)]
    ys = [jnp.dot(u_ref[:, :(t + 1) * w], tmat_ref[:(t + 1) * w, cols[t]], preferred_element_type=F32)
          + jnp.dot(hs, cmat_ref[:, cols[t]], preferred_element_type=F32) for t in range(S5_L)]
    ys = [_gelu_tanh(y + dskip_ref[...] * u_ref[:, cl].astype(F32)) for y, cl in zip(ys, cols)]
    zs = [jnp.dot(y.astype(BF16), wglu_ref[...], preferred_element_type=F32) for y in ys]
    for t, (y, z) in enumerate(zip(ys, zs)):
        o = (y * _sigmoid(z + bglu_ref[...])) * gs_ref[:, cols[t]].astype(F32)
        for half in range(w // 128):
            o_ref[half, pl.ds(t, rows, stride=S5_L), :] = o[:, half * 128:(half + 1) * 128]


def _s5(u, gs, batch, seq, a_re, a_im, log_dt, b_re, b_im, c_re, c_im, d_skip, w_glu, b_glu):
    t = batch * seq
    lw = S5_L * S5_WIDTH
    gp2 = 2 * S5_GROUPS * S5_STATE
    bmat, tmat, cmat, lam = _s5_matrices(a_re, a_im, log_dt, b_re, b_im, c_re, c_im)
    rows = S5_NB * seq // S5_L
    halves = S5_WIDTH // 128
    blk = pl.BlockSpec((rows, lw), lambda i: (i, 0))
    return pl.pallas_call(
        _s5_kernel,
        grid=(batch // S5_NB,),
        in_specs=[blk, blk, _const_spec((lw, gp2), True), _const_spec((lw, lw), True),
                  _const_spec((gp2, lw), True), _const_spec((2, gp2 // 2)),
                  _const_spec((1, S5_WIDTH)), _const_spec((S5_WIDTH, S5_WIDTH)), _const_spec((1, S5_WIDTH))],
        out_specs=pl.BlockSpec((halves, rows * S5_L, 128), lambda i: (0, i, 0)),
        out_shape=jax.ShapeDtypeStruct((halves, t, 128), F32),
        scratch_shapes=[pltpu.VMEM((rows, gp2), F32), pltpu.VMEM((rows, gp2), F32)],
        compiler_params=_cparams(1),
        name="s5_mixer",
    )(u, gs, bmat, tmat, cmat, lam, d_skip.reshape(1, -1).astype(F32), w_glu.astype(BF16),
      b_glu.reshape(1, -1).astype(F32))


def _out_ln_kernel(n_act, x_ref, *refs):
    acts = refs[:n_act]
    ws = refs[n_act:2 * n_act]
    g_ref, b_ref, o_ref = refs[2 * n_act:]
    def operand(ref):
        if len(ref.shape) == 2:
            return ref[...]
        return jnp.concatenate([ref[i] for i in range(ref.shape[0])], axis=1).astype(BF16)

    y = jnp.dot(operand(acts[0]), ws[0][...], preferred_element_type=F32)
    for a, w in zip(acts[1:], ws[1:]):
        y = y + jnp.dot(operand(a), w[...], preferred_element_type=F32)
    z = DEEPNORM_ALPHA * x_ref[...] + y
    mu = jnp.mean(z, axis=-1, keepdims=True)
    zc = z - mu
    var = jnp.mean(zc * zc, axis=-1, keepdims=True)
    o_ref[...] = zc * lax.rsqrt(var + LN_EPS) * g_ref[...] + b_ref[...]


def _out_ln(xf, acts, ws, ln_g, ln_b):
    t = xf.shape[0]
    tm = TOKEN_TILE
    n = len(acts)

    def row(c):
        return pl.BlockSpec((tm, c), lambda i: (i, 0))

    def act_spec(a):
        if a.ndim == 2:
            return row(a.shape[1])
        return pl.BlockSpec((a.shape[0], tm, a.shape[2]), lambda i: (0, i, 0))

    return pl.pallas_call(
        functools.partial(_out_ln_kernel, n),
        grid=(t // tm,),
        in_specs=[row(D_MODEL)] + [act_spec(a) for a in acts] + [_const_spec(w.shape) for w in ws]
                 + [_const_spec((1, D_MODEL)), _const_spec((1, D_MODEL))],
        out_specs=row(D_MODEL),
        out_shape=jax.ShapeDtypeStruct((t, D_MODEL), F32),
        compiler_params=_cparams(1),
        name="out_proj_layernorm",
    )(xf, *acts, *ws, ln_g.reshape(1, -1), ln_b.reshape(1, -1))


OD_Q = 0
OD_K = OD_Q + GLA_KEY_WIDTH
OD_V = OD_K + GLA_KEY_WIDTH
OD_GL = OD_V + GLA_WIDTH
OD_G = OD_GL + 128
OD_COLS = OD_G + GLA_WIDTH


def _odd_in_kernel(x_ref, w_ref, wgk_ref, bgk_ref, q_ref, k_ref, v_ref, la_ref, g_ref):
    xb = x_ref[...].astype(BF16)

    def proj(lo, hi):
        return jnp.dot(xb, w_ref[:, lo:hi], preferred_element_type=F32)

    q_ref[...] = proj(OD_Q, OD_K).astype(BF16)
    k_ref[...] = proj(OD_K, OD_V).astype(BF16)
    v_ref[...] = proj(OD_V, OD_GL).astype(BF16)
    z = jnp.dot(proj(OD_GL, OD_G).astype(BF16), wgk_ref[...], preferred_element_type=F32) + bgk_ref[...]
    log_sig = -(jnp.maximum(-z, 0.0) + jnp.log1p(jnp.exp(-jnp.abs(z))))
    la_ref[...] = log_sig / GLA_GATE_NORM
    g_ref[...] = _silu(proj(OD_G, OD_COLS)).astype(BF16)


def _odd_in(xf, w_in, w_gk2, b_gk):
    t = xf.shape[0]
    tm = TOKEN_TILE
    q, k, v, gl, g = jnp.split(w_in, [OD_K, OD_V, OD_GL, OD_GL + GLA_GATE_RANK], axis=1)
    gl = jnp.pad(gl, ((0, 0), (0, 128 - GLA_GATE_RANK)))
    w_p = jnp.concatenate([q, k, v, gl, g], axis=1).astype(BF16)
    wgk_p = jnp.pad(w_gk2, ((0, 128 - GLA_GATE_RANK), (0, 0))).astype(BF16)

    def row(c):
        return pl.BlockSpec((tm, c), lambda i: (i, 0))

    outs = [jax.ShapeDtypeStruct((t, GLA_KEY_WIDTH), BF16), jax.ShapeDtypeStruct((t, GLA_KEY_WIDTH), BF16),
            jax.ShapeDtypeStruct((t, GLA_WIDTH), BF16), jax.ShapeDtypeStruct((t, GLA_KEY_WIDTH), F32),
            jax.ShapeDtypeStruct((t, GLA_WIDTH), BF16)]
    return pl.pallas_call(
        _odd_in_kernel,
        grid=(t // tm,),
        in_specs=[row(D_MODEL), _const_spec((D_MODEL, OD_COLS)), _const_spec((128, GLA_KEY_WIDTH)),
                  _const_spec((1, GLA_KEY_WIDTH))],
        out_specs=[row(GLA_KEY_WIDTH), row(GLA_KEY_WIDTH), row(GLA_WIDTH), row(GLA_KEY_WIDTH), row(GLA_WIDTH)],
        out_shape=outs,
        compiler_params=_cparams(1),
        name="odd_in_proj",
    )(xf, w_p, wgk_p, b_gk.reshape(1, -1).astype(F32))


def _gla_kernel(q_ref, k_ref, v_ref, la_ref, g_ref, gn_ref, o_ref, bc_scr):
    c, sb = GLA_CHUNK, GLA_SUPER
    cps = sb // c
    seq = q_ref.shape[1]
    nch = seq // c
    nt = (((1,), (1,)), ((), ()))
    tn = (((0,), (0,)), ((), ()))
    ri = lax.broadcasted_iota(jnp.int32, (sb, sb), 0)
    ci = lax.broadcasted_iota(jnp.int32, (sb, sb), 1)
    valid = jnp.logical_and(ci <= ri, ci >= (ri // c) * c)
    tri = jnp.where(valid, 1.0, 0.0).astype(BF16)
    scale = GLA_DK ** -0.5

    blocks = [slice(s * sb, (s + 1) * sb) for s in range(seq // sb)]
    chunks = [slice(n * c, (n + 1) * c) for n in range(nch)]

    bcs = []
    for rows in blocks:
        ga = la_ref[0, rows, :]
        ga_hi = ga.astype(BF16)
        ga_lo = (ga - ga_hi.astype(F32)).astype(BF16)
        cs = jnp.dot(tri, jnp.concatenate([ga_hi, ga_lo], axis=1), preferred_element_type=F32)
        bc = cs[:, :GLA_DK] + cs[:, GLA_DK:]
        bc_scr[rows, :] = bc
        bcs.append(bc.reshape(cps, c, GLA_DK))

    qds, k_invs, k_ends = [], [], []
    for rows, bc3 in zip(blocks, bcs):
        bl3 = bc3[:, c - 1:c, :]
        q3 = q_ref[0, rows, :].astype(F32).reshape(cps, c, GLA_DK) * scale
        k3 = k_ref[0, rows, :].astype(F32).reshape(cps, c, GLA_DK)
        qds.append((q3 * jnp.exp(bc3)).reshape(sb, GLA_DK).astype(BF16))
        k_invs.append((k3 * jnp.exp(-bc3)).reshape(sb, GLA_DK).astype(BF16))
        k_ends.append((k3 * jnp.exp(bl3 - bc3)).reshape(sb, GLA_DK).astype(BF16))

    atts = [lax.dot_general(qd, k_inv, nt, preferred_element_type=F32) for qd, k_inv in zip(qds, k_invs)]
    atts = [jnp.where(valid, att, 0.0).astype(BF16) for att in atts]
    o_intras = [jnp.dot(att, v_ref[0, rows, :], preferred_element_type=F32)
                for att, rows in zip(atts, blocks)]

    def chunk_of(vals, n):
        s, j = divmod(n, cps)
        return vals[s][j * c:(j + 1) * c]

    kvs = [lax.dot_general(chunk_of(k_ends, n), v_ref[0, chunks[n], :], tn, preferred_element_type=F32)
           for n in range(nch - 1)]

    b_last = bc_scr[pl.ds(c - 1, nch, stride=c), :]
    dec = jnp.concatenate([jnp.exp(b_last), jnp.zeros((GLA_DK - nch, GLA_DK), F32)], axis=0)
    dec_t = dec.T

    st = jnp.zeros((GLA_DK, GLA_DV), F32)
    states = [st.astype(BF16)]
    for n in range(nch - 1):
        st = st * dec_t[:, n:n + 1] + kvs[n]
        states.append(st.astype(BF16))

    o_inters = [jnp.dot(chunk_of(qds, n), states[n], preferred_element_type=F32) for n in range(nch)]
    for n in range(nch):
        o = chunk_of(o_intras, n) + o_inters[n]
        o = _rms(o, gn_ref[...]) * g_ref[0, chunks[n], :].astype(F32)
        o_ref[0, chunks[n], :] = o.astype(BF16)


def _gla(q, k, v, la, g, g_norm, batch, seq):
    q = q.reshape(batch, seq, -1)
    k = k.reshape(batch, seq, -1)
    v = v.reshape(batch, seq, -1)
    la = la.reshape(batch, seq, -1)
    g = g.reshape(batch, seq, -1)

    def spec(c):
        return pl.BlockSpec((1, seq, c), lambda b, h: (b, 0, h))

    o = pl.pallas_call(
        _gla_kernel,
        grid=(batch, GLA_HEADS),
        in_specs=[spec(GLA_DK), spec(GLA_DK), spec(GLA_DV), spec(GLA_DK), spec(GLA_DV),
                  _const_spec((1, GLA_DV))],
        out_specs=spec(GLA_DV),
        out_shape=jax.ShapeDtypeStruct((batch, seq, GLA_WIDTH), BF16),
        scratch_shapes=[pltpu.VMEM((seq, GLA_DK), F32)],
        compiler_params=_cparams(2),
        name="gla_mixer",
    )(q, k, v, la, g, g_norm.reshape(1, -1).astype(F32))
    return o.reshape(batch * seq, GLA_WIDTH)


def kernel(x, positions, ln_g, ln_b, even_w_in, mla_q_norm, mla_kv_norm, mla_w_uq, mla_w_ukv, s5_a_re, s5_a_im, s5_log_dt, s5_b_re, s5_b_im, s5_c_re, s5_c_im, s5_d, s5_w_glu, s5_b_glu, even_w_out, odd_w_in, gla_w_gk2, gla_b_gk, gla_g_norm, odd_w_out):
    batch, seq, _ = x.shape
    xf = x.reshape(batch * seq, D_MODEL)
    cos_full, sin_full = _rope_tables(positions)
    for layer in range(DEPTH):
        j = layer // 2
        if layer % 2 == 0:
            q, k, v, gm, u, gs = _even_in(xf, cos_full, sin_full, even_w_in[j], mla_q_norm[j],
                                          mla_kv_norm[j], mla_w_uq[j], mla_w_ukv[j])
            o_mla = _attention(q, k, v, gm, batch, seq)
            o_s5 = _s5(u, gs, batch, seq, s5_a_re[j], s5_a_im[j], s5_log_dt[j], s5_b_re[j], s5_b_im[j],
                       s5_c_re[j], s5_c_im[j], s5_d[j], s5_w_glu[j], s5_b_glu[j])
            w_out = even_w_out[j].astype(BF16)
            xf = _out_ln(xf, [o_mla, o_s5], [w_out[:MLA_WIDTH], w_out[MLA_WIDTH:]], ln_g[layer], ln_b[layer])
        else:
            q, k, v, la, g = _odd_in(xf, odd_w_in[j], gla_w_gk2[j], gla_b_gk[j])
            o = _gla(q, k, v, la, g, gla_g_norm[j], batch, seq)
            xf = _out_ln(xf, [o], [odd_w_out[j].astype(BF16)], ln_g[layer], ln_b[layer])
    return xf.reshape(batch, seq, D_MODEL)
```

```python
import functools
import math

import jax
import jax.numpy as jnp
from jax import lax
from jax.experimental import pallas as pl
from jax.experimental.pallas import tpu as pltpu

F32 = jnp.float32
BF16 = jnp.bfloat16

D_MODEL = 1024
DEPTH = 2

MLA_HEADS = 8
MLA_NOPE = 64
MLA_ROPE = 32
MLA_V = 64
MLA_Q_RANK = 256
MLA_KV_RANK = 128
MLA_WIDTH = MLA_HEADS * MLA_V
ROPE_BASE = 10000.0
HEAD_SLOT = 128

S5_WIDTH = 256
S5_GROUP = 16
S5_GROUPS = S5_WIDTH // S5_GROUP
S5_STATE = 64
S5_L = 8
S5_NB = 2

GLA_HEADS = 4
GLA_KEY_WIDTH = D_MODEL // 2
GLA_WIDTH = D_MODEL
GLA_DK = GLA_KEY_WIDTH // GLA_HEADS
GLA_DV = GLA_WIDTH // GLA_HEADS
GLA_GATE_RANK = 16
GLA_GATE_NORM = 16.0
GLA_CHUNK = 64
GLA_SUPER = 256

DEEPNORM_ALPHA = (2 * DEPTH) ** 0.25
LN_EPS = 1e-5
RMS_EPS = 1e-6

TOKEN_TILE = 1024
ATTN_T = 256
ATTN_GROUP = 4
NEG_BIG = -1e30
VMEM_LIMIT = 56 * 1024 * 1024


def _cparams(n_axes):
    return pltpu.CompilerParams(dimension_semantics=("arbitrary",) * n_axes,
                                vmem_limit_bytes=VMEM_LIMIT)


def _const_spec(shape, single=False):
    nd = len(shape)
    if single:
        return pl.BlockSpec(shape, lambda *_: (0,) * nd, pipeline_mode=pl.Buffered(1))
    return pl.BlockSpec(shape, lambda *_: (0,) * nd)


def _sigmoid(x):
    return 1.0 / (1.0 + jnp.exp(-x))


def _silu(x):
    return x * _sigmoid(x)


def _rms(x, g):
    return (x * lax.rsqrt(jnp.mean(x * x, axis=-1, keepdims=True) + RMS_EPS)) * g


ROPE_PACK = 128 // (MLA_ROPE // 2)


def _rope_kernel(pos_ref, invf_ref, cos_ref, sin_ref):
    half = MLA_ROPE // 2
    rb = pos_ref.shape[0]
    pos = pos_ref[...].astype(F32)
    lane = lax.broadcasted_iota(jnp.int32, (rb, 128), 1)
    owner = lane // half
    posx = jnp.zeros((rb, 128), F32)
    for s in range(ROPE_PACK):
        posx = jnp.where(owner == s, pos[:, s:s + 1], posx)
    ang = posx * invf_ref[...]
    c = jnp.cos(ang)
    sn = jnp.sin(ang)
    first = (lane & 112) == MLA_NOPE
    second = (lane & 112) == MLA_NOPE + half
    for s in range(ROPE_PACK):
        k1 = (MLA_NOPE - half * s) % 128
        k2 = (MLA_NOPE + half - half * s) % 128
        cf = jnp.where(first, pltpu.roll(c, k1, 1), jnp.where(second, pltpu.roll(c, k2, 1), 1.0))
        sf = jnp.where(first, -pltpu.roll(sn, k1, 1), jnp.where(second, pltpu.roll(sn, k2, 1), 0.0))
        cos_ref[pl.ds(s, rb, stride=ROPE_PACK), :] = cf
        sin_ref[pl.ds(s, rb, stride=ROPE_PACK), :] = sf


def _rope_tables(positions):
    t = positions.size
    half = MLA_ROPE // 2
    pos = positions.reshape(t // ROPE_PACK, ROPE_PACK)
    inv_freq = ROPE_BASE ** (-jnp.arange(half, dtype=F32) / half)
    invf = jnp.tile(inv_freq, ROPE_PACK).reshape(1, 128)
    rb = 512
    out = jax.ShapeDtypeStruct((t, HEAD_SLOT), F32)
    return pl.pallas_call(
        _rope_kernel,
        grid=(t // ROPE_PACK // rb,),
        in_specs=[pl.BlockSpec((rb, ROPE_PACK), lambda i: (i, 0)), _const_spec((1, 128))],
        out_specs=[pl.BlockSpec((rb * ROPE_PACK, HEAD_SLOT), lambda i: (i, 0))] * 2,
        out_shape=[out, out],
        compiler_params=_cparams(1),
        name="rope_tables",
    )(pos, invf)


EV_CQ = 0
EV_CKV = EV_CQ + MLA_Q_RANK
EV_KPE = EV_CKV + MLA_KV_RANK
EV_GM = EV_KPE + HEAD_SLOT
EV_U = EV_GM + MLA_WIDTH
EV_GS = EV_U + S5_WIDTH
EV_COLS = EV_GS + S5_WIDTH
Q_SCALE = (MLA_NOPE + MLA_ROPE) ** -0.5 * math.log2(math.e)


def _rotate_pairs(x):
    n = x.shape[-1]
    lane = lax.broadcasted_iota(jnp.int32, x.shape, 1)
    lo = (lane & 112) == 64
    return jnp.where(lo, pltpu.roll(x, n - MLA_ROPE // 2, 1), pltpu.roll(x, MLA_ROPE // 2, 1))


def _even_in_kernel(x_ref, cos_ref, sin_ref, w_in_ref, qn_ref, kvn_ref, wuq_ref, wk_ref, wv_ref,
                    q_ref, k_ref, v_ref, gm_ref, u_ref, gs_ref, fold_scr):
    xb = x_ref[...].astype(BF16)
    h = jnp.dot(xb, w_in_ref[...], preferred_element_type=F32)
    cos = cos_ref[...]
    sin = sin_ref[...]

    cqn = _rms(h[:, EV_CQ:EV_CKV], qn_ref[...]).astype(BF16)
    q = jnp.dot(cqn, wuq_ref[...], preferred_element_type=F32)
    q = q * jnp.tile(cos, (1, MLA_HEADS)) + _rotate_pairs(q) * jnp.tile(sin, (1, MLA_HEADS))
    q_ref[...] = (q * Q_SCALE).astype(BF16)

    kpe = h[:, EV_KPE:EV_GM]
    kpe = kpe * cos + _rotate_pairs(kpe) * sin
    ckvn = _rms(h[:, EV_CKV:EV_KPE], kvn_ref[...]).astype(BF16)
    k = jnp.dot(ckvn, wk_ref[...], preferred_element_type=F32) + jnp.tile(kpe, (1, MLA_HEADS))
    k_ref[...] = k.astype(BF16)
    v_ref[...] = jnp.dot(ckvn, wv_ref[...], preferred_element_type=F32).astype(BF16)

    gm_ref[...] = _silu(h[:, EV_GM:EV_U]).astype(BF16)
    rows = x_ref.shape[0] // S5_L
    for ref, val in ((u_ref, h[:, EV_U:EV_GS]), (gs_ref, _silu(h[:, EV_GS:EV_COLS]))):
        for half in range(S5_WIDTH // 128):
            fold_scr[half] = val[:, half * 128:(half + 1) * 128]
        for s in range(S5_L):
            for half in range(S5_WIDTH // 128):
                lo = s * S5_WIDTH + half * 128
                ref[:, lo:lo + 128] = fold_scr[half, pl.ds(s, rows, stride=S5_L), :].astype(BF16)


def _even_in(xf, cos_full, sin_full, w_in, q_norm, kv_norm, w_uq, w_ukv):
    t = xf.shape[0]
    tm = TOKEN_TILE
    pad = HEAD_SLOT - MLA_NOPE - MLA_ROPE
    cq, ckv, kr, gm, u, gs = jnp.split(
        w_in, [MLA_Q_RANK, MLA_Q_RANK + MLA_KV_RANK, MLA_Q_RANK + MLA_KV_RANK + MLA_ROPE,
               MLA_Q_RANK + MLA_KV_RANK + MLA_ROPE + MLA_WIDTH,
               MLA_Q_RANK + MLA_KV_RANK + MLA_ROPE + MLA_WIDTH + S5_WIDTH], axis=1)
    kr = jnp.pad(kr, ((0, 0), (MLA_NOPE, pad)))
    w_in_p = jnp.concatenate([cq, ckv, kr, gm, u, gs], axis=1).astype(BF16)
    wuq_p = jnp.pad(w_uq.reshape(MLA_Q_RANK, MLA_HEADS, MLA_NOPE + MLA_ROPE),
                    ((0, 0), (0, 0), (0, pad))).reshape(MLA_Q_RANK, MLA_HEADS * HEAD_SLOT).astype(BF16)
    wkv = w_ukv.reshape(MLA_KV_RANK, MLA_HEADS, MLA_NOPE + MLA_V)
    wk_p = jnp.pad(wkv[:, :, :MLA_NOPE], ((0, 0), (0, 0), (0, HEAD_SLOT - MLA_NOPE))
                   ).reshape(MLA_KV_RANK, MLA_HEADS * HEAD_SLOT).astype(BF16)
    wv_p = wkv[:, :, MLA_NOPE:].reshape(MLA_KV_RANK, MLA_WIDTH).astype(BF16)

    def row(n):
        return pl.BlockSpec((tm, n), lambda i: (i, 0))

    qk_w = MLA_HEADS * HEAD_SLOT
    lw = S5_L * S5_WIDTH
    folded = jax.ShapeDtypeStruct((t // S5_L, lw), BF16)
    folded_spec = pl.BlockSpec((tm // S5_L, lw), lambda i: (i, 0))
    outs = [jax.ShapeDtypeStruct((t, qk_w), BF16), jax.ShapeDtypeStruct((t, qk_w), BF16),
            jax.ShapeDtypeStruct((t, MLA_WIDTH), BF16), jax.ShapeDtypeStruct((t, MLA_WIDTH), BF16),
            folded, folded]
    return pl.pallas_call(
        _even_in_kernel,
        grid=(t // tm,),
        in_specs=[row(D_MODEL), row(HEAD_SLOT), row(HEAD_SLOT),
                  _const_spec((D_MODEL, EV_COLS)), _const_spec((1, MLA_Q_RANK)),
                  _const_spec((1, MLA_KV_RANK)), _const_spec((MLA_Q_RANK, qk_w)),
                  _const_spec((MLA_KV_RANK, qk_w)), _const_spec((MLA_KV_RANK, MLA_WIDTH))],
        out_specs=[row(qk_w), row(qk_w), row(MLA_WIDTH), row(MLA_WIDTH), folded_spec, folded_spec],
        out_shape=outs,
        scratch_shapes=[pltpu.VMEM((S5_WIDTH // 128, tm, 128), F32)],
        compiler_params=_cparams(1),
        name="even_in_proj",
    )(xf, cos_full, sin_full, w_in_p, q_norm.reshape(1, -1), kv_norm.reshape(1, -1), wuq_p, wk_p, wv_p)


def _attn_kernel(q_ref, k_ref, v_ref, g_ref, o_ref):
    seq = q_ref.shape[1]
    t = ATTN_T
    n = seq // t
    nt = (((1,), (1,)), ((), ()))
    hh = pl.program_id(2)

    @pl.when(hh == 0)
    def _():
        o_ref[...] = jnp.zeros_like(o_ref)

    keep = lax.broadcasted_iota(jnp.int32, (t, t), 1) <= lax.broadcasted_iota(jnp.int32, (t, t), 0)
    mine = (lax.broadcasted_iota(jnp.int32, (t, ATTN_GROUP * MLA_V), 1) // MLA_V) == hh

    def scores(qi):
        rows = slice(qi * t, (qi + 1) * t)
        q = q_ref[0, rows, :]
        sd = jnp.where(keep, lax.dot_general(q, k_ref[0, rows, :], nt, preferred_element_type=F32), NEG_BIG)
        if qi == 0:
            return sd
        s = lax.dot_general(q, k_ref[0, :qi * t, :], nt, preferred_element_type=F32)
        return jnp.concatenate([s, sd], axis=1)

    def attend(qi, s):
        rows = slice(qi * t, (qi + 1) * t)
        p = jnp.exp2(s - jnp.max(s, axis=1, keepdims=True))
        l = jnp.sum(p, axis=1, keepdims=True)
        acc = jnp.dot(p.astype(BF16), v_ref[0, :(qi + 1) * t, :], preferred_element_type=F32)
        o = (acc * (1.0 / l)) * g_ref[0, rows, :].astype(F32)
        o_ref[0, rows, :] = jnp.where(mine, o.astype(BF16), o_ref[0, rows, :])

    pending = None
    for qi in reversed(range(n)):
        s = scores(qi)
        if pending is not None:
            attend(*pending)
        pending = (qi, s)
    attend(*pending)


def _attention(q, k, v, g, batch, seq):
    q = q.reshape(batch, seq, -1)
    k = k.reshape(batch, seq, -1)
    v = v.reshape(batch, seq, -1)
    g = g.reshape(batch, seq, -1)
    gw = ATTN_GROUP * MLA_V
    head = pl.BlockSpec((1, seq, HEAD_SLOT), lambda b, gi, hh: (b, 0, gi * ATTN_GROUP + hh))
    group = pl.BlockSpec((1, seq, gw), lambda b, gi, hh: (b, 0, gi))
    o = pl.pallas_call(
        _attn_kernel,
        grid=(batch, MLA_HEADS // ATTN_GROUP, ATTN_GROUP),
        in_specs=[head, head, group, group],
        out_specs=group,
        out_shape=jax.ShapeDtypeStruct((batch, seq, MLA_WIDTH), BF16),
        compiler_params=_cparams(3),
        name="mla_attention",
    )(q, k, v, g)
    return o.reshape(batch * seq, MLA_WIDTH)


def _s5_matrices(a_re, a_im, log_dt, b_re, b_im, c_re, c_im):
    L, G, P, H = S5_L, S5_GROUPS, S5_STATE, S5_GROUP
    dt = jnp.exp(log_dt.astype(F32))[:, None]
    ar, ai = a_re.astype(F32), a_im.astype(F32)
    mag = jnp.exp(ar * dt)
    lr, li = mag * jnp.cos(ai * dt), mag * jnp.sin(ai * dt)
    den = ar * ar + ai * ai
    nr, ni = lr - 1.0, li
    zr = (nr * ar + ni * ai) / den
    zi = (ni * ar - nr * ai) / den
    br, bi = b_re.astype(F32), b_im.astype(F32)
    bbr = zr[..., None] * br - zi[..., None] * bi
    bbi = zr[..., None] * bi + zi[..., None] * br
    cr, ci = c_re.astype(F32), c_im.astype(F32)

    def powers(j):
        j = j.astype(F32)[:, None, None]
        pmag = jnp.exp(j * (ar * dt))
        return pmag * jnp.cos(j * (ai * dt)), pmag * jnp.sin(j * (ai * dt))

    gp, w = G * P, G * H
    pr, pi = powers(jnp.arange(L + 1))
    prow = jnp.concatenate([pr.reshape(L + 1, gp), pi.reshape(L + 1, gp)], axis=0)
    prow3 = prow.reshape(2 * (L + 1), 1, gp)
    pcol3 = prow.reshape(2 * (L + 1), gp, 1)

    same_b = (jnp.arange(w)[:, None] // H) == (jnp.arange(gp)[None, :] // P)
    same_c = (jnp.arange(gp)[:, None] // P) == (jnp.arange(w)[None, :] // H)
    bd_br = jnp.where(same_b, jnp.tile(bbr.transpose(2, 0, 1).reshape(H, gp), (G, 1)), 0.0)
    bd_bi = jnp.where(same_b, jnp.tile(bbi.transpose(2, 0, 1).reshape(H, gp), (G, 1)), 0.0)
    bd_cr = jnp.where(same_c, jnp.tile(cr.transpose(0, 2, 1).reshape(gp, H), (1, G)), 0.0)
    bd_ci = jnp.where(same_c, jnp.tile(ci.transpose(0, 2, 1).reshape(gp, H), (1, G)), 0.0)

    lw = L * w
    bmat, tmat, cmat = pl.pallas_call(
        _s5_prepare_kernel,
        grid=(L,),
        in_specs=[_const_spec((w, gp)), _const_spec((w, gp)), _const_spec((gp, w)), _const_spec((gp, w)),
                  _const_spec((2 * (L + 1), gp)),
                  pl.BlockSpec((1, 1, gp), lambda i: (L - 1 - i, 0, 0)),
                  pl.BlockSpec((1, 1, gp), lambda i: (2 * L - i, 0, 0)),
                  pl.BlockSpec((1, gp, 1), lambda i: (i + 1, 0, 0)),
                  pl.BlockSpec((1, gp, 1), lambda i: (L + 2 + i, 0, 0))],
        out_specs=[pl.BlockSpec((w, 2 * gp), lambda i: (i, 0)),
                   pl.BlockSpec((w, lw), lambda i: (i, 0)),
                   pl.BlockSpec((2 * gp, w), lambda i: (0, i))],
        out_shape=[jax.ShapeDtypeStruct((lw, 2 * gp), BF16), jax.ShapeDtypeStruct((lw, lw), BF16),
                   jax.ShapeDtypeStruct((2 * gp, lw), BF16)],
        scratch_shapes=[pltpu.VMEM((L, w, w), F32)],
        compiler_params=_cparams(1),
        name="s5_prepare",
    )(bd_br, bd_bi, bd_cr, bd_ci, prow, prow3, prow3, pcol3, pcol3)
    lam = jnp.stack([pr[L].reshape(gp), pi[L].reshape(gp)])
    return bmat, tmat, cmat, lam


def _s5_prepare_kernel(br_ref, bi_ref, cr_ref, ci_ref, prow_ref, rr_ref, ri_ref, cr_pow_ref, ci_pow_ref,
                       bmat_ref, tmat_ref, cmat_ref, k_scr):
    L = S5_L
    w = S5_WIDTH
    gp = S5_GROUPS * S5_STATE
    i = pl.program_id(0)
    br, bi = br_ref[...], bi_ref[...]
    cr, ci = cr_ref[...], ci_ref[...]

    @pl.when(i == 0)
    def _():
        for j in range(L):
            pr = prow_ref[j:j + 1, :]
            pi = prow_ref[L + 1 + j:L + 2 + j, :]
            k_scr[j] = (jnp.dot(br * pr - bi * pi, cr, precision=lax.Precision.HIGHEST,
                                preferred_element_type=F32)
                        - jnp.dot(bi * pr + br * pi, ci, precision=lax.Precision.HIGHEST,
                                  preferred_element_type=F32))

    pr, pi = rr_ref[0], ri_ref[0]
    bmat_ref[:, :gp] = (br * pr - bi * pi).astype(BF16)
    bmat_ref[:, gp:] = (bi * pr + br * pi).astype(BF16)
    for t in range(L):
        lag = jnp.maximum(t - i, 0)
        tmat_ref[:, t * w:(t + 1) * w] = jnp.where(t >= i, k_scr[lag], 0.0).astype(BF16)
    pcr, pci = cr_pow_ref[0], ci_pow_ref[0]
    cmat_ref[:gp, :] = (cr * pcr - ci * pci).astype(BF16)
    cmat_ref[gp:, :] = (-(cr * pci + ci * pcr)).astype(BF16)


def _gelu_tanh(y):
    return 0.5 * y * (1.0 + jnp.tanh(math.sqrt(2.0 / math.pi) * (y + 0.044715 * (y * y * y))))


def _s5_kernel(u_ref, gs_ref, bmat_ref, tmat_ref, cmat_ref, lam_ref, dskip_ref, wglu_ref, bglu_ref,
               o_ref, x_scr, h_scr):
    w = S5_WIDTH
    gp = S5_GROUPS * S5_STATE
    rows = u_ref.shape[0]
    per_batch = rows // S5_NB
    x_scr[...] = jnp.dot(u_ref[...], bmat_ref[...], preferred_element_type=F32)
    lam_r = lam_ref[0:1, :]
    lam_i = lam_ref[1:2, :]

    def block(i, hs):
        out = []
        for b in range(S5_NB):
            hr, hi = hs[b]
            r0 = pl.multiple_of(b * per_batch + i * 8, 8)
            x8 = x_scr[pl.ds(r0, 8), :]
            starts_r, starts_i = [], []
            for jj in range(8):
                starts_r.append(hr)
                starts_i.append(hi)
                xr = x8[jj:jj + 1, :gp]
                xi = x8[jj:jj + 1, gp:]
                hr, hi = lam_r * hr - lam_i * hi + xr, lam_r * hi + lam_i * hr + xi
            h_scr[pl.ds(r0, 8), :gp] = jnp.concatenate(starts_r, axis=0)
            h_scr[pl.ds(r0, 8), gp:] = jnp.concatenate(starts_i, axis=0)
            out.append((hr, hi))
        return tuple(out)

    zero = jnp.zeros((1, gp), F32)
    lax.fori_loop(0, per_batch // 8, block, tuple((zero, zero) for _ in range(S5_NB)))

    hs = h_scr[...].astype(BF16)
    cols = [slice(t * w, (t + 1) * w) for t in range(S5_L)]
    ys = [jnp.dot(u_ref[:, :(t + 1) * w], tmat_ref[:(t + 1) * w, cols[t]], preferred_element_type=F32)
          + jnp.dot(hs, cmat_ref[:, cols[t]], preferred_element_type=F32) for t in range(S5_L)]
    ys = [_gelu_tanh(y + dskip_ref[...] * u_ref[:, cl].astype(F32)) for y, cl in zip(ys, cols)]
    zs = [jnp.dot(y.astype(BF16), wglu_ref[...], preferred_element_type=F32) for y in ys]
    for t, (y, z) in enumerate(zip(ys, zs)):
        o = (y * _sigmoid(z + bglu_ref[...])) * gs_ref[:, cols[t]].astype(F32)
        for half in range(w // 128):
            o_ref[half, pl.ds(t, rows, stride=S5_L), :] = o[:, half * 128:(half + 1) * 128]


def _s5(u, gs, batch, seq, a_re, a_im, log_dt, b_re, b_im, c_re, c_im, d_skip, w_glu, b_glu):
    t = batch * seq
    lw = S5_L * S5_WIDTH
    gp2 = 2 * S5_GROUPS * S5_STATE
    bmat, tmat, cmat, lam = _s5_matrices(a_re, a_im, log_dt, b_re, b_im, c_re, c_im)
    rows = S5_NB * seq // S5_L
    halves = S5_WIDTH // 128
    blk = pl.BlockSpec((rows, lw), lambda i: (i, 0))
    return pl.pallas_call(
        _s5_kernel,
        grid=(batch // S5_NB,),
        in_specs=[blk, blk, _const_spec((lw, gp2), True), _const_spec((lw, lw), True),
                  _const_spec((gp2, lw), True), _const_spec((2, gp2 // 2)),
                  _const_spec((1, S5_WIDTH)), _const_spec((S5_WIDTH, S5_WIDTH)), _const_spec((1, S5_WIDTH))],
        out_specs=pl.BlockSpec((halves, rows * S5_L, 128), lambda i: (0, i, 0)),
        out_shape=jax.ShapeDtypeStruct((halves, t, 128), F32),
        scratch_shapes=[pltpu.VMEM((rows, gp2), F32), pltpu.VMEM((rows, gp2), F32)],
        compiler_params=_cparams(1),
        name="s5_mixer",
    )(u, gs, bmat, tmat, cmat, lam, d_skip.reshape(1, -1).astype(F32), w_glu.astype(BF16),
      b_glu.reshape(1, -1).astype(F32))


def _out_ln_kernel(n_act, x_ref, *refs):
    acts = refs[:n_act]
    ws = refs[n_act:2 * n_act]
    g_ref, b_ref, o_ref = refs[2 * n_act:]
    def operand(ref):
        if len(ref.shape) == 2:
            return ref[...]
        return jnp.concatenate([ref[i] for i in range(ref.shape[0])], axis=1).astype(BF16)

    y = jnp.dot(operand(acts[0]), ws[0][...], preferred_element_type=F32)
    for a, w in zip(acts[1:], ws[1:]):
        y = y + jnp.dot(operand(a), w[...], preferred_element_type=F32)
    z = DEEPNORM_ALPHA * x_ref[...] + y
    mu = jnp.mean(z, axis=-1, keepdims=True)
    zc = z - mu
    var = jnp.mean(zc * zc, axis=-1, keepdims=True)
    o_ref[...] = zc * lax.rsqrt(var + LN_EPS) * g_ref[...] + b_ref[...]


def _out_ln(xf, acts, ws, ln_g, ln_b):
    t = xf.shape[0]
    tm = TOKEN_TILE
    n = len(acts)

    def row(c):
        return pl.BlockSpec((tm, c), lambda i: (i, 0))

    def act_spec(a):
        if a.ndim == 2:
            return row(a.shape[1])
        return pl.BlockSpec((a.shape[0], tm, a.shape[2]), lambda i: (0, i, 0))

    return pl.pallas_call(
        functools.partial(_out_ln_kernel, n),
        grid=(t // tm,),
        in_specs=[row(D_MODEL)] + [act_spec(a) for a in acts] + [_const_spec(w.shape) for w in ws]
                 + [_const_spec((1, D_MODEL)), _const_spec((1, D_MODEL))],
        out_specs=row(D_MODEL),
        out_shape=jax.ShapeDtypeStruct((t, D_MODEL), F32),
        compiler_params=_cparams(1),
        name="out_proj_layernorm",
    )(xf, *acts, *ws, ln_g.reshape(1, -1), ln_b.reshape(1, -1))


OD_Q = 0
OD_K = OD_Q + GLA_KEY_WIDTH
OD_V = OD_K + GLA_KEY_WIDTH
OD_GL = OD_V + GLA_WIDTH
OD_G = OD_GL + 128
OD_COLS = OD_G + GLA_WIDTH


def _odd_in_kernel(x_ref, w_ref, wgk_ref, bgk_ref, q_ref, k_ref, v_ref, la_ref, g_ref):
    xb = x_ref[...].astype(BF16)

    def proj(lo, hi):
        return jnp.dot(xb, w_ref[:, lo:hi], preferred_element_type=F32)

    z = jnp.dot(proj(OD_GL, OD_G).astype(BF16), wgk_ref[...], preferred_element_type=F32) + bgk_ref[...]
    g = proj(OD_G, OD_COLS)
    log_sig = -(jnp.maximum(-z, 0.0) + jnp.log1p(jnp.exp(-jnp.abs(z))))
    la_ref[...] = log_sig / GLA_GATE_NORM
    v_ref[...] = proj(OD_V, OD_GL).astype(BF16)
    g_ref[...] = _silu(g).astype(BF16)
    q_ref[...] = proj(OD_Q, OD_K).astype(BF16)
    k_ref[...] = proj(OD_K, OD_V).astype(BF16)


def _odd_in(xf, w_in, w_gk2, b_gk):
    t = xf.shape[0]
    tm = TOKEN_TILE
    q, k, v, gl, g = jnp.split(w_in, [OD_K, OD_V, OD_GL, OD_GL + GLA_GATE_RANK], axis=1)
    gl = jnp.pad(gl, ((0, 0), (0, 128 - GLA_GATE_RANK)))
    w_p = jnp.concatenate([q, k, v, gl, g], axis=1).astype(BF16)
    wgk_p = jnp.pad(w_gk2, ((0, 128 - GLA_GATE_RANK), (0, 0))).astype(BF16)

    def row(c):
        return pl.BlockSpec((tm, c), lambda i: (i, 0))

    outs = [jax.ShapeDtypeStruct((t, GLA_KEY_WIDTH), BF16), jax.ShapeDtypeStruct((t, GLA_KEY_WIDTH), BF16),
            jax.ShapeDtypeStruct((t, GLA_WIDTH), BF16), jax.ShapeDtypeStruct((t, GLA_KEY_WIDTH), F32),
            jax.ShapeDtypeStruct((t, GLA_WIDTH), BF16)]
    return pl.pallas_call(
        _odd_in_kernel,
        grid=(t // tm,),
        in_specs=[row(D_MODEL), _const_spec((D_MODEL, OD_COLS)), _const_spec((128, GLA_KEY_WIDTH)),
                  _const_spec((1, GLA_KEY_WIDTH))],
        out_specs=[row(GLA_KEY_WIDTH), row(GLA_KEY_WIDTH), row(GLA_WIDTH), row(GLA_KEY_WIDTH), row(GLA_WIDTH)],
        out_shape=outs,
        compiler_params=_cparams(1),
        name="odd_in_proj",
    )(xf, w_p, wgk_p, b_gk.reshape(1, -1).astype(F32))


def _gla_kernel(q_ref, k_ref, v_ref, la_ref, g_ref, gn_ref, o_ref, bc_scr):
    c, sb = GLA_CHUNK, GLA_SUPER
    cps = sb // c
    seq = q_ref.shape[1]
    nch = seq // c
    nt = (((1,), (1,)), ((), ()))
    tn = (((0,), (0,)), ((), ()))
    ri = lax.broadcasted_iota(jnp.int32, (sb, sb), 0)
    ci = lax.broadcasted_iota(jnp.int32, (sb, sb), 1)
    valid = jnp.logical_and(ci <= ri, ci >= (ri // c) * c)
    tri = jnp.where(valid, 1.0, 0.0).astype(BF16)
    scale = GLA_DK ** -0.5

    blocks = [slice(s * sb, (s + 1) * sb) for s in range(seq // sb)]
    chunks = [slice(n * c, (n + 1) * c) for n in range(nch)]

    bcs = []
    for rows in blocks:
        ga = la_ref[0, rows, :]
        ga_hi = ga.astype(BF16)
        ga_lo = (ga - ga_hi.astype(F32)).astype(BF16)
        cs = jnp.dot(tri, jnp.concatenate([ga_hi, ga_lo], axis=1), preferred_element_type=F32)
        bc = cs[:, :GLA_DK] + cs[:, GLA_DK:]
        bc_scr[rows, :] = bc
        bcs.append(bc.reshape(cps, c, GLA_DK))

    qds, k_invs, k_ends = [], [], []
    for rows, bc3 in zip(blocks, bcs):
        bl3 = bc3[:, c - 1:c, :]
        q3 = q_ref[0, rows, :].astype(F32).reshape(cps, c, GLA_DK) * scale
        k3 = k_ref[0, rows, :].astype(F32).reshape(cps, c, GLA_DK)
        qds.append((q3 * jnp.exp(bc3)).reshape(sb, GLA_DK).astype(BF16))
        k_invs.append((k3 * jnp.exp(-bc3)).reshape(sb, GLA_DK).astype(BF16))
        k_ends.append((k3 * jnp.exp(bl3 - bc3)).reshape(sb, GLA_DK).astype(BF16))

    atts = [lax.dot_general(qd, k_inv, nt, preferred_element_type=F32) for qd, k_inv in zip(qds, k_invs)]
    atts = [jnp.where(valid, att, 0.0).astype(BF16) for att in atts]
    o_intras = [jnp.dot(att, v_ref[0, rows, :], preferred_element_type=F32)
                for att, rows in zip(atts, blocks)]

    def chunk_of(vals, n):
        s, j = divmod(n, cps)
        return vals[s][j * c:(j + 1) * c]

    kvs = [lax.dot_general(chunk_of(k_ends, n), v_ref[0, chunks[n], :], tn, preferred_element_type=F32)
           for n in range(nch - 1)]

    b_last = bc_scr[pl.ds(c - 1, nch, stride=c), :]
    dec = jnp.concatenate([jnp.exp(b_last), jnp.zeros((GLA_DK - nch, GLA_DK), F32)], axis=0)
    dec_t = dec.T

    st = jnp.zeros((GLA_DK, GLA_DV), F32)
    states = [st.astype(BF16)]
    for n in range(nch - 1):
        st = st * dec_t[:, n:n + 1] + kvs[n]
        states.append(st.astype(BF16))

    o_inters = [jnp.dot(chunk_of(qds, n), states[n], preferred_element_type=F32) for n in range(nch)]
    for n in range(nch):
        o = chunk_of(o_intras, n) + o_inters[n]
        o = _rms(o, gn_ref[...]) * g_ref[0, chunks[n], :].astype(F32)
        o_ref[0, chunks[n], :] = o.astype(BF16)


def _gla(q, k, v, la, g, g_norm, batch, seq):
    q = q.reshape(batch, seq, -1)
    k = k.reshape(batch, seq, -1)
    v = v.reshape(batch, seq, -1)
    la = la.reshape(batch, seq, -1)
    g = g.reshape(batch, seq, -1)

    def spec(c):
        return pl.BlockSpec((1, seq, c), lambda b, h: (b, 0, h))

    o = pl.pallas_call(
        _gla_kernel,
        grid=(batch, GLA_HEADS),
        in_specs=[spec(GLA_DK), spec(GLA_DK), spec(GLA_DV), spec(GLA_DK), spec(GLA_DV),
                  _const_spec((1, GLA_DV))],
        out_specs=spec(GLA_DV),
        out_shape=jax.ShapeDtypeStruct((batch, seq, GLA_WIDTH), BF16),
        scratch_shapes=[pltpu.VMEM((seq, GLA_DK), F32)],
        compiler_params=_cparams(2),
        name="gla_mixer",
    )(q, k, v, la, g, g_norm.reshape(1, -1).astype(F32))
    return o.reshape(batch * seq, GLA_WIDTH)


def kernel(x, positions, ln_g, ln_b, even_w_in, mla_q_norm, mla_kv_norm, mla_w_uq, mla_w_ukv, s5_a_re, s5_a_im, s5_log_dt, s5_b_re, s5_b_im, s5_c_re, s5_c_im, s5_d, s5_w_glu, s5_b_glu, even_w_out, odd_w_in, gla_w_gk2, gla_b_gk, gla_g_norm, odd_w_out):
    batch, seq, _ = x.shape
    xf = x.reshape(batch * seq, D_MODEL)
    cos_full, sin_full = _rope_tables(positions)
    for layer in range(DEPTH):
        j = layer // 2
        if layer % 2 == 0:
            q, k, v, gm, u, gs = _even_in(xf, cos_full, sin_full, even_w_in[j], mla_q_norm[j],
                                          mla_kv_norm[j], mla_w_uq[j], mla_w_ukv[j])
            o_mla = _attention(q, k, v, gm, batch, seq)
            o_s5 = _s5(u, gs, batch, seq, s5_a_re[j], s5_a_im[j], s5_log_dt[j], s5_b_re[j], s5_b_im[j],
                       s5_c_re[j], s5_c_im[j], s5_d[j], s5_w_glu[j], s5_b_glu[j])
            w_out = even_w_out[j].astype(BF16)
            xf = _out_ln(xf, [o_mla, o_s5], [w_out[:MLA_WIDTH], w_out[MLA_WIDTH:]], ln_g[layer], ln_b[layer])
        else:
            q, k, v, la, g = _odd_in(xf, odd_w_in[j], gla_w_gk2[j], gla_b_gk[j])
            o = _gla(q, k, v, la, g, gla_g_norm[j], batch, seq)
            xf = _out_ln(xf, [o], [odd_w_out[j].astype(BF16)], ln_g[layer], ln_b[layer])
    return xf.reshape(batch, seq, D_MODEL)
```

```python
import functools
import math

import jax
import jax.numpy as jnp
from jax import lax
from jax.experimental import pallas as pl
from jax.experimental.pallas import tpu as pltpu

F32 = jnp.float32
BF16 = jnp.bfloat16

D_MODEL = 1024
DEPTH = 2

MLA_HEADS = 8
MLA_NOPE = 64
MLA_ROPE = 32
MLA_V = 64
MLA_Q_RANK = 256
MLA_KV_RANK = 128
MLA_WIDTH = MLA_HEADS * MLA_V
ROPE_BASE = 10000.0
HEAD_SLOT = 128

S5_WIDTH = 256
S5_GROUP = 16
S5_GROUPS = S5_WIDTH // S5_GROUP
S5_STATE = 64
S5_L = 8
S5_NB = 2
S5_HALVES = 2

GLA_HEADS = 4
GLA_KEY_WIDTH = D_MODEL // 2
GLA_WIDTH = D_MODEL
GLA_DK = GLA_KEY_WIDTH // GLA_HEADS
GLA_DV = GLA_WIDTH // GLA_HEADS
GLA_GATE_RANK = 16
GLA_GATE_NORM = 16.0
GLA_CHUNK = 64
GLA_SUPER = 256

DEEPNORM_ALPHA = (2 * DEPTH) ** 0.25
LN_EPS = 1e-5
RMS_EPS = 1e-6

TOKEN_TILE = 1024
ATTN_T = 256
ATTN_GROUP = 4
NEG_BIG = -1e30
VMEM_LIMIT = 56 * 1024 * 1024


def _cparams(n_axes):
    return pltpu.CompilerParams(dimension_semantics=("arbitrary",) * n_axes,
                                vmem_limit_bytes=VMEM_LIMIT)


def _const_spec(shape, single=False):
    nd = len(shape)
    if single:
        return pl.BlockSpec(shape, lambda *_: (0,) * nd, pipeline_mode=pl.Buffered(1))
    return pl.BlockSpec(shape, lambda *_: (0,) * nd)


def _sigmoid(x):
    return 1.0 / (1.0 + jnp.exp(-x))


def _silu(x):
    return x * _sigmoid(x)


def _rms(x, g):
    return (x * lax.rsqrt(jnp.mean(x * x, axis=-1, keepdims=True) + RMS_EPS)) * g


ROPE_PACK = 128 // (MLA_ROPE // 2)


def _rope_kernel(pos_ref, invf_ref, cos_ref, sin_ref):
    half = MLA_ROPE // 2
    rb = pos_ref.shape[0]
    pos = pos_ref[...].astype(F32)
    lane = lax.broadcasted_iota(jnp.int32, (rb, 128), 1)
    owner = lane // half
    posx = jnp.zeros((rb, 128), F32)
    for s in range(ROPE_PACK):
        posx = jnp.where(owner == s, pos[:, s:s + 1], posx)
    ang = posx * invf_ref[...]
    c = jnp.cos(ang)
    sn = jnp.sin(ang)
    first = (lane & 112) == MLA_NOPE
    second = (lane & 112) == MLA_NOPE + half
    for s in range(ROPE_PACK):
        k1 = (MLA_NOPE - half * s) % 128
        k2 = (MLA_NOPE + half - half * s) % 128
        cf = jnp.where(first, pltpu.roll(c, k1, 1), jnp.where(second, pltpu.roll(c, k2, 1), 1.0))
        sf = jnp.where(first, -pltpu.roll(sn, k1, 1), jnp.where(second, pltpu.roll(sn, k2, 1), 0.0))
        cos_ref[pl.ds(s, rb, stride=ROPE_PACK), :] = cf
        sin_ref[pl.ds(s, rb, stride=ROPE_PACK), :] = sf


def _rope_tables(positions):
    t = positions.size
    half = MLA_ROPE // 2
    pos = positions.reshape(t // ROPE_PACK, ROPE_PACK)
    inv_freq = ROPE_BASE ** (-jnp.arange(half, dtype=F32) / half)
    invf = jnp.tile(inv_freq, ROPE_PACK).reshape(1, 128)
    rb = 512
    out = jax.ShapeDtypeStruct((t, HEAD_SLOT), F32)
    return pl.pallas_call(
        _rope_kernel,
        grid=(t // ROPE_PACK // rb,),
        in_specs=[pl.BlockSpec((rb, ROPE_PACK), lambda i: (i, 0)), _const_spec((1, 128))],
        out_specs=[pl.BlockSpec((rb * ROPE_PACK, HEAD_SLOT), lambda i: (i, 0))] * 2,
        out_shape=[out, out],
        compiler_params=_cparams(1),
        name="rope_tables",
    )(pos, invf)


EV_CQ = 0
EV_CKV = EV_CQ + MLA_Q_RANK
EV_KPE = EV_CKV + MLA_KV_RANK
EV_GM = EV_KPE + HEAD_SLOT
EV_U = EV_GM + MLA_WIDTH
EV_GS = EV_U + S5_WIDTH
EV_COLS = EV_GS + S5_WIDTH
Q_SCALE = (MLA_NOPE + MLA_ROPE) ** -0.5 * math.log2(math.e)


def _rotate_pairs(x):
    n = x.shape[-1]
    lane = lax.broadcasted_iota(jnp.int32, x.shape, 1)
    lo = (lane & 112) == 64
    return jnp.where(lo, pltpu.roll(x, n - MLA_ROPE // 2, 1), pltpu.roll(x, MLA_ROPE // 2, 1))


def _even_in_kernel(x_ref, cos_ref, sin_ref, w_in_ref, qn_ref, kvn_ref, wuq_ref, wk_ref, wv_ref,
                    q_ref, k_ref, v_ref, gm_ref, u_ref, gs_ref, fold_scr):
    xb = x_ref[...].astype(BF16)
    h = jnp.dot(xb, w_in_ref[...], preferred_element_type=F32)
    cos = cos_ref[...]
    sin = sin_ref[...]

    cqn = _rms(h[:, EV_CQ:EV_CKV], qn_ref[...]).astype(BF16)
    q = jnp.dot(cqn, wuq_ref[...], preferred_element_type=F32)
    q = q * jnp.tile(cos, (1, MLA_HEADS)) + _rotate_pairs(q) * jnp.tile(sin, (1, MLA_HEADS))
    q_ref[...] = (q * Q_SCALE).astype(BF16)

    kpe = h[:, EV_KPE:EV_GM]
    kpe = kpe * cos + _rotate_pairs(kpe) * sin
    ckvn = _rms(h[:, EV_CKV:EV_KPE], kvn_ref[...]).astype(BF16)
    k = jnp.dot(ckvn, wk_ref[...], preferred_element_type=F32) + jnp.tile(kpe, (1, MLA_HEADS))
    k_ref[...] = k.astype(BF16)
    v_ref[...] = jnp.dot(ckvn, wv_ref[...], preferred_element_type=F32).astype(BF16)

    gm_ref[...] = _silu(h[:, EV_GM:EV_U]).astype(BF16)
    rows = x_ref.shape[0] // S5_L
    hw = S5_WIDTH // S5_HALVES
    for ref, val in ((u_ref, h[:, EV_U:EV_GS]), (gs_ref, _silu(h[:, EV_GS:EV_COLS]))):
        for half in range(S5_HALVES):
            fold_scr[half] = val[:, half * hw:(half + 1) * hw]
        for half in range(S5_HALVES):
            for s in range(S5_L):
                lo = (half * S5_L + s) * hw
                ref[:, lo:lo + hw] = fold_scr[half, pl.ds(s, rows, stride=S5_L), :].astype(BF16)


def _even_in(xf, cos_full, sin_full, w_in, q_norm, kv_norm, w_uq, w_ukv):
    t = xf.shape[0]
    tm = TOKEN_TILE
    pad = HEAD_SLOT - MLA_NOPE - MLA_ROPE
    cq, ckv, kr, gm, u, gs = jnp.split(
        w_in, [MLA_Q_RANK, MLA_Q_RANK + MLA_KV_RANK, MLA_Q_RANK + MLA_KV_RANK + MLA_ROPE,
               MLA_Q_RANK + MLA_KV_RANK + MLA_ROPE + MLA_WIDTH,
               MLA_Q_RANK + MLA_KV_RANK + MLA_ROPE + MLA_WIDTH + S5_WIDTH], axis=1)
    kr = jnp.pad(kr, ((0, 0), (MLA_NOPE, pad)))
    w_in_p = jnp.concatenate([cq, ckv, kr, gm, u, gs], axis=1).astype(BF16)
    wuq_p = jnp.pad(w_uq.reshape(MLA_Q_RANK, MLA_HEADS, MLA_NOPE + MLA_ROPE),
                    ((0, 0), (0, 0), (0, pad))).reshape(MLA_Q_RANK, MLA_HEADS * HEAD_SLOT).astype(BF16)
    wkv = w_ukv.reshape(MLA_KV_RANK, MLA_HEADS, MLA_NOPE + MLA_V)
    wk_p = jnp.pad(wkv[:, :, :MLA_NOPE], ((0, 0), (0, 0), (0, HEAD_SLOT - MLA_NOPE))
                   ).reshape(MLA_KV_RANK, MLA_HEADS * HEAD_SLOT).astype(BF16)
    wv_p = wkv[:, :, MLA_NOPE:].reshape(MLA_KV_RANK, MLA_WIDTH).astype(BF16)

    def row(n):
        return pl.BlockSpec((tm, n), lambda i: (i, 0))

    qk_w = MLA_HEADS * HEAD_SLOT
    lw = S5_L * S5_WIDTH
    folded = jax.ShapeDtypeStruct((t // S5_L, lw), BF16)
    folded_spec = pl.BlockSpec((tm // S5_L, lw), lambda i: (i, 0))
    outs = [jax.ShapeDtypeStruct((t, qk_w), BF16), jax.ShapeDtypeStruct((t, qk_w), BF16),
            jax.ShapeDtypeStruct((t, MLA_WIDTH), BF16), jax.ShapeDtypeStruct((t, MLA_WIDTH), BF16),
            folded, folded]
    return pl.pallas_call(
        _even_in_kernel,
        grid=(t // tm,),
        in_specs=[row(D_MODEL), row(HEAD_SLOT), row(HEAD_SLOT),
                  _const_spec((D_MODEL, EV_COLS)), _const_spec((1, MLA_Q_RANK)),
                  _const_spec((1, MLA_KV_RANK)), _const_spec((MLA_Q_RANK, qk_w)),
                  _const_spec((MLA_KV_RANK, qk_w)), _const_spec((MLA_KV_RANK, MLA_WIDTH))],
        out_specs=[row(qk_w), row(qk_w), row(MLA_WIDTH), row(MLA_WIDTH), folded_spec, folded_spec],
        out_shape=outs,
        scratch_shapes=[pltpu.VMEM((S5_WIDTH // 128, tm, 128), F32)],
        compiler_params=_cparams(1),
        name="even_in_proj",
    )(xf, cos_full, sin_full, w_in_p, q_norm.reshape(1, -1), kv_norm.reshape(1, -1), wuq_p, wk_p, wv_p)


def _attn_kernel(q_ref, k_ref, v_ref, g_ref, o_ref):
    seq = q_ref.shape[1]
    t = ATTN_T
    n = seq // t
    nt = (((1,), (1,)), ((), ()))
    hh = pl.program_id(2)

    @pl.when(hh == 0)
    def _():
        o_ref[...] = jnp.zeros_like(o_ref)

    keep = lax.broadcasted_iota(jnp.int32, (t, t), 1) <= lax.broadcasted_iota(jnp.int32, (t, t), 0)
    mine = (lax.broadcasted_iota(jnp.int32, (t, ATTN_GROUP * MLA_V), 1) // MLA_V) == hh

    def scores(qi):
        rows = slice(qi * t, (qi + 1) * t)
        q = q_ref[0, rows, :]
        sd = jnp.where(keep, lax.dot_general(q, k_ref[0, rows, :], nt, preferred_element_type=F32), NEG_BIG)
        if qi == 0:
            return sd
        s = lax.dot_general(q, k_ref[0, :qi * t, :], nt, preferred_element_type=F32)
        return jnp.concatenate([s, sd], axis=1)

    def attend(qi, s):
        rows = slice(qi * t, (qi + 1) * t)
        p = jnp.exp2(s - jnp.max(s, axis=1, keepdims=True))
        l = jnp.sum(p, axis=1, keepdims=True)
        acc = jnp.dot(p.astype(BF16), v_ref[0, :(qi + 1) * t, :], preferred_element_type=F32)
        o = (acc * (1.0 / l)) * g_ref[0, rows, :].astype(F32)
        o_ref[0, rows, :] = jnp.where(mine, o.astype(BF16), o_ref[0, rows, :])

    pending = None
    for qi in reversed(range(n)):
        s = scores(qi)
        if pending is not None:
            attend(*pending)
        pending = (qi, s)
    attend(*pending)


def _attention(q, k, v, g, batch, seq):
    q = q.reshape(batch, seq, -1)
    k = k.reshape(batch, seq, -1)
    v = v.reshape(batch, seq, -1)
    g = g.reshape(batch, seq, -1)
    gw = ATTN_GROUP * MLA_V
    head = pl.BlockSpec((1, seq, HEAD_SLOT), lambda b, gi, hh: (b, 0, gi * ATTN_GROUP + hh))
    group = pl.BlockSpec((1, seq, gw), lambda b, gi, hh: (b, 0, gi))
    o = pl.pallas_call(
        _attn_kernel,
        grid=(batch, MLA_HEADS // ATTN_GROUP, ATTN_GROUP),
        in_specs=[head, head, group, group],
        out_specs=group,
        out_shape=jax.ShapeDtypeStruct((batch, seq, MLA_WIDTH), BF16),
        compiler_params=_cparams(3),
        name="mla_attention",
    )(q, k, v, g)
    return o.reshape(batch * seq, MLA_WIDTH)


def _s5_matrices(a_re, a_im, log_dt, b_re, b_im, c_re, c_im):
    L, G, P, H = S5_L, S5_GROUPS, S5_STATE, S5_GROUP
    dt = jnp.exp(log_dt.astype(F32))[:, None]
    ar, ai = a_re.astype(F32), a_im.astype(F32)
    mag = jnp.exp(ar * dt)
    lr, li = mag * jnp.cos(ai * dt), mag * jnp.sin(ai * dt)
    den = ar * ar + ai * ai
    nr, ni = lr - 1.0, li
    zr = (nr * ar + ni * ai) / den
    zi = (ni * ar - nr * ai) / den
    br, bi = b_re.astype(F32), b_im.astype(F32)
    bbr = zr[..., None] * br - zi[..., None] * bi
    bbi = zr[..., None] * bi + zi[..., None] * br
    cr, ci = c_re.astype(F32), c_im.astype(F32)

    def powers(j):
        j = j.astype(F32)[:, None, None]
        pmag = jnp.exp(j * (ar * dt))
        return pmag * jnp.cos(j * (ai * dt)), pmag * jnp.sin(j * (ai * dt))

    gp, w = G * P, G * H
    pr, pi = powers(jnp.arange(L + 1))
    prow = jnp.concatenate([pr.reshape(L + 1, gp), pi.reshape(L + 1, gp)], axis=0)
    prow3 = prow.reshape(2 * (L + 1), 1, gp)
    pcol3 = prow.reshape(2 * (L + 1), gp, 1)

    same_b = (jnp.arange(w)[:, None] // H) == (jnp.arange(gp)[None, :] // P)
    same_c = (jnp.arange(gp)[:, None] // P) == (jnp.arange(w)[None, :] // H)
    bd_br = jnp.where(same_b, jnp.tile(bbr.transpose(2, 0, 1).reshape(H, gp), (G, 1)), 0.0)
    bd_bi = jnp.where(same_b, jnp.tile(bbi.transpose(2, 0, 1).reshape(H, gp), (G, 1)), 0.0)
    bd_cr = jnp.where(same_c, jnp.tile(cr.transpose(0, 2, 1).reshape(gp, H), (1, G)), 0.0)
    bd_ci = jnp.where(same_c, jnp.tile(ci.transpose(0, 2, 1).reshape(gp, H), (1, G)), 0.0)

    nh, hw = S5_HALVES, S5_WIDTH // S5_HALVES
    bmat, tmat, cmat = pl.pallas_call(
        _s5_prepare_kernel,
        grid=(L,),
        in_specs=[_const_spec((w, gp)), _const_spec((w, gp)), _const_spec((gp, w)), _const_spec((gp, w)),
                  _const_spec((2 * (L + 1), gp)),
                  pl.BlockSpec((1, 1, gp), lambda i: (L - 1 - i, 0, 0)),
                  pl.BlockSpec((1, 1, gp), lambda i: (2 * L - i, 0, 0)),
                  pl.BlockSpec((1, gp, 1), lambda i: (i + 1, 0, 0)),
                  pl.BlockSpec((1, gp, 1), lambda i: (L + 2 + i, 0, 0))],
        out_specs=[pl.BlockSpec((nh, hw, 2 * gp // nh), lambda i: (0, i, 0)),
                   pl.BlockSpec((nh, hw, L * hw), lambda i: (0, i, 0)),
                   pl.BlockSpec((nh, 2 * gp // nh, hw), lambda i: (0, 0, i))],
        out_shape=[jax.ShapeDtypeStruct((nh, L * hw, 2 * gp // nh), BF16),
                   jax.ShapeDtypeStruct((nh, L * hw, L * hw), BF16),
                   jax.ShapeDtypeStruct((nh, 2 * gp // nh, L * hw), BF16)],
        scratch_shapes=[pltpu.VMEM((L, w, w), F32)],
        compiler_params=_cparams(1),
        name="s5_prepare",
    )(bd_br, bd_bi, bd_cr, bd_ci, prow, prow3, prow3, pcol3, pcol3)
    lam = jnp.stack([pr[L].reshape(gp), pi[L].reshape(gp)])
    return bmat, tmat, cmat, lam


def _s5_prepare_kernel(br_ref, bi_ref, cr_ref, ci_ref, prow_ref, rr_ref, ri_ref, cr_pow_ref, ci_pow_ref,
                       bmat_ref, tmat_ref, cmat_ref, k_scr):
    L = S5_L
    hw = S5_WIDTH // S5_HALVES
    hgp = S5_GROUPS * S5_STATE // S5_HALVES
    i = pl.program_id(0)
    br, bi = br_ref[...], bi_ref[...]
    cr, ci = cr_ref[...], ci_ref[...]

    @pl.when(i == 0)
    def _():
        for j in range(L):
            pr = prow_ref[j:j + 1, :]
            pi = prow_ref[L + 1 + j:L + 2 + j, :]
            k_scr[j] = (jnp.dot(br * pr - bi * pi, cr, precision=lax.Precision.HIGHEST,
                                preferred_element_type=F32)
                        - jnp.dot(bi * pr + br * pi, ci, precision=lax.Precision.HIGHEST,
                                  preferred_element_type=F32))

    pr, pi = rr_ref[0], ri_ref[0]
    pcr, pci = cr_pow_ref[0], ci_pow_ref[0]
    pb_r, pb_i = br * pr - bi * pi, bi * pr + br * pi
    g_r, g_i = cr * pcr - ci * pci, -(cr * pci + ci * pcr)
    lags = [k_scr[jnp.maximum(t - i, 0)] for t in range(L)]
    for h in range(S5_HALVES):
        ch = slice(h * hw, (h + 1) * hw)
        st = slice(h * hgp, (h + 1) * hgp)
        bmat_ref[h, :, :hgp] = pb_r[ch, st].astype(BF16)
        bmat_ref[h, :, hgp:] = pb_i[ch, st].astype(BF16)
        for t in range(L):
            tmat_ref[h, :, t * hw:(t + 1) * hw] = jnp.where(t >= i, lags[t][ch, ch], 0.0).astype(BF16)
        cmat_ref[h, :hgp, :] = g_r[st, ch].astype(BF16)
        cmat_ref[h, hgp:, :] = g_i[st, ch].astype(BF16)


def _gelu_tanh(y):
    return 0.5 * y * (1.0 + jnp.tanh(math.sqrt(2.0 / math.pi) * (y + 0.044715 * (y * y * y))))


def _s5_kernel(u_ref, gs_ref, bmat_ref, tmat_ref, cmat_ref, lam_ref, dskip_ref, wglu_ref, bglu_ref,
               o_ref, x_scr, h_scr):
    nh = S5_HALVES
    hw = S5_WIDTH // nh
    gp = S5_GROUPS * S5_STATE
    hgp = gp // nh
    hl = S5_L * hw
    rows = u_ref.shape[0]
    per_batch = rows // S5_NB
    for h in range(nh):
        xh = jnp.dot(u_ref[:, h * hl:(h + 1) * hl], bmat_ref[h], preferred_element_type=F32)
        x_scr[:, h * hgp:(h + 1) * hgp] = xh[:, :hgp]
        x_scr[:, gp + h * hgp:gp + (h + 1) * hgp] = xh[:, hgp:]
    lam_r = lam_ref[0:1, :]
    lam_i = lam_ref[1:2, :]

    def block(i, hs):
        out = []
        for b in range(S5_NB):
            hr, hi = hs[b]
            r0 = pl.multiple_of(b * per_batch + i * 8, 8)
            x8 = x_scr[pl.ds(r0, 8), :]
            starts_r, starts_i = [], []
            for jj in range(8):
                starts_r.append(hr)
                starts_i.append(hi)
                xr = x8[jj:jj + 1, :gp]
                xi = x8[jj:jj + 1, gp:]
                hr, hi = lam_r * hr - lam_i * hi + xr, lam_r * hi + lam_i * hr + xi
            h_scr[pl.ds(r0, 8), :gp] = jnp.concatenate(starts_r, axis=0)
            h_scr[pl.ds(r0, 8), gp:] = jnp.concatenate(starts_i, axis=0)
            out.append((hr, hi))
        return tuple(out)

    zero = jnp.zeros((1, gp), F32)
    lax.fori_loop(0, per_batch // 8, block, tuple((zero, zero) for _ in range(S5_NB)))

    hs = h_scr[...].astype(BF16)
    hs = [jnp.concatenate([hs[:, h * hgp:(h + 1) * hgp], hs[:, gp + h * hgp:gp + (h + 1) * hgp]], axis=1)
          for h in range(nh)]

    pair = 2 * hw
    ypairs = [[jnp.dot(u_ref[:, h * hl:h * hl + (tp + 1) * pair],
                       tmat_ref[h, :(tp + 1) * pair, tp * pair:(tp + 1) * pair], preferred_element_type=F32)
               + jnp.dot(hs[h], cmat_ref[h, :, tp * pair:(tp + 1) * pair], preferred_element_type=F32)
               for h in range(nh)] for tp in range(S5_L // 2)]

    def token(ref, t):
        return jnp.concatenate([ref[:, h * hl + t * hw:h * hl + (t + 1) * hw] for h in range(nh)], axis=1)

    ys = []
    for t in range(S5_L):
        e = (t % 2) * hw
        y = jnp.concatenate([yh[:, e:e + hw] for yh in ypairs[t // 2]], axis=1)
        ys.append(_gelu_tanh(y + dskip_ref[...] * token(u_ref, t).astype(F32)))
    zs = [jnp.dot(y.astype(BF16), wglu_ref[...], preferred_element_type=F32) for y in ys]
    for t, (y, z) in enumerate(zip(ys, zs)):
        o = (y * _sigmoid(z + bglu_ref[...])) * token(gs_ref, t).astype(F32)
        for h in range(nh):
            o_ref[h, pl.ds(t, rows, stride=S5_L), :] = o[:, h * hw:(h + 1) * hw]


def _s5(u, gs, batch, seq, a_re, a_im, log_dt, b_re, b_im, c_re, c_im, d_skip, w_glu, b_glu):
    t = batch * seq
    lw = S5_L * S5_WIDTH
    gp2 = 2 * S5_GROUPS * S5_STATE
    bmat, tmat, cmat, lam = _s5_matrices(a_re, a_im, log_dt, b_re, b_im, c_re, c_im)
    rows = S5_NB * seq // S5_L
    halves = S5_HALVES
    blk = pl.BlockSpec((rows, lw), lambda i: (i, 0))
    return pl.pallas_call(
        _s5_kernel,
        grid=(batch // S5_NB,),
        in_specs=[blk, blk, _const_spec(bmat.shape, True), _const_spec(tmat.shape, True),
                  _const_spec(cmat.shape, True), _const_spec((2, gp2 // 2)),
                  _const_spec((1, S5_WIDTH)), _const_spec((S5_WIDTH, S5_WIDTH)), _const_spec((1, S5_WIDTH))],
        out_specs=pl.BlockSpec((halves, rows * S5_L, 128), lambda i: (0, i, 0)),
        out_shape=jax.ShapeDtypeStruct((halves, t, 128), F32),
        scratch_shapes=[pltpu.VMEM((rows, gp2), F32), pltpu.VMEM((rows, gp2), F32)],
        compiler_params=_cparams(1),
        name="s5_mixer",
    )(u, gs, bmat, tmat, cmat, lam, d_skip.reshape(1, -1).astype(F32), w_glu.astype(BF16),
      b_glu.reshape(1, -1).astype(F32))


def _out_ln_kernel(n_act, x_ref, *refs):
    acts = refs[:n_act]
    ws = refs[n_act:2 * n_act]
    g_ref, b_ref, o_ref = refs[2 * n_act:]
    def operand(ref):
        if len(ref.shape) == 2:
            return ref[...]
        return jnp.concatenate([ref[i] for i in range(ref.shape[0])], axis=1).astype(BF16)

    y = jnp.dot(operand(acts[0]), ws[0][...], preferred_element_type=F32)
    for a, w in zip(acts[1:], ws[1:]):
        y = y + jnp.dot(operand(a), w[...], preferred_element_type=F32)
    z = DEEPNORM_ALPHA * x_ref[...] + y
    mu = jnp.mean(z, axis=-1, keepdims=True)
    zc = z - mu
    var = jnp.mean(zc * zc, axis=-1, keepdims=True)
    o_ref[...] = zc * lax.rsqrt(var + LN_EPS) * g_ref[...] + b_ref[...]


def _out_ln(xf, acts, ws, ln_g, ln_b):
    t = xf.shape[0]
    tm = TOKEN_TILE
    n = len(acts)

    def row(c):
        return pl.BlockSpec((tm, c), lambda i: (i, 0))

    def act_spec(a):
        if a.ndim == 2:
            return row(a.shape[1])
        return pl.BlockSpec((a.shape[0], tm, a.shape[2]), lambda i: (0, i, 0))

    return pl.pallas_call(
        functools.partial(_out_ln_kernel, n),
        grid=(t // tm,),
        in_specs=[row(D_MODEL)] + [act_spec(a) for a in acts] + [_const_spec(w.shape) for w in ws]
                 + [_const_spec((1, D_MODEL)), _const_spec((1, D_MODEL))],
        out_specs=row(D_MODEL),
        out_shape=jax.ShapeDtypeStruct((t, D_MODEL), F32),
        compiler_params=_cparams(1),
        name="out_proj_layernorm",
    )(xf, *acts, *ws, ln_g.reshape(1, -1), ln_b.reshape(1, -1))


OD_Q = 0
OD_K = OD_Q + GLA_KEY_WIDTH
OD_V = OD_K + GLA_KEY_WIDTH
OD_GL = OD_V + GLA_WIDTH
OD_G = OD_GL + 128
OD_COLS = OD_G + GLA_WIDTH


def _odd_in_kernel(x_ref, w_ref, wgk_ref, bgk_ref, q_ref, k_ref, v_ref, la_ref, g_ref):
    xb = x_ref[...].astype(BF16)

    def proj(lo, hi):
        return jnp.dot(xb, w_ref[:, lo:hi], preferred_element_type=F32)

    z = jnp.dot(proj(OD_GL, OD_G).astype(BF16), wgk_ref[...], preferred_element_type=F32) + bgk_ref[...]
    g = proj(OD_G, OD_COLS)
    log_sig = -(jnp.maximum(-z, 0.0) + jnp.log1p(jnp.exp(-jnp.abs(z))))
    la_ref[...] = log_sig / GLA_GATE_NORM
    v_ref[...] = proj(OD_V, OD_GL).astype(BF16)
    g_ref[...] = _silu(g).astype(BF16)
    q_ref[...] = proj(OD_Q, OD_K).astype(BF16)
    k_ref[...] = proj(OD_K, OD_V).astype(BF16)


def _odd_in(xf, w_in, w_gk2, b_gk):
    t = xf.shape[0]
    tm = TOKEN_TILE
    q, k, v, gl, g = jnp.split(w_in, [OD_K, OD_V, OD_GL, OD_GL + GLA_GATE_RANK], axis=1)
    gl = jnp.pad(gl, ((0, 0), (0, 128 - GLA_GATE_RANK)))
    w_p = jnp.concatenate([q, k, v, gl, g], axis=1).astype(BF16)
    wgk_p = jnp.pad(w_gk2, ((0, 128 - GLA_GATE_RANK), (0, 0))).astype(BF16)

    def row(c):
        return pl.BlockSpec((tm, c), lambda i: (i, 0))

    outs = [jax.ShapeDtypeStruct((t, GLA_KEY_WIDTH), BF16), jax.ShapeDtypeStruct((t, GLA_KEY_WIDTH), BF16),
            jax.ShapeDtypeStruct((t, GLA_WIDTH), BF16), jax.ShapeDtypeStruct((t, GLA_KEY_WIDTH), F32),
            jax.ShapeDtypeStruct((t, GLA_WIDTH), BF16)]
    return pl.pallas_call(
        _odd_in_kernel,
        grid=(t // tm,),
        in_specs=[row(D_MODEL), _const_spec((D_MODEL, OD_COLS)), _const_spec((128, GLA_KEY_WIDTH)),
                  _const_spec((1, GLA_KEY_WIDTH))],
        out_specs=[row(GLA_KEY_WIDTH), row(GLA_KEY_WIDTH), row(GLA_WIDTH), row(GLA_KEY_WIDTH), row(GLA_WIDTH)],
        out_shape=outs,
        compiler_params=_cparams(1),
        name="odd_in_proj",
    )(xf, w_p, wgk_p, b_gk.reshape(1, -1).astype(F32))


def _gla_kernel(q_ref, k_ref, v_ref, la_ref, g_ref, gn_ref, o_ref, bc_scr):
    c, sb = GLA_CHUNK, GLA_SUPER
    cps = sb // c
    seq = q_ref.shape[1]
    nch = seq // c
    nt = (((1,), (1,)), ((), ()))
    tn = (((0,), (0,)), ((), ()))
    ri = lax.broadcasted_iota(jnp.int32, (sb, sb), 0)
    ci = lax.broadcasted_iota(jnp.int32, (sb, sb), 1)
    valid = jnp.logical_and(ci <= ri, ci >= (ri // c) * c)
    tri = jnp.where(valid, 1.0, 0.0).astype(BF16)
    scale = GLA_DK ** -0.5

    blocks = [slice(s * sb, (s + 1) * sb) for s in range(seq // sb)]
    chunks = [slice(n * c, (n + 1) * c) for n in range(nch)]

    bcs = []
    for rows in blocks:
        ga = la_ref[0, rows, :]
        ga_hi = ga.astype(BF16)
        ga_lo = (ga - ga_hi.astype(F32)).astype(BF16)
        cs = jnp.dot(tri, jnp.concatenate([ga_hi, ga_lo], axis=1), preferred_element_type=F32)
        bc = cs[:, :GLA_DK] + cs[:, GLA_DK:]
        bc_scr[rows, :] = bc
        bcs.append(bc.reshape(cps, c, GLA_DK))

    qds, k_invs, k_ends = [], [], []
    for rows, bc3 in zip(blocks, bcs):
        bl3 = bc3[:, c - 1:c, :]
        q3 = q_ref[0, rows, :].astype(F32).reshape(cps, c, GLA_DK) * scale
        k3 = k_ref[0, rows, :].astype(F32).reshape(cps, c, GLA_DK)
        qds.append((q3 * jnp.exp(bc3)).reshape(sb, GLA_DK).astype(BF16))
        k_invs.append((k3 * jnp.exp(-bc3)).reshape(sb, GLA_DK).astype(BF16))
        k_ends.append((k3 * jnp.exp(bl3 - bc3)).reshape(sb, GLA_DK).astype(BF16))

    atts = [lax.dot_general(qd, k_inv, nt, preferred_element_type=F32) for qd, k_inv in zip(qds, k_invs)]
    atts = [jnp.where(valid, att, 0.0).astype(BF16) for att in atts]
    o_intras = [jnp.dot(att, v_ref[0, rows, :], preferred_element_type=F32)
                for att, rows in zip(atts, blocks)]

    def chunk_of(vals, n):
        s, j = divmod(n, cps)
        return vals[s][j * c:(j + 1) * c]

    kvs = [lax.dot_general(chunk_of(k_ends, n), v_ref[0, chunks[n], :], tn, preferred_element_type=F32)
           for n in range(nch - 1)]

    b_last = bc_scr[pl.ds(c - 1, nch, stride=c), :]
    dec = jnp.concatenate([jnp.exp(b_last), jnp.zeros((GLA_DK - nch, GLA_DK), F32)], axis=0)
    dec_t = dec.T

    st = jnp.zeros((GLA_DK, GLA_DV), F32)
    states = [st.astype(BF16)]
    for n in range(nch - 1):
        st = st * dec_t[:, n:n + 1] + kvs[n]
        states.append(st.astype(BF16))

    o_inters = [jnp.dot(chunk_of(qds, n), states[n], preferred_element_type=F32) for n in range(nch)]
    for n in range(nch):
        o = chunk_of(o_intras, n) + o_inters[n]
        o = _rms(o, gn_ref[...]) * g_ref[0, chunks[n], :].astype(F32)
        o_ref[0, chunks[n], :] = o.astype(BF16)


def _gla(q, k, v, la, g, g_norm, batch, seq):
    q = q.reshape(batch, seq, -1)
    k = k.reshape(batch, seq, -1)
    v = v.reshape(batch, seq, -1)
    la = la.reshape(batch, seq, -1)
    g = g.reshape(batch, seq, -1)

    def spec(c):
        return pl.BlockSpec((1, seq, c), lambda b, h: (b, 0, h))

    o = pl.pallas_call(
        _gla_kernel,
        grid=(batch, GLA_HEADS),
        in_specs=[spec(GLA_DK), spec(GLA_DK), spec(GLA_DV), spec(GLA_DK), spec(GLA_DV),
                  _const_spec((1, GLA_DV))],
        out_specs=spec(GLA_DV),
        out_shape=jax.ShapeDtypeStruct((batch, seq, GLA_WIDTH), BF16),
        scratch_shapes=[pltpu.VMEM((seq, GLA_DK), F32)],
        compiler_params=_cparams(2),
        name="gla_mixer",
    )(q, k, v, la, g, g_norm.reshape(1, -1).astype(F32))
    return o.reshape(batch * seq, GLA_WIDTH)


def kernel(x, positions, ln_g, ln_b, even_w_in, mla_q_norm, mla_kv_norm, mla_w_uq, mla_w_ukv, s5_a_re, s5_a_im, s5_log_dt, s5_b_re, s5_b_im, s5_c_re, s5_c_im, s5_d, s5_w_glu, s5_b_glu, even_w_out, odd_w_in, gla_w_gk2, gla_b_gk, gla_g_norm, odd_w_out):
    batch, seq, _ = x.shape
    xf = x.reshape(batch * seq, D_MODEL)
    cos_full, sin_full = _rope_tables(positions)
    for layer in range(DEPTH):
        j = layer // 2
        if layer % 2 == 0:
            q, k, v, gm, u, gs = _even_in(xf, cos_full, sin_full, even_w_in[j], mla_q_norm[j],
                                          mla_kv_norm[j], mla_w_uq[j], mla_w_ukv[j])
            o_mla = _attention(q, k, v, gm, batch, seq)
            o_s5 = _s5(u, gs, batch, seq, s5_a_re[j], s5_a_im[j], s5_log_dt[j], s5_b_re[j], s5_b_im[j],
                       s5_c_re[j], s5_c_im[j], s5_d[j], s5_w_glu[j], s5_b_glu[j])
            w_out = even_w_out[j].astype(BF16)
            xf = _out_ln(xf, [o_mla, o_s5], [w_out[:MLA_WIDTH], w_out[MLA_WIDTH:]], ln_g[layer], ln_b[layer])
        else:
            q, k, v, la, g = _odd_in(xf, odd_w_in[j], gla_w_gk2[j], gla_b_gk[j])
            o = _gla(q, k, v, la, g, gla_g_norm[j], batch, seq)
            xf = _out_ln(xf, [o], [odd_w_out[j].astype(BF16)], ln_g[layer], ln_b[layer])
    return xf.reshape(batch, seq, D_MODEL)
```

```python
import functools
import math

import jax
import jax.numpy as jnp
from jax import lax
from jax.experimental import pallas as pl
from jax.experimental.pallas import tpu as pltpu

F32 = jnp.float32
BF16 = jnp.bfloat16

D_MODEL = 1024
DEPTH = 2

MLA_HEADS = 8
MLA_NOPE = 64
MLA_ROPE = 32
MLA_V = 64
MLA_Q_RANK = 256
MLA_KV_RANK = 128
MLA_WIDTH = MLA_HEADS * MLA_V
ROPE_BASE = 10000.0
HEAD_SLOT = 128

S5_WIDTH = 256
S5_GROUP = 16
S5_GROUPS = S5_WIDTH // S5_GROUP
S5_STATE = 64
S5_L = 8
S5_NB = 2
S5_HALVES = 2

GLA_HEADS = 4
GLA_KEY_WIDTH = D_MODEL // 2
GLA_WIDTH = D_MODEL
GLA_DK = GLA_KEY_WIDTH // GLA_HEADS
GLA_DV = GLA_WIDTH // GLA_HEADS
GLA_GATE_RANK = 16
GLA_GATE_NORM = 16.0
GLA_CHUNK = 64
GLA_SUPER = 256

DEEPNORM_ALPHA = (2 * DEPTH) ** 0.25
LN_EPS = 1e-5
RMS_EPS = 1e-6

TOKEN_TILE = 1024
ATTN_T = 256
ATTN_GROUP = 4
ATTN_STEP_HEADS = 2
NEG_BIG = -1e30
VMEM_LIMIT = 56 * 1024 * 1024


def _cparams(n_axes):
    return pltpu.CompilerParams(dimension_semantics=("arbitrary",) * n_axes,
                                vmem_limit_bytes=VMEM_LIMIT)


def _const_spec(shape, single=False):
    nd = len(shape)
    if single:
        return pl.BlockSpec(shape, lambda *_: (0,) * nd, pipeline_mode=pl.Buffered(1))
    return pl.BlockSpec(shape, lambda *_: (0,) * nd)


def _sigmoid(x):
    return 1.0 / (1.0 + jnp.exp(-x))


def _silu(x):
    return x * _sigmoid(x)


def _rms(x, g):
    return (x * lax.rsqrt(jnp.mean(x * x, axis=-1, keepdims=True) + RMS_EPS)) * g


ROPE_PACK = 128 // (MLA_ROPE // 2)


def _rope_kernel(pos_ref, invf_ref, cos_ref, sin_ref):
    half = MLA_ROPE // 2
    rb = pos_ref.shape[0]
    pos = pos_ref[...].astype(F32)
    lane = lax.broadcasted_iota(jnp.int32, (rb, 128), 1)
    owner = lane // half
    posx = jnp.zeros((rb, 128), F32)
    for s in range(ROPE_PACK):
        posx = jnp.where(owner == s, pos[:, s:s + 1], posx)
    ang = posx * invf_ref[...]
    c = jnp.cos(ang)
    sn = jnp.sin(ang)
    first = (lane & 112) == MLA_NOPE
    second = (lane & 112) == MLA_NOPE + half
    for s in range(ROPE_PACK):
        k1 = (MLA_NOPE - half * s) % 128
        k2 = (MLA_NOPE + half - half * s) % 128
        cf = jnp.where(first, pltpu.roll(c, k1, 1), jnp.where(second, pltpu.roll(c, k2, 1), 1.0))
        sf = jnp.where(first, -pltpu.roll(sn, k1, 1), jnp.where(second, pltpu.roll(sn, k2, 1), 0.0))
        cos_ref[pl.ds(s, rb, stride=ROPE_PACK), :] = cf
        sin_ref[pl.ds(s, rb, stride=ROPE_PACK), :] = sf


def _rope_tables(positions):
    t = positions.size
    half = MLA_ROPE // 2
    pos = positions.reshape(t // ROPE_PACK, ROPE_PACK)
    inv_freq = ROPE_BASE ** (-jnp.arange(half, dtype=F32) / half)
    invf = jnp.tile(inv_freq, ROPE_PACK).reshape(1, 128)
    rb = 512
    out = jax.ShapeDtypeStruct((t, HEAD_SLOT), F32)
    return pl.pallas_call(
        _rope_kernel,
        grid=(t // ROPE_PACK // rb,),
        in_specs=[pl.BlockSpec((rb, ROPE_PACK), lambda i: (i, 0)), _const_spec((1, 128))],
        out_specs=[pl.BlockSpec((rb * ROPE_PACK, HEAD_SLOT), lambda i: (i, 0))] * 2,
        out_shape=[out, out],
        compiler_params=_cparams(1),
        name="rope_tables",
    )(pos, invf)


EV_CQ = 0
EV_CKV = EV_CQ + MLA_Q_RANK
EV_KPE = EV_CKV + MLA_KV_RANK
EV_GM = EV_KPE + HEAD_SLOT
EV_U = EV_GM + MLA_WIDTH
EV_GS = EV_U + S5_WIDTH
EV_COLS = EV_GS + S5_WIDTH
Q_SCALE = (MLA_NOPE + MLA_ROPE) ** -0.5 * math.log2(math.e)


def _rotate_pairs(x):
    n = x.shape[-1]
    lane = lax.broadcasted_iota(jnp.int32, x.shape, 1)
    lo = (lane & 112) == 64
    return jnp.where(lo, pltpu.roll(x, n - MLA_ROPE // 2, 1), pltpu.roll(x, MLA_ROPE // 2, 1))


def _even_in_kernel(x_ref, cos_ref, sin_ref, w_in_ref, qn_ref, kvn_ref, wuq_ref, wk_ref, wv_ref,
                    q_ref, k_ref, v_ref, gm_ref, u_ref, gs_ref, fold_scr):
    xb = x_ref[...].astype(BF16)
    h = jnp.dot(xb, w_in_ref[...], preferred_element_type=F32)
    cos = cos_ref[...]
    sin = sin_ref[...]

    cqn = _rms(h[:, EV_CQ:EV_CKV], qn_ref[...]).astype(BF16)
    q = jnp.dot(cqn, wuq_ref[...], preferred_element_type=F32)
    q = q * jnp.tile(cos, (1, MLA_HEADS)) + _rotate_pairs(q) * jnp.tile(sin, (1, MLA_HEADS))
    q_ref[...] = (q * Q_SCALE).astype(BF16)

    kpe = h[:, EV_KPE:EV_GM]
    kpe = kpe * cos + _rotate_pairs(kpe) * sin
    ckvn = _rms(h[:, EV_CKV:EV_KPE], kvn_ref[...]).astype(BF16)
    k = jnp.dot(ckvn, wk_ref[...], preferred_element_type=F32) + jnp.tile(kpe, (1, MLA_HEADS))
    k_ref[...] = k.astype(BF16)
    v_ref[...] = jnp.dot(ckvn, wv_ref[...], preferred_element_type=F32).astype(BF16)

    gm_ref[...] = _silu(h[:, EV_GM:EV_U]).astype(BF16)
    rows = x_ref.shape[0] // S5_L
    hw = S5_WIDTH // S5_HALVES
    for ref, val in ((u_ref, h[:, EV_U:EV_GS]), (gs_ref, _silu(h[:, EV_GS:EV_COLS]))):
        for half in range(S5_HALVES):
            fold_scr[half] = val[:, half * hw:(half + 1) * hw]
        for half in range(S5_HALVES):
            for s in range(S5_L):
                lo = (half * S5_L + s) * hw
                ref[:, lo:lo + hw] = fold_scr[half, pl.ds(s, rows, stride=S5_L), :].astype(BF16)


def _even_in(xf, cos_full, sin_full, w_in, q_norm, kv_norm, w_uq, w_ukv):
    t = xf.shape[0]
    tm = TOKEN_TILE
    pad = HEAD_SLOT - MLA_NOPE - MLA_ROPE
    cq, ckv, kr, gm, u, gs = jnp.split(
        w_in, [MLA_Q_RANK, MLA_Q_RANK + MLA_KV_RANK, MLA_Q_RANK + MLA_KV_RANK + MLA_ROPE,
               MLA_Q_RANK + MLA_KV_RANK + MLA_ROPE + MLA_WIDTH,
               MLA_Q_RANK + MLA_KV_RANK + MLA_ROPE + MLA_WIDTH + S5_WIDTH], axis=1)
    kr = jnp.pad(kr, ((0, 0), (MLA_NOPE, pad)))
    w_in_p = jnp.concatenate([cq, ckv, kr, gm, u, gs], axis=1).astype(BF16)
    wuq_p = jnp.pad(w_uq.reshape(MLA_Q_RANK, MLA_HEADS, MLA_NOPE + MLA_ROPE),
                    ((0, 0), (0, 0), (0, pad))).reshape(MLA_Q_RANK, MLA_HEADS * HEAD_SLOT).astype(BF16)
    wkv = w_ukv.reshape(MLA_KV_RANK, MLA_HEADS, MLA_NOPE + MLA_V)
    wk_p = jnp.pad(wkv[:, :, :MLA_NOPE], ((0, 0), (0, 0), (0, HEAD_SLOT - MLA_NOPE))
                   ).reshape(MLA_KV_RANK, MLA_HEADS * HEAD_SLOT).astype(BF16)
    wv_p = wkv[:, :, MLA_NOPE:].reshape(MLA_KV_RANK, MLA_WIDTH).astype(BF16)

    def row(n):
        return pl.BlockSpec((tm, n), lambda i: (i, 0))

    qk_w = MLA_HEADS * HEAD_SLOT
    lw = S5_L * S5_WIDTH
    folded = jax.ShapeDtypeStruct((t // S5_L, lw), BF16)
    folded_spec = pl.BlockSpec((tm // S5_L, lw), lambda i: (i, 0))
    outs = [jax.ShapeDtypeStruct((t, qk_w), BF16), jax.ShapeDtypeStruct((t, qk_w), BF16),
            jax.ShapeDtypeStruct((t, MLA_WIDTH), BF16), jax.ShapeDtypeStruct((t, MLA_WIDTH), BF16),
            folded, folded]
    return pl.pallas_call(
        _even_in_kernel,
        grid=(t // tm,),
        in_specs=[row(D_MODEL), row(HEAD_SLOT), row(HEAD_SLOT),
                  _const_spec((D_MODEL, EV_COLS)), _const_spec((1, MLA_Q_RANK)),
                  _const_spec((1, MLA_KV_RANK)), _const_spec((MLA_Q_RANK, qk_w)),
                  _const_spec((MLA_KV_RANK, qk_w)), _const_spec((MLA_KV_RANK, MLA_WIDTH))],
        out_specs=[row(qk_w), row(qk_w), row(MLA_WIDTH), row(MLA_WIDTH), folded_spec, folded_spec],
        out_shape=outs,
        scratch_shapes=[pltpu.VMEM((S5_WIDTH // 128, tm, 128), F32)],
        compiler_params=_cparams(1),
        name="even_in_proj",
    )(xf, cos_full, sin_full, w_in_p, q_norm.reshape(1, -1), kv_norm.reshape(1, -1), wuq_p, wk_p, wv_p)


def _attn_kernel(q_ref, k_ref, v_ref, g_ref, o_ref):
    seq = q_ref.shape[1]
    t = ATTN_T
    n = seq // t
    nt = (((1,), (1,)), ((), ()))
    step = pl.program_id(2)

    @pl.when(step == 0)
    def _():
        o_ref[...] = jnp.zeros_like(o_ref)

    keep = lax.broadcasted_iota(jnp.int32, (t, t), 1) <= lax.broadcasted_iota(jnp.int32, (t, t), 0)
    lane_head = lax.broadcasted_iota(jnp.int32, (t, ATTN_GROUP * MLA_V), 1) // MLA_V

    def scores(qi, c):
        rows = slice(qi * t, (qi + 1) * t)
        lanes = slice(c * HEAD_SLOT, (c + 1) * HEAD_SLOT)
        q = q_ref[0, rows, lanes]
        sd = lax.dot_general(q, k_ref[0, rows, lanes], nt, preferred_element_type=F32)
        sd = jnp.where(keep, sd, NEG_BIG)
        if qi == 0:
            return sd
        s = lax.dot_general(q, k_ref[0, :qi * t, lanes], nt, preferred_element_type=F32)
        return jnp.concatenate([s, sd], axis=1)

    def attend(qi, ss):
        rows = slice(qi * t, (qi + 1) * t)
        o = None
        for c, s in enumerate(ss):
            slabs = [s[:, j:j + 128] for j in range(0, s.shape[1], 128)]
            m = functools.reduce(jnp.maximum, slabs)
            m = jnp.max(m, axis=1, keepdims=True)
            lsum = jnp.zeros((t, 128), F32)
            ps = []
            for sl in slabs:
                pj = jnp.exp2(sl - m)
                lsum = lsum + pj
                ps.append(pj.astype(BF16))
            l = jnp.sum(lsum, axis=1, keepdims=True)
            p = jnp.concatenate(ps, axis=1)
            acc = jnp.dot(p, v_ref[0, :(qi + 1) * t, :], preferred_element_type=F32)
            oc = acc * (1.0 / l)
            o = oc if o is None else jnp.where(lane_head == step * ATTN_STEP_HEADS + c, oc, o)
        o = (o * g_ref[0, rows, :].astype(F32)).astype(BF16)
        o_ref[0, rows, :] = jnp.where(lane_head // ATTN_STEP_HEADS == step, o, o_ref[0, rows, :])

    pending = None
    for qi in reversed(range(n)):
        ss = [scores(qi, c) for c in range(ATTN_STEP_HEADS)]
        if pending is not None:
            attend(*pending)
        pending = (qi, ss)
    attend(*pending)


def _attention(q, k, v, g, batch, seq):
    q = q.reshape(batch, seq, -1)
    k = k.reshape(batch, seq, -1)
    v = v.reshape(batch, seq, -1)
    g = g.reshape(batch, seq, -1)
    gw = ATTN_GROUP * MLA_V
    steps = ATTN_GROUP // ATTN_STEP_HEADS
    head = pl.BlockSpec((1, seq, ATTN_STEP_HEADS * HEAD_SLOT), lambda b, gi, st: (b, 0, gi * steps + st))
    group = pl.BlockSpec((1, seq, gw), lambda b, gi, st: (b, 0, gi))
    o = pl.pallas_call(
        _attn_kernel,
        grid=(batch, MLA_HEADS // ATTN_GROUP, steps),
        in_specs=[head, head, group, group],
        out_specs=group,
        out_shape=jax.ShapeDtypeStruct((batch, seq, MLA_WIDTH), BF16),
        compiler_params=_cparams(3),
        name="mla_attention",
    )(q, k, v, g)
    return o.reshape(batch * seq, MLA_WIDTH)


def _s5_matrices(a_re, a_im, log_dt, b_re, b_im, c_re, c_im):
    L, G, P, H = S5_L, S5_GROUPS, S5_STATE, S5_GROUP
    dt = jnp.exp(log_dt.astype(F32))[:, None]
    ar, ai = a_re.astype(F32), a_im.astype(F32)
    mag = jnp.exp(ar * dt)
    lr, li = mag * jnp.cos(ai * dt), mag * jnp.sin(ai * dt)
    den = ar * ar + ai * ai
    nr, ni = lr - 1.0, li
    zr = (nr * ar + ni * ai) / den
    zi = (ni * ar - nr * ai) / den
    br, bi = b_re.astype(F32), b_im.astype(F32)
    bbr = zr[..., None] * br - zi[..., None] * bi
    bbi = zr[..., None] * bi + zi[..., None] * br
    cr, ci = c_re.astype(F32), c_im.astype(F32)

    def powers(j):
        j = j.astype(F32)[:, None, None]
        pmag = jnp.exp(j * (ar * dt))
        return pmag * jnp.cos(j * (ai * dt)), pmag * jnp.sin(j * (ai * dt))

    gp, w = G * P, G * H
    pr, pi = powers(jnp.arange(L + 1))
    prow = jnp.concatenate([pr.reshape(L + 1, gp), pi.reshape(L + 1, gp)], axis=0)
    prow3 = prow.reshape(2 * (L + 1), 1, gp)
    pcol3 = prow.reshape(2 * (L + 1), gp, 1)

    same_b = (jnp.arange(w)[:, None] // H) == (jnp.arange(gp)[None, :] // P)
    same_c = (jnp.arange(gp)[:, None] // P) == (jnp.arange(w)[None, :] // H)
    bd_br = jnp.where(same_b, jnp.tile(bbr.transpose(2, 0, 1).reshape(H, gp), (G, 1)), 0.0)
    bd_bi = jnp.where(same_b, jnp.tile(bbi.transpose(2, 0, 1).reshape(H, gp), (G, 1)), 0.0)
    bd_cr = jnp.where(same_c, jnp.tile(cr.transpose(0, 2, 1).reshape(gp, H), (1, G)), 0.0)
    bd_ci = jnp.where(same_c, jnp.tile(ci.transpose(0, 2, 1).reshape(gp, H), (1, G)), 0.0)

    nh, hw = S5_HALVES, S5_WIDTH // S5_HALVES
    bmat, tmat, cmat = pl.pallas_call(
        _s5_prepare_kernel,
        grid=(L,),
        in_specs=[_const_spec((w, gp)), _const_spec((w, gp)), _const_spec((gp, w)), _const_spec((gp, w)),
                  _const_spec((2 * (L + 1), gp)),
                  pl.BlockSpec((1, 1, gp), lambda i: (L - 1 - i, 0, 0)),
                  pl.BlockSpec((1, 1, gp), lambda i: (2 * L - i, 0, 0)),
                  pl.BlockSpec((1, gp, 1), lambda i: (i + 1, 0, 0)),
                  pl.BlockSpec((1, gp, 1), lambda i: (L + 2 + i, 0, 0))],
        out_specs=[pl.BlockSpec((nh, hw, 2 * gp // nh), lambda i: (0, i, 0)),
                   pl.BlockSpec((nh, hw, L * hw), lambda i: (0, i, 0)),
                   pl.BlockSpec((nh, 2 * gp // nh, hw), lambda i: (0, 0, i))],
        out_shape=[jax.ShapeDtypeStruct((nh, L * hw, 2 * gp // nh), BF16),
                   jax.ShapeDtypeStruct((nh, L * hw, L * hw), BF16),
                   jax.ShapeDtypeStruct((nh, 2 * gp // nh, L * hw), BF16)],
        scratch_shapes=[pltpu.VMEM((L, w, w), F32)],
        compiler_params=_cparams(1),
        name="s5_prepare",
    )(bd_br, bd_bi, bd_cr, bd_ci, prow, prow3, prow3, pcol3, pcol3)
    lam = jnp.stack([pr[L].reshape(gp), pi[L].reshape(gp)])
    return bmat, tmat, cmat, lam


def _s5_prepare_kernel(br_ref, bi_ref, cr_ref, ci_ref, prow_ref, rr_ref, ri_ref, cr_pow_ref, ci_pow_ref,
                       bmat_ref, tmat_ref, cmat_ref, k_scr):
    L = S5_L
    hw = S5_WIDTH // S5_HALVES
    hgp = S5_GROUPS * S5_STATE // S5_HALVES
    i = pl.program_id(0)
    br, bi = br_ref[...], bi_ref[...]
    cr, ci = cr_ref[...], ci_ref[...]

    @pl.when(i == 0)
    def _():
        for j in range(L):
            pr = prow_ref[j:j + 1, :]
            pi = prow_ref[L + 1 + j:L + 2 + j, :]
            k_scr[j] = (jnp.dot(br * pr - bi * pi, cr, precision=lax.Precision.HIGHEST,
                                preferred_element_type=F32)
                        - jnp.dot(bi * pr + br * pi, ci, precision=lax.Precision.HIGHEST,
                                  preferred_element_type=F32))

    pr, pi = rr_ref[0], ri_ref[0]
    pcr, pci = cr_pow_ref[0], ci_pow_ref[0]
    pb_r, pb_i = br * pr - bi * pi, bi * pr + br * pi
    g_r, g_i = cr * pcr - ci * pci, -(cr * pci + ci * pcr)
    lags = [k_scr[jnp.maximum(t - i, 0)] for t in range(L)]
    for h in range(S5_HALVES):
        ch = slice(h * hw, (h + 1) * hw)
        st = slice(h * hgp, (h + 1) * hgp)
        bmat_ref[h, :, :hgp] = pb_r[ch, st].astype(BF16)
        bmat_ref[h, :, hgp:] = pb_i[ch, st].astype(BF16)
        for t in range(L):
            tmat_ref[h, :, t * hw:(t + 1) * hw] = jnp.where(t >= i, lags[t][ch, ch], 0.0).astype(BF16)
        cmat_ref[h, :hgp, :] = g_r[st, ch].astype(BF16)
        cmat_ref[h, hgp:, :] = g_i[st, ch].astype(BF16)


def _gelu_tanh(y):
    return 0.5 * y * (1.0 + jnp.tanh(math.sqrt(2.0 / math.pi) * (y + 0.044715 * (y * y * y))))


def _s5_kernel(u_ref, gs_ref, bmat_ref, tmat_ref, cmat_ref, lam_ref, dskip_ref, wglu_ref, bglu_ref,
               o_ref, x_scr, h_scr):
    nh = S5_HALVES
    hw = S5_WIDTH // nh
    gp = S5_GROUPS * S5_STATE
    hgp = gp // nh
    hl = S5_L * hw
    rows = u_ref.shape[0]
    per_batch = rows // S5_NB
    for h in range(nh):
        xh = jnp.dot(u_ref[:, h * hl:(h + 1) * hl], bmat_ref[h], preferred_element_type=F32)
        x_scr[:, h * hgp:(h + 1) * hgp] = xh[:, :hgp]
        x_scr[:, gp + h * hgp:gp + (h + 1) * hgp] = xh[:, hgp:]
    lam_r = lam_ref[0:1, :]
    lam_i = lam_ref[1:2, :]

    def block(i, hs):
        out = []
        for b in range(S5_NB):
            hr, hi = hs[b]
            r0 = pl.multiple_of(b * per_batch + i * 8, 8)
            x8 = x_scr[pl.ds(r0, 8), :]
            starts_r, starts_i = [], []
            for jj in range(8):
                starts_r.append(hr)
                starts_i.append(hi)
                xr = x8[jj:jj + 1, :gp]
                xi = x8[jj:jj + 1, gp:]
                hr, hi = lam_r * hr - lam_i * hi + xr, lam_r * hi + lam_i * hr + xi
            h_scr[pl.ds(r0, 8), :gp] = jnp.concatenate(starts_r, axis=0)
            h_scr[pl.ds(r0, 8), gp:] = jnp.concatenate(starts_i, axis=0)
            out.append((hr, hi))
        return tuple(out)

    zero = jnp.zeros((1, gp), F32)
    lax.fori_loop(0, per_batch // 8, block, tuple((zero, zero) for _ in range(S5_NB)))

    hs = h_scr[...].astype(BF16)
    hs = [jnp.concatenate([hs[:, h * hgp:(h + 1) * hgp], hs[:, gp + h * hgp:gp + (h + 1) * hgp]], axis=1)
          for h in range(nh)]

    pair = 2 * hw
    ypairs = [[jnp.dot(u_ref[:, h * hl:h * hl + (tp + 1) * pair],
                       tmat_ref[h, :(tp + 1) * pair, tp * pair:(tp + 1) * pair], preferred_element_type=F32)
               + jnp.dot(hs[h], cmat_ref[h, :, tp * pair:(tp + 1) * pair], preferred_element_type=F32)
               for h in range(nh)] for tp in range(S5_L // 2)]

    def token(ref, t):
        return jnp.concatenate([ref[:, h * hl + t * hw:h * hl + (t + 1) * hw] for h in range(nh)], axis=1)

    ys = []
    for t in range(S5_L):
        e = (t % 2) * hw
        y = jnp.concatenate([yh[:, e:e + hw] for yh in ypairs[t // 2]], axis=1)
        ys.append(_gelu_tanh(y + dskip_ref[...] * token(u_ref, t).astype(F32)))
    zs = [jnp.dot(y.astype(BF16), wglu_ref[...], preferred_element_type=F32) for y in ys]
    for t, (y, z) in enumerate(zip(ys, zs)):
        o = (y * _sigmoid(z + bglu_ref[...])) * token(gs_ref, t).astype(F32)
        for h in range(nh):
            o_ref[h, pl.ds(t, rows, stride=S5_L), :] = o[:, h * hw:(h + 1) * hw]


def _s5(u, gs, batch, seq, a_re, a_im, log_dt, b_re, b_im, c_re, c_im, d_skip, w_glu, b_glu):
    t = batch * seq
    lw = S5_L * S5_WIDTH
    gp2 = 2 * S5_GROUPS * S5_STATE
    bmat, tmat, cmat, lam = _s5_matrices(a_re, a_im, log_dt, b_re, b_im, c_re, c_im)
    rows = S5_NB * seq // S5_L
    halves = S5_HALVES
    blk = pl.BlockSpec((rows, lw), lambda i: (i, 0))
    return pl.pallas_call(
        _s5_kernel,
        grid=(batch // S5_NB,),
        in_specs=[blk, blk, _const_spec(bmat.shape, True), _const_spec(tmat.shape, True),
                  _const_spec(cmat.shape, True), _const_spec((2, gp2 // 2)),
                  _const_spec((1, S5_WIDTH)), _const_spec((S5_WIDTH, S5_WIDTH)), _const_spec((1, S5_WIDTH))],
        out_specs=pl.BlockSpec((halves, rows * S5_L, 128), lambda i: (0, i, 0)),
        out_shape=jax.ShapeDtypeStruct((halves, t, 128), F32),
        scratch_shapes=[pltpu.VMEM((rows, gp2), F32), pltpu.VMEM((rows, gp2), F32)],
        compiler_params=_cparams(1),
        name="s5_mixer",
    )(u, gs, bmat, tmat, cmat, lam, d_skip.reshape(1, -1).astype(F32), w_glu.astype(BF16),
      b_glu.reshape(1, -1).astype(F32))


def _out_ln_kernel(n_act, then_odd_in, x_ref, *refs):
    acts = refs[:n_act]
    ws = refs[n_act:2 * n_act]
    g_ref, b_ref = refs[2 * n_act:2 * n_act + 2]
    rest = refs[2 * n_act + 2:]
    o_ref = rest[3] if then_odd_in else rest[0]

    def operand(ref):
        if len(ref.shape) == 2:
            return ref[...]
        return jnp.concatenate([ref[i] for i in range(ref.shape[0])], axis=1).astype(BF16)

    y = jnp.dot(operand(acts[0]), ws[0][...], preferred_element_type=F32)
    for a, w in zip(acts[1:], ws[1:]):
        y = y + jnp.dot(operand(a), w[...], preferred_element_type=F32)
    z = DEEPNORM_ALPHA * x_ref[...] + y
    mu = jnp.mean(z, axis=-1, keepdims=True)
    zc = z - mu
    var = jnp.mean(zc * zc, axis=-1, keepdims=True)
    xn = zc * lax.rsqrt(var + LN_EPS) * g_ref[...] + b_ref[...]
    o_ref[...] = xn
    if then_odd_in:
        _odd_in_body(xn.astype(BF16), *rest[:3], *rest[4:])


def _out_ln(xf, acts, ws, ln_g, ln_b, odd_in=None):
    t = xf.shape[0]
    tm = TOKEN_TILE
    n = len(acts)

    def row(c):
        return pl.BlockSpec((tm, c), lambda i: (i, 0))

    def act_spec(a):
        if a.ndim == 2:
            return row(a.shape[1])
        return pl.BlockSpec((a.shape[0], tm, a.shape[2]), lambda i: (0, i, 0))

    in_specs = ([row(D_MODEL)] + [act_spec(a) for a in acts] + [_const_spec(w.shape) for w in ws]
                + [_const_spec((1, D_MODEL)), _const_spec((1, D_MODEL))])
    args = [xf, *acts, *ws, ln_g.reshape(1, -1), ln_b.reshape(1, -1)]
    out_specs = [row(D_MODEL)]
    out_shape = [jax.ShapeDtypeStruct((t, D_MODEL), F32)]
    if odd_in is not None:
        odd_args, odd_specs, odd_out_specs, odd_out_shape = _odd_in_operands(t, tm, *odd_in)
        in_specs += odd_specs
        args += odd_args
        out_specs += odd_out_specs
        out_shape += odd_out_shape
    outs = pl.pallas_call(
        functools.partial(_out_ln_kernel, n, odd_in is not None),
        grid=(t // tm,),
        in_specs=in_specs,
        out_specs=out_specs,
        out_shape=out_shape,
        compiler_params=_cparams(1),
        name="out_proj_layernorm" if odd_in is None else "out_proj_layernorm_odd_in_proj",
    )(*args)
    return outs[0] if odd_in is None else outs


OD_Q = 0
OD_K = OD_Q + GLA_KEY_WIDTH
OD_V = OD_K + GLA_KEY_WIDTH
OD_GL = OD_V + GLA_WIDTH
OD_G = OD_GL + 128
OD_COLS = OD_G + GLA_WIDTH


def _odd_in_kernel(x_ref, *refs):
    _odd_in_body(x_ref[...].astype(BF16), *refs)


def _odd_in_body(xb, w_ref, wgk_ref, bgk_ref, q_ref, k_ref, v_ref, la_ref, g_ref):
    def proj(lo, hi):
        return jnp.dot(xb, w_ref[:, lo:hi], preferred_element_type=F32)

    z = jnp.dot(proj(OD_GL, OD_G).astype(BF16), wgk_ref[...], preferred_element_type=F32) + bgk_ref[...]
    g = proj(OD_G, OD_COLS)
    log_sig = -(jnp.maximum(-z, 0.0) + jnp.log1p(jnp.exp(-jnp.abs(z))))
    la_ref[...] = log_sig / GLA_GATE_NORM
    v_ref[...] = proj(OD_V, OD_GL).astype(BF16)
    g_ref[...] = _silu(g).astype(BF16)
    q_ref[...] = proj(OD_Q, OD_K).astype(BF16)
    k_ref[...] = proj(OD_K, OD_V).astype(BF16)


def _odd_in_operands(t, tm, w_in, w_gk2, b_gk):
    q, k, v, gl, g = jnp.split(w_in, [OD_K, OD_V, OD_GL, OD_GL + GLA_GATE_RANK], axis=1)
    gl = jnp.pad(gl, ((0, 0), (0, 128 - GLA_GATE_RANK)))
    w_p = jnp.concatenate([q, k, v, gl, g], axis=1).astype(BF16)
    wgk_p = jnp.pad(w_gk2, ((0, 128 - GLA_GATE_RANK), (0, 0))).astype(BF16)

    def row(c):
        return pl.BlockSpec((tm, c), lambda i: (i, 0))

    args = [w_p, wgk_p, b_gk.reshape(1, -1).astype(F32)]
    specs = [_const_spec((D_MODEL, OD_COLS)), _const_spec((128, GLA_KEY_WIDTH)), _const_spec((1, GLA_KEY_WIDTH))]
    out_specs = [row(GLA_KEY_WIDTH), row(GLA_KEY_WIDTH), row(GLA_WIDTH), row(GLA_KEY_WIDTH), row(GLA_WIDTH)]
    out_shape = [jax.ShapeDtypeStruct((t, GLA_KEY_WIDTH), BF16), jax.ShapeDtypeStruct((t, GLA_KEY_WIDTH), BF16),
                 jax.ShapeDtypeStruct((t, GLA_WIDTH), BF16), jax.ShapeDtypeStruct((t, GLA_KEY_WIDTH), F32),
                 jax.ShapeDtypeStruct((t, GLA_WIDTH), BF16)]
    return args, specs, out_specs, out_shape


def _odd_in(xf, w_in, w_gk2, b_gk):
    t = xf.shape[0]
    tm = TOKEN_TILE
    args, specs, out_specs, out_shape = _odd_in_operands(t, tm, w_in, w_gk2, b_gk)
    return pl.pallas_call(
        _odd_in_kernel,
        grid=(t // tm,),
        in_specs=[pl.BlockSpec((tm, D_MODEL), lambda i: (i, 0))] + specs,
        out_specs=out_specs,
        out_shape=out_shape,
        compiler_params=_cparams(1),
        name="odd_in_proj",
    )(xf, *args)


def _gla_kernel(q_ref, k_ref, v_ref, la_ref, g_ref, gn_ref, o_ref, bc_scr):
    c, sb = GLA_CHUNK, GLA_SUPER
    cps = sb // c
    seq = q_ref.shape[1]
    nch = seq // c
    nt = (((1,), (1,)), ((), ()))
    tn = (((0,), (0,)), ((), ()))
    ri = lax.broadcasted_iota(jnp.int32, (sb, sb), 0)
    ci = lax.broadcasted_iota(jnp.int32, (sb, sb), 1)
    valid = jnp.logical_and(ci <= ri, ci >= (ri // c) * c)
    tri = jnp.where(valid, 1.0, 0.0).astype(BF16)
    scale = GLA_DK ** -0.5

    blocks = [slice(s * sb, (s + 1) * sb) for s in range(seq // sb)]
    chunks = [slice(n * c, (n + 1) * c) for n in range(nch)]

    bcs = []
    for rows in blocks:
        ga = la_ref[0, rows, :]
        ga_hi = ga.astype(BF16)
        ga_lo = (ga - ga_hi.astype(F32)).astype(BF16)
        cs = jnp.dot(tri, jnp.concatenate([ga_hi, ga_lo], axis=1), preferred_element_type=F32)
        bc = cs[:, :GLA_DK] + cs[:, GLA_DK:]
        bc_scr[rows, :] = bc
        bcs.append(bc.reshape(cps, c, GLA_DK))

    qds, k_invs, k_ends = [], [], []
    for rows, bc3 in zip(blocks, bcs):
        bl3 = bc3[:, c - 1:c, :]
        q3 = q_ref[0, rows, :].astype(F32).reshape(cps, c, GLA_DK) * scale
        k3 = k_ref[0, rows, :].astype(F32).reshape(cps, c, GLA_DK)
        qds.append((q3 * jnp.exp(bc3)).reshape(sb, GLA_DK).astype(BF16))
        k_invs.append((k3 * jnp.exp(-bc3)).reshape(sb, GLA_DK).astype(BF16))
        k_ends.append((k3 * jnp.exp(bl3 - bc3)).reshape(sb, GLA_DK).astype(BF16))

    atts = [lax.dot_general(qd, k_inv, nt, preferred_element_type=F32) for qd, k_inv in zip(qds, k_invs)]
    atts = [jnp.where(valid, att, 0.0).astype(BF16) for att in atts]
    o_intras = [jnp.dot(att, v_ref[0, rows, :], preferred_element_type=F32)
                for att, rows in zip(atts, blocks)]

    def chunk_of(vals, n):
        s, j = divmod(n, cps)
        return vals[s][j * c:(j + 1) * c]

    kvs = [lax.dot_general(chunk_of(k_ends, n), v_ref[0, chunks[n], :], tn, preferred_element_type=F32)
           for n in range(nch - 1)]

    b_last = bc_scr[pl.ds(c - 1, nch, stride=c), :]
    dec = jnp.concatenate([jnp.exp(b_last), jnp.zeros((GLA_DK - nch, GLA_DK), F32)], axis=0)
    dec_t = dec.T

    st = jnp.zeros((GLA_DK, GLA_DV), F32)
    states = [st.astype(BF16)]
    for n in range(nch - 1):
        st = st * dec_t[:, n:n + 1] + kvs[n]
        states.append(st.astype(BF16))

    o_inters = [jnp.dot(chunk_of(qds, n), states[n], preferred_element_type=F32) for n in range(nch)]
    for n in range(nch):
        o = chunk_of(o_intras, n) + o_inters[n]
        o = _rms(o, gn_ref[...]) * g_ref[0, chunks[n], :].astype(F32)
        o_ref[0, chunks[n], :] = o.astype(BF16)


def _gla(q, k, v, la, g, g_norm, batch, seq):
    q = q.reshape(batch, seq, -1)
    k = k.reshape(batch, seq, -1)
    v = v.reshape(batch, seq, -1)
    la = la.reshape(batch, seq, -1)
    g = g.reshape(batch, seq, -1)

    def spec(c):
        return pl.BlockSpec((1, seq, c), lambda b, h: (b, 0, h))

    o = pl.pallas_call(
        _gla_kernel,
        grid=(batch, GLA_HEADS),
        in_specs=[spec(GLA_DK), spec(GLA_DK), spec(GLA_DV), spec(GLA_DK), spec(GLA_DV),
                  _const_spec((1, GLA_DV))],
        out_specs=spec(GLA_DV),
        out_shape=jax.ShapeDtypeStruct((batch, seq, GLA_WIDTH), BF16),
        scratch_shapes=[pltpu.VMEM((seq, GLA_DK), F32)],
        compiler_params=_cparams(2),
        name="gla_mixer",
    )(q, k, v, la, g, g_norm.reshape(1, -1).astype(F32))
    return o.reshape(batch * seq, GLA_WIDTH)


def kernel(x, positions, ln_g, ln_b, even_w_in, mla_q_norm, mla_kv_norm, mla_w_uq, mla_w_ukv, s5_a_re, s5_a_im, s5_log_dt, s5_b_re, s5_b_im, s5_c_re, s5_c_im, s5_d, s5_w_glu, s5_b_glu, even_w_out, odd_w_in, gla_w_gk2, gla_b_gk, gla_g_norm, odd_w_out):
    batch, seq, _ = x.shape
    xf = x.reshape(batch * seq, D_MODEL)
    cos_full, sin_full = _rope_tables(positions)
    odd_inputs = None
    for layer in range(DEPTH):
        j = layer // 2
        if layer % 2 == 0:
            q, k, v, gm, u, gs = _even_in(xf, cos_full, sin_full, even_w_in[j], mla_q_norm[j],
                                          mla_kv_norm[j], mla_w_uq[j], mla_w_ukv[j])
            o_mla = _attention(q, k, v, gm, batch, seq)
            o_s5 = _s5(u, gs, batch, seq, s5_a_re[j], s5_a_im[j], s5_log_dt[j], s5_b_re[j], s5_b_im[j],
                       s5_c_re[j], s5_c_im[j], s5_d[j], s5_w_glu[j], s5_b_glu[j])
            w_out = even_w_out[j].astype(BF16)
            nxt = (odd_w_in[j], gla_w_gk2[j], gla_b_gk[j]) if layer + 1 < DEPTH else None
            res = _out_ln(xf, [o_mla, o_s5], [w_out[:MLA_WIDTH], w_out[MLA_WIDTH:]], ln_g[layer], ln_b[layer],
                          odd_in=nxt)
            xf, odd_inputs = (res, None) if nxt is None else (res[0], res[1:])
        else:
            if odd_inputs is None:
                odd_inputs = _odd_in(xf, odd_w_in[j], gla_w_gk2[j], gla_b_gk[j])
            q, k, v, la, g = odd_inputs
            odd_inputs = None
            o = _gla(q, k, v, la, g, gla_g_norm[j], batch, seq)
            xf = _out_ln(xf, [o], [odd_w_out[j].astype(BF16)], ln_g[layer], ln_b[layer])
    return xf.reshape(batch, seq, D_MODEL)
```

```python
import functools
import itertools
import math

import jax
import jax.numpy as jnp
from jax import lax
from jax.experimental import pallas as pl
from jax.experimental.pallas import tpu as pltpu

F32 = jnp.float32
BF16 = jnp.bfloat16

D_MODEL = 1024
DEPTH = 2

MLA_HEADS = 8
MLA_NOPE = 64
MLA_ROPE = 32
MLA_V = 64
MLA_Q_RANK = 256
MLA_KV_RANK = 128
MLA_WIDTH = MLA_HEADS * MLA_V
ROPE_BASE = 10000.0
HEAD_SLOT = 128

S5_WIDTH = 256
S5_GROUP = 16
S5_GROUPS = S5_WIDTH // S5_GROUP
S5_STATE = 64
S5_L = 8
S5_NB = 2
S5_HALVES = 2

GLA_HEADS = 4
GLA_KEY_WIDTH = D_MODEL // 2
GLA_WIDTH = D_MODEL
GLA_DK = GLA_KEY_WIDTH // GLA_HEADS
GLA_DV = GLA_WIDTH // GLA_HEADS
GLA_GATE_RANK = 16
GLA_GATE_NORM = 16.0
GLA_CHUNK = 64
GLA_SUPER = 256
GLA_STEP_HEADS = 2
GLA_STAGE_SKEW = 3

DEEPNORM_ALPHA = (2 * DEPTH) ** 0.25
LN_EPS = 1e-5
RMS_EPS = 1e-6

TOKEN_TILE = 1024
ROW_SPLIT = 4
ATTN_T = 256
ATTN_GROUP = 4
ATTN_STEP_HEADS = 2
NEG_BIG = -1e30
VMEM_LIMIT = 56 * 1024 * 1024


def _cparams(n_axes):
    return pltpu.CompilerParams(dimension_semantics=("arbitrary",) * n_axes,
                                vmem_limit_bytes=VMEM_LIMIT)


def _const_spec(shape, single=False):
    nd = len(shape)
    if single:
        return pl.BlockSpec(shape, lambda *_: (0,) * nd, pipeline_mode=pl.Buffered(1))
    return pl.BlockSpec(shape, lambda *_: (0,) * nd)


def _sigmoid(x):
    return 1.0 / (1.0 + jnp.exp(-x))


def _silu(x):
    return x * _sigmoid(x)


def _rms(x, g):
    return (x * lax.rsqrt(jnp.mean(x * x, axis=-1, keepdims=True) + RMS_EPS)) * g


ROPE_PACK = 128 // (MLA_ROPE // 2)


def _rope_kernel(pos_ref, invf_ref, cos_ref, sin_ref):
    half = MLA_ROPE // 2
    rb = pos_ref.shape[0]
    pos = pos_ref[...].astype(F32)
    lane = lax.broadcasted_iota(jnp.int32, (rb, 128), 1)
    owner = lane // half
    posx = jnp.zeros((rb, 128), F32)
    for s in range(ROPE_PACK):
        posx = jnp.where(owner == s, pos[:, s:s + 1], posx)
    ang = posx * invf_ref[...]
    c = jnp.cos(ang)
    sn = jnp.sin(ang)
    first = (lane & 112) == MLA_NOPE
    second = (lane & 112) == MLA_NOPE + half
    for s in range(ROPE_PACK):
        k1 = (MLA_NOPE - half * s) % 128
        k2 = (MLA_NOPE + half - half * s) % 128
        cf = jnp.where(first, pltpu.roll(c, k1, 1), jnp.where(second, pltpu.roll(c, k2, 1), 1.0))
        sf = jnp.where(first, -pltpu.roll(sn, k1, 1), jnp.where(second, pltpu.roll(sn, k2, 1), 0.0))
        cos_ref[pl.ds(s, rb, stride=ROPE_PACK), :] = cf
        sin_ref[pl.ds(s, rb, stride=ROPE_PACK), :] = sf


def _rope_tables(positions):
    t = positions.size
    half = MLA_ROPE // 2
    pos = positions.reshape(t // ROPE_PACK, ROPE_PACK)
    inv_freq = ROPE_BASE ** (-jnp.arange(half, dtype=F32) / half)
    invf = jnp.tile(inv_freq, ROPE_PACK).reshape(1, 128)
    rb = 512
    out = jax.ShapeDtypeStruct((t, HEAD_SLOT), F32)
    return pl.pallas_call(
        _rope_kernel,
        grid=(t // ROPE_PACK // rb,),
        in_specs=[pl.BlockSpec((rb, ROPE_PACK), lambda i: (i, 0)), _const_spec((1, 128))],
        out_specs=[pl.BlockSpec((rb * ROPE_PACK, HEAD_SLOT), lambda i: (i, 0))] * 2,
        out_shape=[out, out],
        compiler_params=_cparams(1),
        name="rope_tables",
    )(pos, invf)


EV_CQ = 0
EV_CKV = EV_CQ + MLA_Q_RANK
EV_KPE = EV_CKV + MLA_KV_RANK
EV_GM = EV_KPE + HEAD_SLOT
EV_U = EV_GM + MLA_WIDTH
EV_GS = EV_U + S5_WIDTH
EV_COLS = EV_GS + S5_WIDTH
Q_SCALE = (MLA_NOPE + MLA_ROPE) ** -0.5 * math.log2(math.e)


def _rotate_pairs(x):
    n = x.shape[-1]
    lane = lax.broadcasted_iota(jnp.int32, x.shape, 1)
    lo = (lane & 112) == 64
    return jnp.where(lo, pltpu.roll(x, n - MLA_ROPE // 2, 1), pltpu.roll(x, MLA_ROPE // 2, 1))


def _even_in_kernel(x_ref, cos_ref, sin_ref, w_in_ref, qn_ref, kvn_ref, wuq_ref, wk_ref, wv_ref,
                    q_ref, k_ref, v_ref, gm_ref, u_ref, gs_ref, fold_scr):
    tm = x_ref.shape[0]
    nb = tm // ROW_SPLIT
    for r in range(ROW_SPLIT):
        rows = slice(r * nb, (r + 1) * nb)
        h = jnp.dot(x_ref[rows, :].astype(BF16), w_in_ref[...], preferred_element_type=F32)
        cos = cos_ref[rows, :]
        sin = sin_ref[rows, :]

        cqn = _rms(h[:, EV_CQ:EV_CKV], qn_ref[...]).astype(BF16)
        q = jnp.dot(cqn, wuq_ref[...], preferred_element_type=F32)
        q = q * jnp.tile(cos, (1, MLA_HEADS)) + _rotate_pairs(q) * jnp.tile(sin, (1, MLA_HEADS))
        q_ref[rows, :] = (q * Q_SCALE).astype(BF16)

        kpe = h[:, EV_KPE:EV_GM]
        kpe = kpe * cos + _rotate_pairs(kpe) * sin
        ckvn = _rms(h[:, EV_CKV:EV_KPE], kvn_ref[...]).astype(BF16)
        k = jnp.dot(ckvn, wk_ref[...], preferred_element_type=F32) + jnp.tile(kpe, (1, MLA_HEADS))
        k_ref[rows, :] = k.astype(BF16)
        v_ref[rows, :] = jnp.dot(ckvn, wv_ref[...], preferred_element_type=F32).astype(BF16)

        gm_ref[rows, :] = _silu(h[:, EV_GM:EV_U]).astype(BF16)
        fr = nb // S5_L
        folded = slice(r * fr, (r + 1) * fr)
        hw = S5_WIDTH // S5_HALVES
        for ref, val in ((u_ref, h[:, EV_U:EV_GS]), (gs_ref, _silu(h[:, EV_GS:EV_COLS]))):
            for half in range(S5_HALVES):
                fold_scr[half] = val[:, half * hw:(half + 1) * hw]
            for half in range(S5_HALVES):
                for s in range(S5_L):
                    lo = (half * S5_L + s) * hw
                    ref[folded, lo:lo + hw] = fold_scr[half, pl.ds(s, fr, stride=S5_L), :].astype(BF16)


def _even_in(xf, cos_full, sin_full, w_in, q_norm, kv_norm, w_uq, w_ukv):
    t = xf.shape[0]
    tm = TOKEN_TILE
    pad = HEAD_SLOT - MLA_NOPE - MLA_ROPE
    cq, ckv, kr, gm, u, gs = jnp.split(
        w_in, [MLA_Q_RANK, MLA_Q_RANK + MLA_KV_RANK, MLA_Q_RANK + MLA_KV_RANK + MLA_ROPE,
               MLA_Q_RANK + MLA_KV_RANK + MLA_ROPE + MLA_WIDTH,
               MLA_Q_RANK + MLA_KV_RANK + MLA_ROPE + MLA_WIDTH + S5_WIDTH], axis=1)
    kr = jnp.pad(kr, ((0, 0), (MLA_NOPE, pad)))
    w_in_p = jnp.concatenate([cq, ckv, kr, gm, u, gs], axis=1).astype(BF16)
    wuq_p = jnp.pad(w_uq.reshape(MLA_Q_RANK, MLA_HEADS, MLA_NOPE + MLA_ROPE),
                    ((0, 0), (0, 0), (0, pad))).reshape(MLA_Q_RANK, MLA_HEADS * HEAD_SLOT).astype(BF16)
    wkv = w_ukv.reshape(MLA_KV_RANK, MLA_HEADS, MLA_NOPE + MLA_V)
    wk_p = jnp.pad(wkv[:, :, :MLA_NOPE], ((0, 0), (0, 0), (0, HEAD_SLOT - MLA_NOPE))
                   ).reshape(MLA_KV_RANK, MLA_HEADS * HEAD_SLOT).astype(BF16)
    wv_p = wkv[:, :, MLA_NOPE:].reshape(MLA_KV_RANK, MLA_WIDTH).astype(BF16)

    def row(n):
        return pl.BlockSpec((tm, n), lambda i: (i, 0))

    qk_w = MLA_HEADS * HEAD_SLOT
    lw = S5_L * S5_WIDTH
    folded = jax.ShapeDtypeStruct((t // S5_L, lw), BF16)
    folded_spec = pl.BlockSpec((tm // S5_L, lw), lambda i: (i, 0))
    outs = [jax.ShapeDtypeStruct((t, qk_w), BF16), jax.ShapeDtypeStruct((t, qk_w), BF16),
            jax.ShapeDtypeStruct((t, MLA_WIDTH), BF16), jax.ShapeDtypeStruct((t, MLA_WIDTH), BF16),
            folded, folded]
    return pl.pallas_call(
        _even_in_kernel,
        grid=(t // tm,),
        in_specs=[row(D_MODEL), row(HEAD_SLOT), row(HEAD_SLOT),
                  _const_spec((D_MODEL, EV_COLS)), _const_spec((1, MLA_Q_RANK)),
                  _const_spec((1, MLA_KV_RANK)), _const_spec((MLA_Q_RANK, qk_w)),
                  _const_spec((MLA_KV_RANK, qk_w)), _const_spec((MLA_KV_RANK, MLA_WIDTH))],
        out_specs=[row(qk_w), row(qk_w), row(MLA_WIDTH), row(MLA_WIDTH), folded_spec, folded_spec],
        out_shape=outs,
        scratch_shapes=[pltpu.VMEM((S5_HALVES, tm // ROW_SPLIT, S5_WIDTH // S5_HALVES), F32)],
        compiler_params=_cparams(1),
        name="even_in_proj",
    )(xf, cos_full, sin_full, w_in_p, q_norm.reshape(1, -1), kv_norm.reshape(1, -1), wuq_p, wk_p, wv_p)


def _attn_kernel(q_ref, k_ref, v_ref, g_ref, o_ref):
    seq = q_ref.shape[1]
    t = ATTN_T
    n = seq // t
    nt = (((1,), (1,)), ((), ()))
    step = pl.program_id(2)

    @pl.when(step == 0)
    def _():
        o_ref[...] = jnp.zeros_like(o_ref)

    keep = lax.broadcasted_iota(jnp.int32, (t, t), 1) <= lax.broadcasted_iota(jnp.int32, (t, t), 0)
    lane_head = lax.broadcasted_iota(jnp.int32, (t, ATTN_GROUP * MLA_V), 1) // MLA_V

    def scores(qi, c):
        rows = slice(qi * t, (qi + 1) * t)
        lanes = slice(c * HEAD_SLOT, (c + 1) * HEAD_SLOT)
        q = q_ref[0, rows, lanes]
        sd = lax.dot_general(q, k_ref[0, rows, lanes], nt, preferred_element_type=F32)
        sd = jnp.where(keep, sd, NEG_BIG)
        if qi == 0:
            return sd
        s = lax.dot_general(q, k_ref[0, :qi * t, lanes], nt, preferred_element_type=F32)
        return jnp.concatenate([s, sd], axis=1)

    def attend(qi, ss):
        rows = slice(qi * t, (qi + 1) * t)
        o = None
        for c, s in enumerate(ss):
            slabs = [s[:, j:j + 128] for j in range(0, s.shape[1], 128)]
            m = functools.reduce(jnp.maximum, slabs)
            m = jnp.max(m, axis=1, keepdims=True)
            lsum = jnp.zeros((t, 128), F32)
            ps = []
            for sl in slabs:
                pj = jnp.exp2(sl - m)
                lsum = lsum + pj
                ps.append(pj.astype(BF16))
            l = jnp.sum(lsum, axis=1, keepdims=True)
            p = jnp.concatenate(ps, axis=1)
            acc = jnp.dot(p, v_ref[0, :(qi + 1) * t, :], preferred_element_type=F32)
            oc = acc * (1.0 / l)
            o = oc if o is None else jnp.where(lane_head == step * ATTN_STEP_HEADS + c, oc, o)
        o = (o * g_ref[0, rows, :].astype(F32)).astype(BF16)
        o_ref[0, rows, :] = jnp.where(lane_head // ATTN_STEP_HEADS == step, o, o_ref[0, rows, :])

    pending = None
    for qi in reversed(range(n)):
        ss = [scores(qi, c) for c in range(ATTN_STEP_HEADS)]
        if pending is not None:
            attend(*pending)
        pending = (qi, ss)
    attend(*pending)


def _attention(q, k, v, g, batch, seq):
    q = q.reshape(batch, seq, -1)
    k = k.reshape(batch, seq, -1)
    v = v.reshape(batch, seq, -1)
    g = g.reshape(batch, seq, -1)
    gw = ATTN_GROUP * MLA_V
    steps = ATTN_GROUP // ATTN_STEP_HEADS
    head = pl.BlockSpec((1, seq, ATTN_STEP_HEADS * HEAD_SLOT), lambda b, gi, st: (b, 0, gi * steps + st))
    group = pl.BlockSpec((1, seq, gw), lambda b, gi, st: (b, 0, gi))
    o = pl.pallas_call(
        _attn_kernel,
        grid=(batch, MLA_HEADS // ATTN_GROUP, steps),
        in_specs=[head, head, group, group],
        out_specs=group,
        out_shape=jax.ShapeDtypeStruct((batch, seq, MLA_WIDTH), BF16),
        compiler_params=_cparams(3),
        name="mla_attention",
    )(q, k, v, g)
    return o.reshape(batch * seq, MLA_WIDTH)


def _s5_matrices(a_re, a_im, log_dt, b_re, b_im, c_re, c_im):
    L, G, P, H = S5_L, S5_GROUPS, S5_STATE, S5_GROUP
    dt = jnp.exp(log_dt.astype(F32))[:, None]
    ar, ai = a_re.astype(F32), a_im.astype(F32)
    mag = jnp.exp(ar * dt)
    lr, li = mag * jnp.cos(ai * dt), mag * jnp.sin(ai * dt)
    den = ar * ar + ai * ai
    nr, ni = lr - 1.0, li
    zr = (nr * ar + ni * ai) / den
    zi = (ni * ar - nr * ai) / den
    br, bi = b_re.astype(F32), b_im.astype(F32)
    bbr = zr[..., None] * br - zi[..., None] * bi
    bbi = zr[..., None] * bi + zi[..., None] * br
    cr, ci = c_re.astype(F32), c_im.astype(F32)

    def powers(j):
        j = j.astype(F32)[:, None, None]
        pmag = jnp.exp(j * (ar * dt))
        return pmag * jnp.cos(j * (ai * dt)), pmag * jnp.sin(j * (ai * dt))

    gp, w = G * P, G * H
    pr, pi = powers(jnp.arange(L + 1))
    prow = jnp.concatenate([pr.reshape(L + 1, gp), pi.reshape(L + 1, gp)], axis=0)
    prow3 = prow.reshape(2 * (L + 1), 1, gp)
    pcol3 = prow.reshape(2 * (L + 1), gp, 1)

    same_b = (jnp.arange(w)[:, None] // H) == (jnp.arange(gp)[None, :] // P)
    same_c = (jnp.arange(gp)[:, None] // P) == (jnp.arange(w)[None, :] // H)
    bd_br = jnp.where(same_b, jnp.tile(bbr.transpose(2, 0, 1).reshape(H, gp), (G, 1)), 0.0)
    bd_bi = jnp.where(same_b, jnp.tile(bbi.transpose(2, 0, 1).reshape(H, gp), (G, 1)), 0.0)
    bd_cr = jnp.where(same_c, jnp.tile(cr.transpose(0, 2, 1).reshape(gp, H), (1, G)), 0.0)
    bd_ci = jnp.where(same_c, jnp.tile(ci.transpose(0, 2, 1).reshape(gp, H), (1, G)), 0.0)

    nh, hw = S5_HALVES, S5_WIDTH // S5_HALVES
    bmat, tmat, cmat = pl.pallas_call(
        _s5_prepare_kernel,
        grid=(L,),
        in_specs=[_const_spec((w, gp)), _const_spec((w, gp)), _const_spec((gp, w)), _const_spec((gp, w)),
                  _const_spec((2 * (L + 1), gp)),
                  pl.BlockSpec((1, 1, gp), lambda i: (L - 1 - i, 0, 0)),
                  pl.BlockSpec((1, 1, gp), lambda i: (2 * L - i, 0, 0)),
                  pl.BlockSpec((1, gp, 1), lambda i: (i + 1, 0, 0)),
                  pl.BlockSpec((1, gp, 1), lambda i: (L + 2 + i, 0, 0))],
        out_specs=[pl.BlockSpec((nh, hw, 2 * gp // nh), lambda i: (0, i, 0)),
                   pl.BlockSpec((nh, hw, L * hw), lambda i: (0, i, 0)),
                   pl.BlockSpec((nh, 2 * gp // nh, hw), lambda i: (0, 0, i))],
        out_shape=[jax.ShapeDtypeStruct((nh, L * hw, 2 * gp // nh), BF16),
                   jax.ShapeDtypeStruct((nh, L * hw, L * hw), BF16),
                   jax.ShapeDtypeStruct((nh, 2 * gp // nh, L * hw), BF16)],
        scratch_shapes=[pltpu.VMEM((L, w, w), F32)],
        compiler_params=_cparams(1),
        name="s5_prepare",
    )(bd_br, bd_bi, bd_cr, bd_ci, prow, prow3, prow3, pcol3, pcol3)
    lam = jnp.stack([pr[L].reshape(gp), pi[L].reshape(gp)])
    return bmat, tmat, cmat, lam


def _s5_prepare_kernel(br_ref, bi_ref, cr_ref, ci_ref, prow_ref, rr_ref, ri_ref, cr_pow_ref, ci_pow_ref,
                       bmat_ref, tmat_ref, cmat_ref, k_scr):
    L = S5_L
    hw = S5_WIDTH // S5_HALVES
    hgp = S5_GROUPS * S5_STATE // S5_HALVES
    i = pl.program_id(0)
    br, bi = br_ref[...], bi_ref[...]
    cr, ci = cr_ref[...], ci_ref[...]

    @pl.when(i == 0)
    def _():
        for j in range(L):
            pr = prow_ref[j:j + 1, :]
            pi = prow_ref[L + 1 + j:L + 2 + j, :]
            k_scr[j] = (jnp.dot(br * pr - bi * pi, cr, precision=lax.Precision.HIGHEST,
                                preferred_element_type=F32)
                        - jnp.dot(bi * pr + br * pi, ci, precision=lax.Precision.HIGHEST,
                                  preferred_element_type=F32))

    pr, pi = rr_ref[0], ri_ref[0]
    pcr, pci = cr_pow_ref[0], ci_pow_ref[0]
    pb_r, pb_i = br * pr - bi * pi, bi * pr + br * pi
    g_r, g_i = cr * pcr - ci * pci, -(cr * pci + ci * pcr)
    lags = [k_scr[jnp.maximum(t - i, 0)] for t in range(L)]
    for h in range(S5_HALVES):
        ch = slice(h * hw, (h + 1) * hw)
        st = slice(h * hgp, (h + 1) * hgp)
        bmat_ref[h, :, :hgp] = pb_r[ch, st].astype(BF16)
        bmat_ref[h, :, hgp:] = pb_i[ch, st].astype(BF16)
        for t in range(L):
            tmat_ref[h, :, t * hw:(t + 1) * hw] = jnp.where(t >= i, lags[t][ch, ch], 0.0).astype(BF16)
        cmat_ref[h, :hgp, :] = g_r[st, ch].astype(BF16)
        cmat_ref[h, hgp:, :] = g_i[st, ch].astype(BF16)


def _gelu_tanh(y):
    return 0.5 * y * (1.0 + jnp.tanh(math.sqrt(2.0 / math.pi) * (y + 0.044715 * (y * y * y))))


def _s5_kernel(u_ref, gs_ref, bmat_ref, tmat_ref, cmat_ref, lam_ref, dskip_ref, wglu_ref, bglu_ref,
               o_ref, x_scr, h_scr):
    nh = S5_HALVES
    hw = S5_WIDTH // nh
    gp = S5_GROUPS * S5_STATE
    hgp = gp // nh
    hl = S5_L * hw
    rows = u_ref.shape[0]
    per_batch = rows // S5_NB
    for h in range(nh):
        xh = jnp.dot(u_ref[:, h * hl:(h + 1) * hl], bmat_ref[h], preferred_element_type=F32)
        x_scr[:, h * hgp:(h + 1) * hgp] = xh[:, :hgp]
        x_scr[:, gp + h * hgp:gp + (h + 1) * hgp] = xh[:, hgp:]
    lam_r = lam_ref[0:1, :]
    lam_i = lam_ref[1:2, :]

    def block(i, hs):
        out = []
        for b in range(S5_NB):
            hr, hi = hs[b]
            r0 = pl.multiple_of(b * per_batch + i * 8, 8)
            x8 = x_scr[pl.ds(r0, 8), :]
            starts_r, starts_i = [], []
            for jj in range(8):
                starts_r.append(hr)
                starts_i.append(hi)
                xr = x8[jj:jj + 1, :gp]
                xi = x8[jj:jj + 1, gp:]
                hr, hi = lam_r * hr - lam_i * hi + xr, lam_r * hi + lam_i * hr + xi
            h_scr[pl.ds(r0, 8), :gp] = jnp.concatenate(starts_r, axis=0)
            h_scr[pl.ds(r0, 8), gp:] = jnp.concatenate(starts_i, axis=0)
            out.append((hr, hi))
        return tuple(out)

    zero = jnp.zeros((1, gp), F32)
    lax.fori_loop(0, per_batch // 8, block, tuple((zero, zero) for _ in range(S5_NB)))

    hs = h_scr[...].astype(BF16)
    hs = [jnp.concatenate([hs[:, h * hgp:(h + 1) * hgp], hs[:, gp + h * hgp:gp + (h + 1) * hgp]], axis=1)
          for h in range(nh)]

    pair = 2 * hw
    ypairs = [[jnp.dot(u_ref[:, h * hl:h * hl + (tp + 1) * pair],
                       tmat_ref[h, :(tp + 1) * pair, tp * pair:(tp + 1) * pair], preferred_element_type=F32)
               + jnp.dot(hs[h], cmat_ref[h, :, tp * pair:(tp + 1) * pair], preferred_element_type=F32)
               for h in range(nh)] for tp in range(S5_L // 2)]

    def token(ref, t):
        return jnp.concatenate([ref[:, h * hl + t * hw:h * hl + (t + 1) * hw] for h in range(nh)], axis=1)

    ys = []
    for t in range(S5_L):
        e = (t % 2) * hw
        y = jnp.concatenate([yh[:, e:e + hw] for yh in ypairs[t // 2]], axis=1)
        ys.append(_gelu_tanh(y + dskip_ref[...] * token(u_ref, t).astype(F32)))
    zs = [jnp.dot(y.astype(BF16), wglu_ref[...], preferred_element_type=F32) for y in ys]
    for t, (y, z) in enumerate(zip(ys, zs)):
        o = (y * _sigmoid(z + bglu_ref[...])) * token(gs_ref, t).astype(F32)
        for h in range(nh):
            o_ref[h, pl.ds(t, rows, stride=S5_L), :] = o[:, h * hw:(h + 1) * hw]


def _s5(u, gs, batch, seq, a_re, a_im, log_dt, b_re, b_im, c_re, c_im, d_skip, w_glu, b_glu):
    t = batch * seq
    lw = S5_L * S5_WIDTH
    gp2 = 2 * S5_GROUPS * S5_STATE
    bmat, tmat, cmat, lam = _s5_matrices(a_re, a_im, log_dt, b_re, b_im, c_re, c_im)
    rows = S5_NB * seq // S5_L
    halves = S5_HALVES
    blk = pl.BlockSpec((rows, lw), lambda i: (i, 0))
    return pl.pallas_call(
        _s5_kernel,
        grid=(batch // S5_NB,),
        in_specs=[blk, blk, _const_spec(bmat.shape, True), _const_spec(tmat.shape, True),
                  _const_spec(cmat.shape, True), _const_spec((2, gp2 // 2)),
                  _const_spec((1, S5_WIDTH)), _const_spec((S5_WIDTH, S5_WIDTH)), _const_spec((1, S5_WIDTH))],
        out_specs=pl.BlockSpec((halves, rows * S5_L, 128), lambda i: (0, i, 0)),
        out_shape=jax.ShapeDtypeStruct((halves, t, 128), F32),
        scratch_shapes=[pltpu.VMEM((rows, gp2), F32), pltpu.VMEM((rows, gp2), F32)],
        compiler_params=_cparams(1),
        name="s5_mixer",
    )(u, gs, bmat, tmat, cmat, lam, d_skip.reshape(1, -1).astype(F32), w_glu.astype(BF16),
      b_glu.reshape(1, -1).astype(F32))


def _out_ln_kernel(n_act, then_odd_in, x_ref, *refs):
    acts = refs[:n_act]
    ws = refs[n_act:2 * n_act]
    g_ref, b_ref = refs[2 * n_act:2 * n_act + 2]
    rest = refs[2 * n_act + 2:]
    o_ref = rest[3] if then_odd_in else rest[0]

    def operand(ref, rows):
        if len(ref.shape) == 2:
            return ref[rows, :]
        return jnp.concatenate([ref[i, rows, :] for i in range(ref.shape[0])], axis=1).astype(BF16)

    tm = x_ref.shape[0]
    nblk = ROW_SPLIT if then_odd_in else 1
    blocks = [slice(r * tm // nblk, (r + 1) * tm // nblk) for r in range(nblk)]
    xns = []
    for rows in blocks:
        y = jnp.dot(operand(acts[0], rows), ws[0][...], preferred_element_type=F32)
        for a, w in zip(acts[1:], ws[1:]):
            y = y + jnp.dot(operand(a, rows), w[...], preferred_element_type=F32)
        z = DEEPNORM_ALPHA * x_ref[rows, :] + y
        mu = jnp.mean(z, axis=-1, keepdims=True)
        zc = z - mu
        var = jnp.mean(zc * zc, axis=-1, keepdims=True)
        xn = zc * lax.rsqrt(var + LN_EPS) * g_ref[...] + b_ref[...]
        o_ref[rows, :] = xn
        xns.append(xn.astype(BF16))
    if then_odd_in:
        for rows, xb in zip(blocks, xns):
            _odd_in_body(xb, rows, *rest[:3], *rest[4:])


def _out_ln(xf, acts, ws, ln_g, ln_b, odd_in=None):
    t = xf.shape[0]
    tm = TOKEN_TILE
    n = len(acts)

    def row(c):
        return pl.BlockSpec((tm, c), lambda i: (i, 0))

    def act_spec(a):
        if a.ndim == 2:
            return row(a.shape[1])
        return pl.BlockSpec((a.shape[0], tm, a.shape[2]), lambda i: (0, i, 0))

    in_specs = ([row(D_MODEL)] + [act_spec(a) for a in acts] + [_const_spec(w.shape) for w in ws]
                + [_const_spec((1, D_MODEL)), _const_spec((1, D_MODEL))])
    args = [xf, *acts, *ws, ln_g.reshape(1, -1), ln_b.reshape(1, -1)]
    out_specs = [row(D_MODEL)]
    out_shape = [jax.ShapeDtypeStruct((t, D_MODEL), F32)]
    if odd_in is not None:
        odd_args, odd_specs, odd_out_specs, odd_out_shape = _odd_in_operands(t, tm, *odd_in)
        in_specs += odd_specs
        args += odd_args
        out_specs += odd_out_specs
        out_shape += odd_out_shape
    outs = pl.pallas_call(
        functools.partial(_out_ln_kernel, n, odd_in is not None),
        grid=(t // tm,),
        in_specs=in_specs,
        out_specs=out_specs,
        out_shape=out_shape,
        compiler_params=_cparams(1),
        name="out_proj_layernorm" if odd_in is None else "out_proj_layernorm_odd_in_proj",
    )(*args)
    return outs[0] if odd_in is None else outs


OD_Q = 0
OD_K = OD_Q + GLA_KEY_WIDTH
OD_V = OD_K + GLA_KEY_WIDTH
OD_GL = OD_V + GLA_WIDTH
OD_G = OD_GL + 128
OD_COLS = OD_G + GLA_WIDTH


def _odd_in_kernel(x_ref, *refs):
    _odd_in_body(x_ref[...].astype(BF16), slice(None), *refs)


def _odd_in_body(xb, rows, w_ref, wgk_ref, bgk_ref, q_ref, k_ref, v_ref, la_ref, g_ref):
    def proj(lo, hi):
        return jnp.dot(xb, w_ref[:, lo:hi], preferred_element_type=F32)

    z = jnp.dot(proj(OD_GL, OD_G).astype(BF16), wgk_ref[...], preferred_element_type=F32) + bgk_ref[...]
    g = proj(OD_G, OD_COLS)
    log_sig = -(jnp.maximum(-z, 0.0) + jnp.log1p(jnp.exp(-jnp.abs(z))))
    la_ref[rows, :] = log_sig / GLA_GATE_NORM
    v_ref[rows, :] = proj(OD_V, OD_GL).astype(BF16)
    g_ref[rows, :] = _silu(g).astype(BF16)
    q_ref[rows, :] = proj(OD_Q, OD_K).astype(BF16)
    k_ref[rows, :] = proj(OD_K, OD_V).astype(BF16)


def _odd_in_operands(t, tm, w_in, w_gk2, b_gk):
    q, k, v, gl, g = jnp.split(w_in, [OD_K, OD_V, OD_GL, OD_GL + GLA_GATE_RANK], axis=1)
    gl = jnp.pad(gl, ((0, 0), (0, 128 - GLA_GATE_RANK)))
    w_p = jnp.concatenate([q, k, v, gl, g], axis=1).astype(BF16)
    wgk_p = jnp.pad(w_gk2, ((0, 128 - GLA_GATE_RANK), (0, 0))).astype(BF16)

    def row(c):
        return pl.BlockSpec((tm, c), lambda i: (i, 0))

    args = [w_p, wgk_p, b_gk.reshape(1, -1).astype(F32)]
    specs = [_const_spec((D_MODEL, OD_COLS)), _const_spec((128, GLA_KEY_WIDTH)), _const_spec((1, GLA_KEY_WIDTH))]
    out_specs = [row(GLA_KEY_WIDTH), row(GLA_KEY_WIDTH), row(GLA_WIDTH), row(GLA_KEY_WIDTH), row(GLA_WIDTH)]
    out_shape = [jax.ShapeDtypeStruct((t, GLA_KEY_WIDTH), BF16), jax.ShapeDtypeStruct((t, GLA_KEY_WIDTH), BF16),
                 jax.ShapeDtypeStruct((t, GLA_WIDTH), BF16), jax.ShapeDtypeStruct((t, GLA_KEY_WIDTH), F32),
                 jax.ShapeDtypeStruct((t, GLA_WIDTH), BF16)]
    return args, specs, out_specs, out_shape


def _odd_in(xf, w_in, w_gk2, b_gk):
    t = xf.shape[0]
    tm = TOKEN_TILE
    args, specs, out_specs, out_shape = _odd_in_operands(t, tm, w_in, w_gk2, b_gk)
    return pl.pallas_call(
        _odd_in_kernel,
        grid=(t // tm,),
        in_specs=[pl.BlockSpec((tm, D_MODEL), lambda i: (i, 0))] + specs,
        out_specs=out_specs,
        out_shape=out_shape,
        compiler_params=_cparams(1),
        name="odd_in_proj",
    )(xf, *args)


def _gla_kernel(q_ref, k_ref, v_ref, la_ref, g_ref, gn_ref, o_ref, bc_scr):
    c, sb = GLA_CHUNK, GLA_SUPER
    seq = q_ref.shape[1]
    ri = lax.broadcasted_iota(jnp.int32, (sb, sb), 0)
    ci = lax.broadcasted_iota(jnp.int32, (sb, sb), 1)
    valid = jnp.logical_and(ci <= ri, ci >= (ri // c) * c)
    tri = jnp.where(valid, 1.0, 0.0).astype(BF16)
    blocks = [slice(s * sb, (s + 1) * sb) for s in range(seq // sb)]
    chunks = [slice(n * c, (n + 1) * c) for n in range(seq // c)]
    heads = [_gla_head(hh, q_ref, k_ref, v_ref, la_ref, g_ref, gn_ref, o_ref, bc_scr, blocks, chunks, tri, valid)
             for hh in range(GLA_STEP_HEADS)]
    live = list(enumerate(heads))
    tick = 0
    while live:
        for i, gen in list(live):
            if tick >= i * GLA_STAGE_SKEW and next(gen, StopIteration) is StopIteration:
                live.remove((i, gen))
        tick += 1


def _gla_head(hh, q_ref, k_ref, v_ref, la_ref, g_ref, gn_ref, o_ref, bc_scr, blocks, chunks, tri, valid):
    c, sb = GLA_CHUNK, GLA_SUPER
    cps = sb // c
    nch = len(chunks)
    nt = (((1,), (1,)), ((), ()))
    tn = (((0,), (0,)), ((), ()))
    scale = GLA_DK ** -0.5
    kl = slice(hh * GLA_DK, (hh + 1) * GLA_DK)
    vl = slice(hh * GLA_DV, (hh + 1) * GLA_DV)

    bcs = []
    for rows in blocks:
        ga = la_ref[0, rows, kl]
        ga_hi = ga.astype(BF16)
        ga_lo = (ga - ga_hi.astype(F32)).astype(BF16)
        cs = jnp.dot(tri, jnp.concatenate([ga_hi, ga_lo], axis=1), preferred_element_type=F32)
        bc = cs[:, :GLA_DK] + cs[:, GLA_DK:]
        bc_scr[hh, rows, :] = bc
        bcs.append(bc.reshape(cps, c, GLA_DK))
    yield

    qds, k_invs, k_ends = [], [], []
    for rows, bc3 in zip(blocks, bcs):
        bl3 = bc3[:, c - 1:c, :]
        q3 = q_ref[0, rows, kl].astype(F32).reshape(cps, c, GLA_DK) * scale
        k3 = k_ref[0, rows, kl].astype(F32).reshape(cps, c, GLA_DK)
        qds.append((q3 * jnp.exp(bc3)).reshape(sb, GLA_DK).astype(BF16))
        k_invs.append((k3 * jnp.exp(-bc3)).reshape(sb, GLA_DK).astype(BF16))
        k_ends.append((k3 * jnp.exp(bl3 - bc3)).reshape(sb, GLA_DK).astype(BF16))
    yield

    atts = [lax.dot_general(qd, k_inv, nt, preferred_element_type=F32) for qd, k_inv in zip(qds, k_invs)]
    yield
    atts = [jnp.where(valid, att, 0.0).astype(BF16) for att in atts]
    o_intras = [jnp.dot(att, v_ref[0, rows, vl], preferred_element_type=F32)
                for att, rows in zip(atts, blocks)]
    yield

    def chunk_of(vals, n):
        s, j = divmod(n, cps)
        return vals[s][j * c:(j + 1) * c]

    kvs = [lax.dot_general(chunk_of(k_ends, n), v_ref[0, chunks[n], vl], tn, preferred_element_type=F32)
           for n in range(nch - 1)]
    yield

    b_last = bc_scr[hh, pl.ds(c - 1, nch, stride=c), :]
    dec = jnp.concatenate([jnp.exp(b_last), jnp.zeros((GLA_DK - nch, GLA_DK), F32)], axis=0)
    dec_t = dec.T

    st = jnp.zeros((GLA_DK, GLA_DV), F32)
    states = [st.astype(BF16)]
    for n in range(nch - 1):
        st = st * dec_t[:, n:n + 1] + kvs[n]
        states.append(st.astype(BF16))
    yield

    o_inters = [jnp.dot(chunk_of(qds, n), states[n], preferred_element_type=F32) for n in range(nch)]
    yield
    for n in range(nch):
        o = chunk_of(o_intras, n) + o_inters[n]
        o = _rms(o, gn_ref[...]) * g_ref[0, chunks[n], vl].astype(F32)
        o_ref[0, chunks[n], vl] = o.astype(BF16)


def _gla(q, k, v, la, g, g_norm, batch, seq):
    q = q.reshape(batch, seq, -1)
    k = k.reshape(batch, seq, -1)
    v = v.reshape(batch, seq, -1)
    la = la.reshape(batch, seq, -1)
    g = g.reshape(batch, seq, -1)

    def spec(c):
        return pl.BlockSpec((1, seq, GLA_STEP_HEADS * c), lambda b, h: (b, 0, h))

    o = pl.pallas_call(
        _gla_kernel,
        grid=(batch, GLA_HEADS // GLA_STEP_HEADS),
        in_specs=[spec(GLA_DK), spec(GLA_DK), spec(GLA_DV), spec(GLA_DK), spec(GLA_DV),
                  _const_spec((1, GLA_DV))],
        out_specs=spec(GLA_DV),
        out_shape=jax.ShapeDtypeStruct((batch, seq, GLA_WIDTH), BF16),
        scratch_shapes=[pltpu.VMEM((GLA_STEP_HEADS, seq, GLA_DK), F32)],
        compiler_params=_cparams(2),
        name="gla_mixer",
    )(q, k, v, la, g, g_norm.reshape(1, -1).astype(F32))
    return o.reshape(batch * seq, GLA_WIDTH)


def kernel(x, positions, ln_g, ln_b, even_w_in, mla_q_norm, mla_kv_norm, mla_w_uq, mla_w_ukv, s5_a_re, s5_a_im, s5_log_dt, s5_b_re, s5_b_im, s5_c_re, s5_c_im, s5_d, s5_w_glu, s5_b_glu, even_w_out, odd_w_in, gla_w_gk2, gla_b_gk, gla_g_norm, odd_w_out):
    batch, seq, _ = x.shape
    xf = x.reshape(batch * seq, D_MODEL)
    cos_full, sin_full = _rope_tables(positions)
    odd_inputs = None
    for layer in range(DEPTH):
        j = layer // 2
        if layer % 2 == 0:
            q, k, v, gm, u, gs = _even_in(xf, cos_full, sin_full, even_w_in[j], mla_q_norm[j],
                                          mla_kv_norm[j], mla_w_uq[j], mla_w_ukv[j])
            o_mla = _attention(q, k, v, gm, batch, seq)
            o_s5 = _s5(u, gs, batch, seq, s5_a_re[j], s5_a_im[j], s5_log_dt[j], s5_b_re[j], s5_b_im[j],
                       s5_c_re[j], s5_c_im[j], s5_d[j], s5_w_glu[j], s5_b_glu[j])
            w_out = even_w_out[j].astype(BF16)
            nxt = (odd_w_in[j], gla_w_gk2[j], gla_b_gk[j]) if layer + 1 < DEPTH else None
            res = _out_ln(xf, [o_mla, o_s5], [w_out[:MLA_WIDTH], w_out[MLA_WIDTH:]], ln_g[layer], ln_b[layer],
                          odd_in=nxt)
            xf, odd_inputs = (res, None) if nxt is None else (res[0], res[1:])
        else:
            if odd_inputs is None:
                odd_inputs = _odd_in(xf, odd_w_in[j], gla_w_gk2[j], gla_b_gk[j])
            q, k, v, la, g = odd_inputs
            odd_inputs = None
            o = _gla(q, k, v, la, g, gla_g_norm[j], batch, seq)
            xf = _out_ln(xf, [o], [odd_w_out[j].astype(BF16)], ln_g[layer], ln_b[layer])
    return xf.reshape(batch, seq, D_MODEL)
```

```python
import functools
import itertools
import math

import jax
import jax.numpy as jnp
from jax import lax
from jax.experimental import pallas as pl
from jax.experimental.pallas import tpu as pltpu

F32 = jnp.float32
BF16 = jnp.bfloat16

D_MODEL = 1024
DEPTH = 2

MLA_HEADS = 8
MLA_NOPE = 64
MLA_ROPE = 32
MLA_V = 64
MLA_Q_RANK = 256
MLA_KV_RANK = 128
MLA_WIDTH = MLA_HEADS * MLA_V
ROPE_BASE = 10000.0
HEAD_SLOT = 128

S5_WIDTH = 256
S5_GROUP = 16
S5_GROUPS = S5_WIDTH // S5_GROUP
S5_STATE = 64
S5_L = 8
S5_NB = 2
S5_HALVES = 2

GLA_HEADS = 4
GLA_KEY_WIDTH = D_MODEL // 2
GLA_WIDTH = D_MODEL
GLA_DK = GLA_KEY_WIDTH // GLA_HEADS
GLA_DV = GLA_WIDTH // GLA_HEADS
GLA_GATE_RANK = 16
GLA_GATE_NORM = 16.0
GLA_CHUNK = 64
GLA_SUPER = 256
GLA_STEP_HEADS = 2
GLA_STAGE_SKEW = 3

DEEPNORM_ALPHA = (2 * DEPTH) ** 0.25
LN_EPS = 1e-5
RMS_EPS = 1e-6

TOKEN_TILE = 1024
ROW_SPLIT = 4
ATTN_T = 256
ATTN_STEP_HEADS = 2
NEG_BIG = -1e30
VMEM_LIMIT = 56 * 1024 * 1024


def _cparams(n_axes):
    return pltpu.CompilerParams(dimension_semantics=("arbitrary",) * n_axes,
                                vmem_limit_bytes=VMEM_LIMIT)


def _const_spec(shape, single=False):
    nd = len(shape)
    if single:
        return pl.BlockSpec(shape, lambda *_: (0,) * nd, pipeline_mode=pl.Buffered(1))
    return pl.BlockSpec(shape, lambda *_: (0,) * nd)


def _sigmoid(x):
    return 1.0 / (1.0 + jnp.exp(-x))


def _silu(x):
    return x * _sigmoid(x)


def _rms(x, g):
    return (x * lax.rsqrt(jnp.mean(x * x, axis=-1, keepdims=True) + RMS_EPS)) * g


ROPE_PACK = 128 // (MLA_ROPE // 2)


def _rope_tables(pos_ref, invf_ref, cos_ref, sin_ref):
    half = MLA_ROPE // 2
    rb = pos_ref.shape[0]
    pos = pos_ref[...].astype(F32)
    lane = lax.broadcasted_iota(jnp.int32, (rb, 128), 1)
    owner = lane // half
    posx = jnp.zeros((rb, 128), F32)
    for s in range(ROPE_PACK):
        posx = jnp.where(owner == s, pos[:, s:s + 1], posx)
    ang = posx * invf_ref[...]
    c = jnp.cos(ang)
    sn = jnp.sin(ang)
    first = (lane & 112) == MLA_NOPE
    second = (lane & 112) == MLA_NOPE + half
    for s in range(ROPE_PACK):
        k1 = (MLA_NOPE - half * s) % 128
        k2 = (MLA_NOPE + half - half * s) % 128
        cf = jnp.where(first, pltpu.roll(c, k1, 1), jnp.where(second, pltpu.roll(c, k2, 1), 1.0))
        sf = jnp.where(first, -pltpu.roll(sn, k1, 1), jnp.where(second, pltpu.roll(sn, k2, 1), 0.0))
        cos_ref[pl.ds(s, rb, stride=ROPE_PACK), :] = cf
        sin_ref[pl.ds(s, rb, stride=ROPE_PACK), :] = sf


EV_CQ = 0
EV_CKV = EV_CQ + MLA_Q_RANK
EV_KPE = EV_CKV + MLA_KV_RANK
EV_GM = EV_KPE + HEAD_SLOT
EV_U = EV_GM + MLA_WIDTH
EV_GS = EV_U + S5_WIDTH
EV_COLS = EV_GS + S5_WIDTH
Q_SCALE = (MLA_NOPE + MLA_ROPE) ** -0.5 * math.log2(math.e)


def _rotate_pairs(x):
    n = x.shape[-1]
    lane = lax.broadcasted_iota(jnp.int32, x.shape, 1)
    lo = (lane & 112) == 64
    return jnp.where(lo, pltpu.roll(x, n - MLA_ROPE // 2, 1), pltpu.roll(x, MLA_ROPE // 2, 1))


def _even_in_kernel(x_ref, pos_ref, invf_ref, w_in_ref, qn_ref, kvn_ref, wuq_ref, wk_ref, wv_ref,
                    q_ref, k_ref, v_ref, gm_ref, u_ref, gs_ref, fold_scr, cos_ref, sin_ref):
    _rope_tables(pos_ref, invf_ref, cos_ref, sin_ref)
    tm = x_ref.shape[0]
    nb = tm // ROW_SPLIT
    for r in range(ROW_SPLIT):
        rows = slice(r * nb, (r + 1) * nb)
        h = jnp.dot(x_ref[rows, :].astype(BF16), w_in_ref[...], preferred_element_type=F32)
        cos = cos_ref[rows, :]
        sin = sin_ref[rows, :]

        cqn = _rms(h[:, EV_CQ:EV_CKV], qn_ref[...]).astype(BF16)
        q = jnp.dot(cqn, wuq_ref[...], preferred_element_type=F32)
        q = q * jnp.tile(cos, (1, MLA_HEADS)) + _rotate_pairs(q) * jnp.tile(sin, (1, MLA_HEADS))
        q_ref[rows, :] = (q * Q_SCALE).astype(BF16)

        kpe = h[:, EV_KPE:EV_GM]
        kpe = kpe * cos + _rotate_pairs(kpe) * sin
        ckvn = _rms(h[:, EV_CKV:EV_KPE], kvn_ref[...]).astype(BF16)
        k = jnp.dot(ckvn, wk_ref[...], preferred_element_type=F32) + jnp.tile(kpe, (1, MLA_HEADS))
        k_ref[rows, :] = k.astype(BF16)
        v_ref[rows, :] = jnp.dot(ckvn, wv_ref[...], preferred_element_type=F32).astype(BF16)

        gm_ref[rows, :] = _silu(h[:, EV_GM:EV_U]).astype(BF16)
        fr = nb // S5_L
        folded = slice(r * fr, (r + 1) * fr)
        hw = S5_WIDTH // S5_HALVES
        for ref, val in ((u_ref, h[:, EV_U:EV_GS]), (gs_ref, _silu(h[:, EV_GS:EV_COLS]))):
            for half in range(S5_HALVES):
                fold_scr[half] = val[:, half * hw:(half + 1) * hw]
            for half in range(S5_HALVES):
                for s in range(S5_L):
                    lo = (half * S5_L + s) * hw
                    ref[folded, lo:lo + hw] = fold_scr[half, pl.ds(s, fr, stride=S5_L), :].astype(BF16)


def _even_in(xf, positions, w_in, q_norm, kv_norm, w_uq, w_ukv):
    t = xf.shape[0]
    tm = TOKEN_TILE
    pad = HEAD_SLOT - MLA_NOPE - MLA_ROPE
    half = MLA_ROPE // 2
    pos = positions.reshape(t // ROPE_PACK, ROPE_PACK)
    inv_freq = ROPE_BASE ** (-jnp.arange(half, dtype=F32) / half)
    invf = jnp.tile(inv_freq, ROPE_PACK).reshape(1, 128)
    cq, ckv, kr, gm, u, gs = jnp.split(
        w_in, [MLA_Q_RANK, MLA_Q_RANK + MLA_KV_RANK, MLA_Q_RANK + MLA_KV_RANK + MLA_ROPE,
               MLA_Q_RANK + MLA_KV_RANK + MLA_ROPE + MLA_WIDTH,
               MLA_Q_RANK + MLA_KV_RANK + MLA_ROPE + MLA_WIDTH + S5_WIDTH], axis=1)
    kr = jnp.pad(kr, ((0, 0), (MLA_NOPE, pad)))
    w_in_p = jnp.concatenate([cq, ckv, kr, gm, u, gs], axis=1).astype(BF16)
    wuq_p = jnp.pad(w_uq.reshape(MLA_Q_RANK, MLA_HEADS, MLA_NOPE + MLA_ROPE),
                    ((0, 0), (0, 0), (0, pad))).reshape(MLA_Q_RANK, MLA_HEADS * HEAD_SLOT).astype(BF16)
    wkv = w_ukv.reshape(MLA_KV_RANK, MLA_HEADS, MLA_NOPE + MLA_V)
    wk_p = jnp.pad(wkv[:, :, :MLA_NOPE], ((0, 0), (0, 0), (0, HEAD_SLOT - MLA_NOPE))
                   ).reshape(MLA_KV_RANK, MLA_HEADS * HEAD_SLOT).astype(BF16)
    wv_p = wkv[:, :, MLA_NOPE:].reshape(MLA_KV_RANK, MLA_WIDTH).astype(BF16)

    def row(n):
        return pl.BlockSpec((tm, n), lambda i: (i, 0))

    qk_w = MLA_HEADS * HEAD_SLOT
    lw = S5_L * S5_WIDTH
    folded = jax.ShapeDtypeStruct((t // S5_L, lw), BF16)
    folded_spec = pl.BlockSpec((tm // S5_L, lw), lambda i: (i, 0))
    outs = [jax.ShapeDtypeStruct((t, qk_w), BF16), jax.ShapeDtypeStruct((t, qk_w), BF16),
            jax.ShapeDtypeStruct((t, MLA_WIDTH), BF16), jax.ShapeDtypeStruct((t, MLA_WIDTH), BF16),
            folded, folded]
    return pl.pallas_call(
        _even_in_kernel,
        grid=(t // tm,),
        in_specs=[row(D_MODEL), pl.BlockSpec((tm // ROPE_PACK, ROPE_PACK), lambda i: (i, 0)),
                  _const_spec((1, 128)),
                  _const_spec((D_MODEL, EV_COLS)), _const_spec((1, MLA_Q_RANK)),
                  _const_spec((1, MLA_KV_RANK)), _const_spec((MLA_Q_RANK, qk_w)),
                  _const_spec((MLA_KV_RANK, qk_w)), _const_spec((MLA_KV_RANK, MLA_WIDTH))],
        out_specs=[row(qk_w), row(qk_w), row(MLA_WIDTH), row(MLA_WIDTH), folded_spec, folded_spec],
        out_shape=outs,
        scratch_shapes=[pltpu.VMEM((S5_HALVES, tm // ROW_SPLIT, S5_WIDTH // S5_HALVES), F32),
                        pltpu.VMEM((tm, HEAD_SLOT), F32), pltpu.VMEM((tm, HEAD_SLOT), F32)],
        compiler_params=_cparams(1),
        name="even_in_proj",
    )(xf, pos, invf, w_in_p, q_norm.reshape(1, -1), kv_norm.reshape(1, -1), wuq_p, wk_p, wv_p)


def _attn_kernel(q_ref, k_ref, v_ref, g_ref, o_ref):
    seq = q_ref.shape[1]
    t = ATTN_T
    n = seq // t
    nt = (((1,), (1,)), ((), ()))
    keep = lax.broadcasted_iota(jnp.int32, (t, t), 1) <= lax.broadcasted_iota(jnp.int32, (t, t), 0)
    lane_head = lax.broadcasted_iota(jnp.int32, (t, ATTN_STEP_HEADS * MLA_V), 1) // MLA_V

    def scores(qi, c):
        rows = slice(qi * t, (qi + 1) * t)
        lanes = slice(c * HEAD_SLOT, (c + 1) * HEAD_SLOT)
        q = q_ref[0, rows, lanes]
        sd = lax.dot_general(q, k_ref[0, rows, lanes], nt, preferred_element_type=F32)
        sd = jnp.where(keep, sd, NEG_BIG)
        if qi == 0:
            return sd
        s = lax.dot_general(q, k_ref[0, :qi * t, lanes], nt, preferred_element_type=F32)
        return jnp.concatenate([s, sd], axis=1)

    def attend(qi, ss):
        rows = slice(qi * t, (qi + 1) * t)
        o = None
        for c, s in enumerate(ss):
            slabs = [s[:, j:j + 128] for j in range(0, s.shape[1], 128)]
            m = functools.reduce(jnp.maximum, slabs)
            m = jnp.max(m, axis=1, keepdims=True)
            lsum = jnp.zeros((t, 128), F32)
            ps = []
            for sl in slabs:
                pj = jnp.exp2(sl - m)
                lsum = lsum + pj
                ps.append(pj.astype(BF16))
            l = jnp.sum(lsum, axis=1, keepdims=True)
            p = jnp.concatenate(ps, axis=1)
            acc = jnp.dot(p, v_ref[0, :(qi + 1) * t, :], preferred_element_type=F32)
            oc = acc * (1.0 / l)
            o = oc if o is None else jnp.where(lane_head == c, oc, o)
        o_ref[0, rows, :] = (o * g_ref[0, rows, :].astype(F32)).astype(BF16)

    pending = None
    for qi in reversed(range(n)):
        ss = [scores(qi, c) for c in range(ATTN_STEP_HEADS)]
        if pending is not None:
            attend(*pending)
        pending = (qi, ss)
    attend(*pending)


def _attention(q, k, v, g, batch, seq):
    q = q.reshape(batch, seq, -1)
    k = k.reshape(batch, seq, -1)
    v = v.reshape(batch, seq, -1)
    g = g.reshape(batch, seq, -1)
    qk = pl.BlockSpec((1, seq, ATTN_STEP_HEADS * HEAD_SLOT), lambda b, st: (b, 0, st))
    vo = pl.BlockSpec((1, seq, ATTN_STEP_HEADS * MLA_V), lambda b, st: (b, 0, st))
    o = pl.pallas_call(
        _attn_kernel,
        grid=(batch, MLA_HEADS // ATTN_STEP_HEADS),
        in_specs=[qk, qk, vo, vo],
        out_specs=vo,
        out_shape=jax.ShapeDtypeStruct((batch, seq, MLA_WIDTH), BF16),
        compiler_params=_cparams(2),
        name="mla_attention",
    )(q, k, v, g)
    return o.reshape(batch * seq, MLA_WIDTH)


def _s5_matrices(a_re, a_im, log_dt, b_re, b_im, c_re, c_im):
    L, G, P, H = S5_L, S5_GROUPS, S5_STATE, S5_GROUP
    dt = jnp.exp(log_dt.astype(F32))[:, None]
    ar, ai = a_re.astype(F32), a_im.astype(F32)
    mag = jnp.exp(ar * dt)
    lr, li = mag * jnp.cos(ai * dt), mag * jnp.sin(ai * dt)
    den = ar * ar + ai * ai
    nr, ni = lr - 1.0, li
    zr = (nr * ar + ni * ai) / den
    zi = (ni * ar - nr * ai) / den
    br, bi = b_re.astype(F32), b_im.astype(F32)
    bbr = zr[..., None] * br - zi[..., None] * bi
    bbi = zr[..., None] * bi + zi[..., None] * br
    cr, ci = c_re.astype(F32), c_im.astype(F32)

    def powers(j):
        j = j.astype(F32)[:, None, None]
        pmag = jnp.exp(j * (ar * dt))
        return pmag * jnp.cos(j * (ai * dt)), pmag * jnp.sin(j * (ai * dt))

    gp, w = G * P, G * H
    pr, pi = powers(jnp.arange(L + 1))
    prow = jnp.concatenate([pr.reshape(L + 1, gp), pi.reshape(L + 1, gp)], axis=0)
    prow3 = prow.reshape(2 * (L + 1), 1, gp)
    pcol3 = prow.reshape(2 * (L + 1), gp, 1)

    same_b = (jnp.arange(w)[:, None] // H) == (jnp.arange(gp)[None, :] // P)
    same_c = (jnp.arange(gp)[:, None] // P) == (jnp.arange(w)[None, :] // H)
    bd_br = jnp.where(same_b, jnp.tile(bbr.transpose(2, 0, 1).reshape(H, gp), (G, 1)), 0.0)
    bd_bi = jnp.where(same_b, jnp.tile(bbi.transpose(2, 0, 1).reshape(H, gp), (G, 1)), 0.0)
    bd_cr = jnp.where(same_c, jnp.tile(cr.transpose(0, 2, 1).reshape(gp, H), (1, G)), 0.0)
    bd_ci = jnp.where(same_c, jnp.tile(ci.transpose(0, 2, 1).reshape(gp, H), (1, G)), 0.0)

    nh, hw = S5_HALVES, S5_WIDTH // S5_HALVES
    bmat, tmat, cmat = pl.pallas_call(
        _s5_prepare_kernel,
        grid=(L,),
        in_specs=[_const_spec((w, gp)), _const_spec((w, gp)), _const_spec((gp, w)), _const_spec((gp, w)),
                  _const_spec((2 * (L + 1), gp)),
                  pl.BlockSpec((1, 1, gp), lambda i: (L - 1 - i, 0, 0)),
                  pl.BlockSpec((1, 1, gp), lambda i: (2 * L - i, 0, 0)),
                  pl.BlockSpec((1, gp, 1), lambda i: (i + 1, 0, 0)),
                  pl.BlockSpec((1, gp, 1), lambda i: (L + 2 + i, 0, 0))],
        out_specs=[pl.BlockSpec((nh, hw, 2 * gp // nh), lambda i: (0, i, 0)),
                   pl.BlockSpec((nh, hw, L * hw), lambda i: (0, i, 0)),
                   pl.BlockSpec((nh, 2 * gp // nh, hw), lambda i: (0, 0, i))],
        out_shape=[jax.ShapeDtypeStruct((nh, L * hw, 2 * gp // nh), BF16),
                   jax.ShapeDtypeStruct((nh, L * hw, L * hw), BF16),
                   jax.ShapeDtypeStruct((nh, 2 * gp // nh, L * hw), BF16)],
        scratch_shapes=[pltpu.VMEM((L, w, w), F32)],
        compiler_params=_cparams(1),
        name="s5_prepare",
    )(bd_br, bd_bi, bd_cr, bd_ci, prow, prow3, prow3, pcol3, pcol3)
    lam = jnp.stack([pr[L].reshape(gp), pi[L].reshape(gp)])
    return bmat, tmat, cmat, lam


def _s5_prepare_kernel(br_ref, bi_ref, cr_ref, ci_ref, prow_ref, rr_ref, ri_ref, cr_pow_ref, ci_pow_ref,
                       bmat_ref, tmat_ref, cmat_ref, k_scr):
    L = S5_L
    hw = S5_WIDTH // S5_HALVES
    hgp = S5_GROUPS * S5_STATE // S5_HALVES
    i = pl.program_id(0)
    br, bi = br_ref[...], bi_ref[...]
    cr, ci = cr_ref[...], ci_ref[...]

    @pl.when(i == 0)
    def _():
        for j in range(L):
            pr = prow_ref[j:j + 1, :]
            pi = prow_ref[L + 1 + j:L + 2 + j, :]
            k_scr[j] = (jnp.dot(br * pr - bi * pi, cr, precision=lax.Precision.HIGHEST,
                                preferred_element_type=F32)
                        - jnp.dot(bi * pr + br * pi, ci, precision=lax.Precision.HIGHEST,
                                  preferred_element_type=F32))

    pr, pi = rr_ref[0], ri_ref[0]
    pcr, pci = cr_pow_ref[0], ci_pow_ref[0]
    pb_r, pb_i = br * pr - bi * pi, bi * pr + br * pi
    g_r, g_i = cr * pcr - ci * pci, -(cr * pci + ci * pcr)
    lags = [k_scr[jnp.maximum(t - i, 0)] for t in range(L)]
    for h in range(S5_HALVES):
        ch = slice(h * hw, (h + 1) * hw)
        st = slice(h * hgp, (h + 1) * hgp)
        bmat_ref[h, :, :hgp] = pb_r[ch, st].astype(BF16)
        bmat_ref[h, :, hgp:] = pb_i[ch, st].astype(BF16)
        for t in range(L):
            tmat_ref[h, :, t * hw:(t + 1) * hw] = jnp.where(t >= i, lags[t][ch, ch], 0.0).astype(BF16)
        cmat_ref[h, :hgp, :] = g_r[st, ch].astype(BF16)
        cmat_ref[h, hgp:, :] = g_i[st, ch].astype(BF16)


def _gelu_tanh(y):
    return 0.5 * y * (1.0 + jnp.tanh(math.sqrt(2.0 / math.pi) * (y + 0.044715 * (y * y * y))))


def _s5_kernel(u_ref, gs_ref, bmat_ref, tmat_ref, cmat_ref, lam_ref, dskip_ref, wglu_ref, bglu_ref,
               o_ref, x_scr, h_scr):
    nh = S5_HALVES
    hw = S5_WIDTH // nh
    gp = S5_GROUPS * S5_STATE
    hgp = gp // nh
    hl = S5_L * hw
    rows = u_ref.shape[0]
    per_batch = rows // S5_NB
    for h in range(nh):
        xh = jnp.dot(u_ref[:, h * hl:(h + 1) * hl], bmat_ref[h], preferred_element_type=F32)
        x_scr[:, h * hgp:(h + 1) * hgp] = xh[:, :hgp]
        x_scr[:, gp + h * hgp:gp + (h + 1) * hgp] = xh[:, hgp:]
    lam_r = lam_ref[0:1, :]
    lam_i = lam_ref[1:2, :]

    def block(i, hs):
        out = []
        for b in range(S5_NB):
            hr, hi = hs[b]
            r0 = pl.multiple_of(b * per_batch + i * 8, 8)
            x8 = x_scr[pl.ds(r0, 8), :]
            starts_r, starts_i = [], []
            for jj in range(8):
                starts_r.append(hr)
                starts_i.append(hi)
                xr = x8[jj:jj + 1, :gp]
                xi = x8[jj:jj + 1, gp:]
                hr, hi = lam_r * hr - lam_i * hi + xr, lam_r * hi + lam_i * hr + xi
            h_scr[pl.ds(r0, 8), :gp] = jnp.concatenate(starts_r, axis=0)
            h_scr[pl.ds(r0, 8), gp:] = jnp.concatenate(starts_i, axis=0)
            out.append((hr, hi))
        return tuple(out)

    zero = jnp.zeros((1, gp), F32)
    lax.fori_loop(0, per_batch // 8, block, tuple((zero, zero) for _ in range(S5_NB)))

    hs = h_scr[...].astype(BF16)
    hs = [jnp.concatenate([hs[:, h * hgp:(h + 1) * hgp], hs[:, gp + h * hgp:gp + (h + 1) * hgp]], axis=1)
          for h in range(nh)]

    pair = 2 * hw
    ypairs = [[jnp.dot(u_ref[:, h * hl:h * hl + (tp + 1) * pair],
                       tmat_ref[h, :(tp + 1) * pair, tp * pair:(tp + 1) * pair], preferred_element_type=F32)
               + jnp.dot(hs[h], cmat_ref[h, :, tp * pair:(tp + 1) * pair], preferred_element_type=F32)
               for h in range(nh)] for tp in range(S5_L // 2)]

    def token(ref, t):
        return jnp.concatenate([ref[:, h * hl + t * hw:h * hl + (t + 1) * hw] for h in range(nh)], axis=1)

    ys = []
    for t in range(S5_L):
        e = (t % 2) * hw
        y = jnp.concatenate([yh[:, e:e + hw] for yh in ypairs[t // 2]], axis=1)
        ys.append(_gelu_tanh(y + dskip_ref[...] * token(u_ref, t).astype(F32)))
    zs = [jnp.dot(y.astype(BF16), wglu_ref[...], preferred_element_type=F32) for y in ys]
    for t, (y, z) in enumerate(zip(ys, zs)):
        o = (y * _sigmoid(z + bglu_ref[...])) * token(gs_ref, t).astype(F32)
        for h in range(nh):
            o_ref[h, pl.ds(t, rows, stride=S5_L), :] = o[:, h * hw:(h + 1) * hw]


def _s5(u, gs, batch, seq, a_re, a_im, log_dt, b_re, b_im, c_re, c_im, d_skip, w_glu, b_glu):
    t = batch * seq
    lw = S5_L * S5_WIDTH
    gp2 = 2 * S5_GROUPS * S5_STATE
    bmat, tmat, cmat, lam = _s5_matrices(a_re, a_im, log_dt, b_re, b_im, c_re, c_im)
    rows = S5_NB * seq // S5_L
    halves = S5_HALVES
    blk = pl.BlockSpec((rows, lw), lambda i: (i, 0))
    return pl.pallas_call(
        _s5_kernel,
        grid=(batch // S5_NB,),
        in_specs=[blk, blk, _const_spec(bmat.shape, True), _const_spec(tmat.shape, True),
                  _const_spec(cmat.shape, True), _const_spec((2, gp2 // 2)),
                  _const_spec((1, S5_WIDTH)), _const_spec((S5_WIDTH, S5_WIDTH)), _const_spec((1, S5_WIDTH))],
        out_specs=pl.BlockSpec((halves, rows * S5_L, 128), lambda i: (0, i, 0)),
        out_shape=jax.ShapeDtypeStruct((halves, t, 128), F32),
        scratch_shapes=[pltpu.VMEM((rows, gp2), F32), pltpu.VMEM((rows, gp2), F32)],
        compiler_params=_cparams(1),
        name="s5_mixer",
    )(u, gs, bmat, tmat, cmat, lam, d_skip.reshape(1, -1).astype(F32), w_glu.astype(BF16),
      b_glu.reshape(1, -1).astype(F32))


def _out_ln_kernel(n_act, then_odd_in, x_ref, *refs):
    acts = refs[:n_act]
    ws = refs[n_act:2 * n_act]
    g_ref, b_ref = refs[2 * n_act:2 * n_act + 2]
    rest = refs[2 * n_act + 2:]
    o_ref = rest[3] if then_odd_in else rest[0]

    def operand(ref, rows):
        if len(ref.shape) == 2:
            return ref[rows, :]
        return jnp.concatenate([ref[i, rows, :] for i in range(ref.shape[0])], axis=1).astype(BF16)

    tm = x_ref.shape[0]
    nblk = ROW_SPLIT if then_odd_in else 1
    blocks = [slice(r * tm // nblk, (r + 1) * tm // nblk) for r in range(nblk)]
    xns = []
    for rows in blocks:
        y = jnp.dot(operand(acts[0], rows), ws[0][...], preferred_element_type=F32)
        for a, w in zip(acts[1:], ws[1:]):
            y = y + jnp.dot(operand(a, rows), w[...], preferred_element_type=F32)
        z = DEEPNORM_ALPHA * x_ref[rows, :] + y
        mu = jnp.mean(z, axis=-1, keepdims=True)
        zc = z - mu
        var = jnp.mean(zc * zc, axis=-1, keepdims=True)
        xn = zc * lax.rsqrt(var + LN_EPS) * g_ref[...] + b_ref[...]
        o_ref[rows, :] = xn
        xns.append(xn.astype(BF16))
    if then_odd_in:
        for rows, xb in zip(blocks, xns):
            _odd_in_body(xb, rows, *rest[:3], *rest[4:])


def _out_ln(xf, acts, ws, ln_g, ln_b, odd_in=None):
    t = xf.shape[0]
    tm = TOKEN_TILE
    n = len(acts)

    def row(c):
        return pl.BlockSpec((tm, c), lambda i: (i, 0))

    def act_spec(a):
        if a.ndim == 2:
            return row(a.shape[1])
        return pl.BlockSpec((a.shape[0], tm, a.shape[2]), lambda i: (0, i, 0))

    in_specs = ([row(D_MODEL)] + [act_spec(a) for a in acts] + [_const_spec(w.shape) for w in ws]
                + [_const_spec((1, D_MODEL)), _const_spec((1, D_MODEL))])
    args = [xf, *acts, *ws, ln_g.reshape(1, -1), ln_b.reshape(1, -1)]
    out_specs = [row(D_MODEL)]
    out_shape = [jax.ShapeDtypeStruct((t, D_MODEL), F32)]
    if odd_in is not None:
        odd_args, odd_specs, odd_out_specs, odd_out_shape = _odd_in_operands(t, tm, *odd_in)
        in_specs += odd_specs
        args += odd_args
        out_specs += odd_out_specs
        out_shape += odd_out_shape
    outs = pl.pallas_call(
        functools.partial(_out_ln_kernel, n, odd_in is not None),
        grid=(t // tm,),
        in_specs=in_specs,
        out_specs=out_specs,
        out_shape=out_shape,
        compiler_params=_cparams(1),
        name="out_proj_layernorm" if odd_in is None else "out_proj_layernorm_odd_in_proj",
    )(*args)
    return outs[0] if odd_in is None else outs


OD_Q = 0
OD_K = OD_Q + GLA_KEY_WIDTH
OD_V = OD_K + GLA_KEY_WIDTH
OD_GL = OD_V + GLA_WIDTH
OD_G = OD_GL + 128
OD_COLS = OD_G + GLA_WIDTH


def _odd_in_kernel(x_ref, *refs):
    _odd_in_body(x_ref[...].astype(BF16), slice(None), *refs)


def _odd_in_body(xb, rows, w_ref, wgk_ref, bgk_ref, q_ref, k_ref, v_ref, la_ref, g_ref):
    def proj(lo, hi):
        return jnp.dot(xb, w_ref[:, lo:hi], preferred_element_type=F32)

    z = jnp.dot(proj(OD_GL, OD_G).astype(BF16), wgk_ref[...], preferred_element_type=F32) + bgk_ref[...]
    g = proj(OD_G, OD_COLS)
    log_sig = -(jnp.maximum(-z, 0.0) + jnp.log1p(jnp.exp(-jnp.abs(z))))
    la_ref[rows, :] = log_sig / GLA_GATE_NORM
    v_ref[rows, :] = proj(OD_V, OD_GL).astype(BF16)
    g_ref[rows, :] = _silu(g).astype(BF16)
    q_ref[rows, :] = proj(OD_Q, OD_K).astype(BF16)
    k_ref[rows, :] = proj(OD_K, OD_V).astype(BF16)


def _odd_in_operands(t, tm, w_in, w_gk2, b_gk):
    q, k, v, gl, g = jnp.split(w_in, [OD_K, OD_V, OD_GL, OD_GL + GLA_GATE_RANK], axis=1)
    gl = jnp.pad(gl, ((0, 0), (0, 128 - GLA_GATE_RANK)))
    w_p = jnp.concatenate([q, k, v, gl, g], axis=1).astype(BF16)
    wgk_p = jnp.pad(w_gk2, ((0, 128 - GLA_GATE_RANK), (0, 0))).astype(BF16)

    def row(c):
        return pl.BlockSpec((tm, c), lambda i: (i, 0))

    args = [w_p, wgk_p, b_gk.reshape(1, -1).astype(F32)]
    specs = [_const_spec((D_MODEL, OD_COLS)), _const_spec((128, GLA_KEY_WIDTH)), _const_spec((1, GLA_KEY_WIDTH))]
    out_specs = [row(GLA_KEY_WIDTH), row(GLA_KEY_WIDTH), row(GLA_WIDTH), row(GLA_KEY_WIDTH), row(GLA_WIDTH)]
    out_shape = [jax.ShapeDtypeStruct((t, GLA_KEY_WIDTH), BF16), jax.ShapeDtypeStruct((t, GLA_KEY_WIDTH), BF16),
                 jax.ShapeDtypeStruct((t, GLA_WIDTH), BF16), jax.ShapeDtypeStruct((t, GLA_KEY_WIDTH), F32),
                 jax.ShapeDtypeStruct((t, GLA_WIDTH), BF16)]
    return args, specs, out_specs, out_shape


def _odd_in(xf, w_in, w_gk2, b_gk):
    t = xf.shape[0]
    tm = TOKEN_TILE
    args, specs, out_specs, out_shape = _odd_in_operands(t, tm, w_in, w_gk2, b_gk)
    return pl.pallas_call(
        _odd_in_kernel,
        grid=(t // tm,),
        in_specs=[pl.BlockSpec((tm, D_MODEL), lambda i: (i, 0))] + specs,
        out_specs=out_specs,
        out_shape=out_shape,
        compiler_params=_cparams(1),
        name="odd_in_proj",
    )(xf, *args)


def _gla_kernel(q_ref, k_ref, v_ref, la_ref, g_ref, gn_ref, o_ref, bc_scr):
    c, sb = GLA_CHUNK, GLA_SUPER
    seq = q_ref.shape[1]
    ri = lax.broadcasted_iota(jnp.int32, (sb, sb), 0)
    ci = lax.broadcasted_iota(jnp.int32, (sb, sb), 1)
    valid = jnp.logical_and(ci <= ri, ci >= (ri // c) * c)
    tri = jnp.where(valid, 1.0, 0.0).astype(BF16)
    blocks = [slice(s * sb, (s + 1) * sb) for s in range(seq // sb)]
    chunks = [slice(n * c, (n + 1) * c) for n in range(seq // c)]
    heads = [_gla_head(hh, q_ref, k_ref, v_ref, la_ref, g_ref, gn_ref, o_ref, bc_scr, blocks, chunks, tri, valid)
             for hh in range(GLA_STEP_HEADS)]
    live = list(enumerate(heads))
    tick = 0
    while live:
        for i, gen in list(live):
            if tick >= i * GLA_STAGE_SKEW and next(gen, StopIteration) is StopIteration:
                live.remove((i, gen))
        tick += 1


def _gla_head(hh, q_ref, k_ref, v_ref, la_ref, g_ref, gn_ref, o_ref, bc_scr, blocks, chunks, tri, valid):
    c, sb = GLA_CHUNK, GLA_SUPER
    cps = sb // c
    nch = len(chunks)
    nt = (((1,), (1,)), ((), ()))
    tn = (((0,), (0,)), ((), ()))
    scale = GLA_DK ** -0.5
    kl = slice(hh * GLA_DK, (hh + 1) * GLA_DK)
    vl = slice(hh * GLA_DV, (hh + 1) * GLA_DV)

    bcs = []
    for rows in blocks:
        ga = la_ref[0, rows, kl]
        ga_hi = ga.astype(BF16)
        ga_lo = (ga - ga_hi.astype(F32)).astype(BF16)
        cs = jnp.dot(tri, jnp.concatenate([ga_hi, ga_lo], axis=1), preferred_element_type=F32)
        bc = cs[:, :GLA_DK] + cs[:, GLA_DK:]
        bc_scr[hh, rows, :] = bc
        bcs.append(bc.reshape(cps, c, GLA_DK))
    yield

    qds, k_invs, k_ends = [], [], []
    for rows, bc3 in zip(blocks, bcs):
        bl3 = bc3[:, c - 1:c, :]
        q3 = q_ref[0, rows, kl].astype(F32).reshape(cps, c, GLA_DK) * scale
        k3 = k_ref[0, rows, kl].astype(F32).reshape(cps, c, GLA_DK)
        qds.append((q3 * jnp.exp(bc3)).reshape(sb, GLA_DK).astype(BF16))
        k_invs.append((k3 * jnp.exp(-bc3)).reshape(sb, GLA_DK).astype(BF16))
        k_ends.append((k3 * jnp.exp(bl3 - bc3)).reshape(sb, GLA_DK).astype(BF16))
    yield

    atts = [lax.dot_general(qd, k_inv, nt, preferred_element_type=F32) for qd, k_inv in zip(qds, k_invs)]
    yield
    atts = [jnp.where(valid, att, 0.0).astype(BF16) for att in atts]
    o_intras = [jnp.dot(att, v_ref[0, rows, vl], preferred_element_type=F32)
                for att, rows in zip(atts, blocks)]
    yield

    def chunk_of(vals, n):
        s, j = divmod(n, cps)
        return vals[s][j * c:(j + 1) * c]

    kvs = [lax.dot_general(chunk_of(k_ends, n), v_ref[0, chunks[n], vl], tn, preferred_element_type=F32)
           for n in range(nch - 1)]
    yield

    b_last = bc_scr[hh, pl.ds(c - 1, nch, stride=c), :]
    dec = jnp.concatenate([jnp.exp(b_last), jnp.zeros((GLA_DK - nch, GLA_DK), F32)], axis=0)
    dec_t = dec.T

    st = jnp.zeros((GLA_DK, GLA_DV), F32)
    states = [st.astype(BF16)]
    for n in range(nch - 1):
        st = st * dec_t[:, n:n + 1] + kvs[n]
        states.append(st.astype(BF16))
    yield

    o_inters = [jnp.dot(chunk_of(qds, n), states[n], preferred_element_type=F32) for n in range(nch)]
    yield
    for n in range(nch):
        o = chunk_of(o_intras, n) + o_inters[n]
        o = _rms(o, gn_ref[...]) * g_ref[0, chunks[n], vl].astype(F32)
        o_ref[0, chunks[n], vl] = o.astype(BF16)


def _gla(q, k, v, la, g, g_norm, batch, seq):
    q = q.reshape(batch, seq, -1)
    k = k.reshape(batch, seq, -1)
    v = v.reshape(batch, seq, -1)
    la = la.reshape(batch, seq, -1)
    g = g.reshape(batch, seq, -1)

    def spec(c):
        return pl.BlockSpec((1, seq, GLA_STEP_HEADS * c), lambda b, h: (b, 0, h))

    o = pl.pallas_call(
        _gla_kernel,
        grid=(batch, GLA_HEADS // GLA_STEP_HEADS),
        in_specs=[spec(GLA_DK), spec(GLA_DK), spec(GLA_DV), spec(GLA_DK), spec(GLA_DV),
                  _const_spec((1, GLA_DV))],
        out_specs=spec(GLA_DV),
        out_shape=jax.ShapeDtypeStruct((batch, seq, GLA_WIDTH), BF16),
        scratch_shapes=[pltpu.VMEM((GLA_STEP_HEADS, seq, GLA_DK), F32)],
        compiler_params=_cparams(2),
        name="gla_mixer",
    )(q, k, v, la, g, g_norm.reshape(1, -1).astype(F32))
    return o.reshape(batch * seq, GLA_WIDTH)


def kernel(x, positions, ln_g, ln_b, even_w_in, mla_q_norm, mla_kv_norm, mla_w_uq, mla_w_ukv, s5_a_re, s5_a_im, s5_log_dt, s5_b_re, s5_b_im, s5_c_re, s5_c_im, s5_d, s5_w_glu, s5_b_glu, even_w_out, odd_w_in, gla_w_gk2, gla_b_gk, gla_g_norm, odd_w_out):
    batch, seq, _ = x.shape
    xf = x.reshape(batch * seq, D_MODEL)
    odd_inputs = None
    for layer in range(DEPTH):
        j = layer // 2
        if layer % 2 == 0:
            q, k, v, gm, u, gs = _even_in(xf, positions, even_w_in[j], mla_q_norm[j],
                                          mla_kv_norm[j], mla_w_uq[j], mla_w_ukv[j])
            o_mla = _attention(q, k, v, gm, batch, seq)
            o_s5 = _s5(u, gs, batch, seq, s5_a_re[j], s5_a_im[j], s5_log_dt[j], s5_b_re[j], s5_b_im[j],
                       s5_c_re[j], s5_c_im[j], s5_d[j], s5_w_glu[j], s5_b_glu[j])
            w_out = even_w_out[j].astype(BF16)
            nxt = (odd_w_in[j], gla_w_gk2[j], gla_b_gk[j]) if layer + 1 < DEPTH else None
            res = _out_ln(xf, [o_mla, o_s5], [w_out[:MLA_WIDTH], w_out[MLA_WIDTH:]], ln_g[layer], ln_b[layer],
                          odd_in=nxt)
            xf, odd_inputs = (res, None) if nxt is None else (res[0], res[1:])
        else:
            if odd_inputs is None:
                odd_inputs = _odd_in(xf, odd_w_in[j], gla_w_gk2[j], gla_b_gk[j])
            q, k, v, la, g = odd_inputs
            odd_inputs = None
            o = _gla(q, k, v, la, g, gla_g_norm[j], batch, seq)
            xf = _out_ln(xf, [o], [odd_w_out[j].astype(BF16)], ln_g[layer], ln_b[layer])
    return xf.reshape(batch, seq, D_MODEL)
```

```python
import functools
import itertools
import math

import jax
import jax.numpy as jnp
from jax import lax
from jax.experimental import pallas as pl
from jax.experimental.pallas import tpu as pltpu

F32 = jnp.float32
BF16 = jnp.bfloat16

D_MODEL = 1024
DEPTH = 2

MLA_HEADS = 8
MLA_NOPE = 64
MLA_ROPE = 32
MLA_V = 64
MLA_Q_RANK = 256
MLA_KV_RANK = 128
MLA_WIDTH = MLA_HEADS * MLA_V
ROPE_BASE = 10000.0
HEAD_SLOT = 128

S5_WIDTH = 256
S5_GROUP = 16
S5_GROUPS = S5_WIDTH // S5_GROUP
S5_STATE = 64
S5_L = 8
S5_NB = 2
S5_HALVES = 2

GLA_HEADS = 4
GLA_KEY_WIDTH = D_MODEL // 2
GLA_WIDTH = D_MODEL
GLA_DK = GLA_KEY_WIDTH // GLA_HEADS
GLA_DV = GLA_WIDTH // GLA_HEADS
GLA_GATE_RANK = 16
GLA_GATE_NORM = 16.0
GLA_CHUNK = 64
GLA_SUPER = 256
GLA_STEP_HEADS = 2
GLA_STAGE_SKEW = 3

DEEPNORM_ALPHA = (2 * DEPTH) ** 0.25
LN_EPS = 1e-5
RMS_EPS = 1e-6

TOKEN_TILE = 1024
ROW_SPLIT = 4
ATTN_T = 256
ATTN_STEP_HEADS = 2
NEG_BIG = -1e30
VMEM_LIMIT = 56 * 1024 * 1024


def _cparams(n_axes):
    return pltpu.CompilerParams(dimension_semantics=("arbitrary",) * n_axes,
                                vmem_limit_bytes=VMEM_LIMIT)


def _const_spec(shape, single=False):
    nd = len(shape)
    if single:
        return pl.BlockSpec(shape, lambda *_: (0,) * nd, pipeline_mode=pl.Buffered(1))
    return pl.BlockSpec(shape, lambda *_: (0,) * nd)


def _sigmoid(x):
    return 1.0 / (1.0 + jnp.exp(-x))


def _silu(x):
    return x * _sigmoid(x)


def _rms(x, g):
    return (x * lax.rsqrt(jnp.mean(x * x, axis=-1, keepdims=True) + RMS_EPS)) * g


ROPE_PACK = 128 // (MLA_ROPE // 2)


def _rope_tables(pos_ref, invf_ref, cos_ref, sin_ref):
    half = MLA_ROPE // 2
    rb = pos_ref.shape[0]
    pos = pos_ref[...].astype(F32)
    lane = lax.broadcasted_iota(jnp.int32, (rb, 128), 1)
    owner = lane // half
    posx = jnp.zeros((rb, 128), F32)
    for s in range(ROPE_PACK):
        posx = jnp.where(owner == s, pos[:, s:s + 1], posx)
    ang = posx * invf_ref[...]
    c = jnp.cos(ang)
    sn = jnp.sin(ang)
    first = (lane & 112) == MLA_NOPE
    second = (lane & 112) == MLA_NOPE + half
    for s in range(ROPE_PACK):
        k1 = (MLA_NOPE - half * s) % 128
        k2 = (MLA_NOPE + half - half * s) % 128
        cf = jnp.where(first, pltpu.roll(c, k1, 1), jnp.where(second, pltpu.roll(c, k2, 1), 1.0))
        sf = jnp.where(first, -pltpu.roll(sn, k1, 1), jnp.where(second, pltpu.roll(sn, k2, 1), 0.0))
        cos_ref[pl.ds(s, rb, stride=ROPE_PACK), :] = cf
        sin_ref[pl.ds(s, rb, stride=ROPE_PACK), :] = sf


EV_CQ = 0
EV_CKV = EV_CQ + MLA_Q_RANK
EV_KPE = EV_CKV + MLA_KV_RANK
EV_GM = EV_KPE + HEAD_SLOT
EV_U = EV_GM + MLA_WIDTH
EV_GS = EV_U + S5_WIDTH
EV_COLS = EV_GS + S5_WIDTH
Q_SCALE = (MLA_NOPE + MLA_ROPE) ** -0.5 * math.log2(math.e)


def _rotate_pairs(x):
    n = x.shape[-1]
    lane = lax.broadcasted_iota(jnp.int32, x.shape, 1)
    lo = (lane & 112) == 64
    return jnp.where(lo, pltpu.roll(x, n - MLA_ROPE // 2, 1), pltpu.roll(x, MLA_ROPE // 2, 1))


def _even_in_kernel(x_ref, pos_ref, invf_ref, w_in_ref, qn_ref, kvn_ref, wuq_ref, wk_ref, wv_ref,
                    q_ref, k_ref, v_ref, gm_ref, u_ref, gs_ref, fold_scr, cos_ref, sin_ref):
    _rope_tables(pos_ref, invf_ref, cos_ref, sin_ref)
    tm = x_ref.shape[0]
    nb = tm // ROW_SPLIT
    for r in range(ROW_SPLIT):
        rows = slice(r * nb, (r + 1) * nb)
        h = jnp.dot(x_ref[rows, :].astype(BF16), w_in_ref[...], preferred_element_type=F32)
        cos = cos_ref[rows, :]
        sin = sin_ref[rows, :]

        cqn = _rms(h[:, EV_CQ:EV_CKV], qn_ref[...]).astype(BF16)
        q = jnp.dot(cqn, wuq_ref[...], preferred_element_type=F32)
        q = q * jnp.tile(cos, (1, MLA_HEADS)) + _rotate_pairs(q) * jnp.tile(sin, (1, MLA_HEADS))
        q_ref[rows, :] = (q * Q_SCALE).astype(BF16)

        kpe = h[:, EV_KPE:EV_GM]
        kpe = kpe * cos + _rotate_pairs(kpe) * sin
        ckvn = _rms(h[:, EV_CKV:EV_KPE], kvn_ref[...]).astype(BF16)
        k = jnp.dot(ckvn, wk_ref[...], preferred_element_type=F32) + jnp.tile(kpe, (1, MLA_HEADS))
        k_ref[rows, :] = k.astype(BF16)
        v_ref[rows, :] = jnp.dot(ckvn, wv_ref[...], preferred_element_type=F32).astype(BF16)

        gm_ref[rows, :] = _silu(h[:, EV_GM:EV_U]).astype(BF16)
        fr = nb // S5_L
        folded = slice(r * fr, (r + 1) * fr)
        hw = S5_WIDTH // S5_HALVES
        for ref, val in ((u_ref, h[:, EV_U:EV_GS]), (gs_ref, _silu(h[:, EV_GS:EV_COLS]))):
            for half in range(S5_HALVES):
                fold_scr[half] = val[:, half * hw:(half + 1) * hw]
            for half in range(S5_HALVES):
                for s in range(S5_L):
                    lo = (half * S5_L + s) * hw
                    ref[folded, lo:lo + hw] = fold_scr[half, pl.ds(s, fr, stride=S5_L), :].astype(BF16)


def _even_in(xf, positions, w_in, q_norm, kv_norm, w_uq, w_ukv):
    t = xf.shape[0]
    tm = TOKEN_TILE
    pad = HEAD_SLOT - MLA_NOPE - MLA_ROPE
    half = MLA_ROPE // 2
    pos = positions.reshape(t // ROPE_PACK, ROPE_PACK)
    inv_freq = ROPE_BASE ** (-jnp.arange(half, dtype=F32) / half)
    invf = jnp.tile(inv_freq, ROPE_PACK).reshape(1, 128)
    cq, ckv, kr, gm, u, gs = jnp.split(
        w_in, [MLA_Q_RANK, MLA_Q_RANK + MLA_KV_RANK, MLA_Q_RANK + MLA_KV_RANK + MLA_ROPE,
               MLA_Q_RANK + MLA_KV_RANK + MLA_ROPE + MLA_WIDTH,
               MLA_Q_RANK + MLA_KV_RANK + MLA_ROPE + MLA_WIDTH + S5_WIDTH], axis=1)
    kr = jnp.pad(kr, ((0, 0), (MLA_NOPE, pad)))
    w_in_p = jnp.concatenate([cq, ckv, kr, gm, u, gs], axis=1).astype(BF16)
    wuq_p = jnp.pad(w_uq.reshape(MLA_Q_RANK, MLA_HEADS, MLA_NOPE + MLA_ROPE),
                    ((0, 0), (0, 0), (0, pad))).reshape(MLA_Q_RANK, MLA_HEADS * HEAD_SLOT).astype(BF16)
    wkv = w_ukv.reshape(MLA_KV_RANK, MLA_HEADS, MLA_NOPE + MLA_V)
    wk_p = jnp.pad(wkv[:, :, :MLA_NOPE], ((0, 0), (0, 0), (0, HEAD_SLOT - MLA_NOPE))
                   ).reshape(MLA_KV_RANK, MLA_HEADS * HEAD_SLOT).astype(BF16)
    wv_p = wkv[:, :, MLA_NOPE:].reshape(MLA_KV_RANK, MLA_WIDTH).astype(BF16)

    def row(n):
        return pl.BlockSpec((tm, n), lambda i: (i, 0))

    qk_w = MLA_HEADS * HEAD_SLOT
    lw = S5_L * S5_WIDTH
    folded = jax.ShapeDtypeStruct((t // S5_L, lw), BF16)
    folded_spec = pl.BlockSpec((tm // S5_L, lw), lambda i: (i, 0))
    outs = [jax.ShapeDtypeStruct((t, qk_w), BF16), jax.ShapeDtypeStruct((t, qk_w), BF16),
            jax.ShapeDtypeStruct((t, MLA_WIDTH), BF16), jax.ShapeDtypeStruct((t, MLA_WIDTH), BF16),
            folded, folded]
    return pl.pallas_call(
        _even_in_kernel,
        grid=(t // tm,),
        in_specs=[row(D_MODEL), pl.BlockSpec((tm // ROPE_PACK, ROPE_PACK), lambda i: (i, 0)),
                  _const_spec((1, 128)),
                  _const_spec((D_MODEL, EV_COLS)), _const_spec((1, MLA_Q_RANK)),
                  _const_spec((1, MLA_KV_RANK)), _const_spec((MLA_Q_RANK, qk_w)),
                  _const_spec((MLA_KV_RANK, qk_w)), _const_spec((MLA_KV_RANK, MLA_WIDTH))],
        out_specs=[row(qk_w), row(qk_w), row(MLA_WIDTH), row(MLA_WIDTH), folded_spec, folded_spec],
        out_shape=outs,
        scratch_shapes=[pltpu.VMEM((S5_HALVES, tm // ROW_SPLIT, S5_WIDTH // S5_HALVES), F32),
                        pltpu.VMEM((tm, HEAD_SLOT), F32), pltpu.VMEM((tm, HEAD_SLOT), F32)],
        compiler_params=_cparams(1),
        name="even_in_proj",
    )(xf, pos, invf, w_in_p, q_norm.reshape(1, -1), kv_norm.reshape(1, -1), wuq_p, wk_p, wv_p)


def _attn_kernel(q_ref, k_ref, v_ref, g_ref, o_ref):
    seq = q_ref.shape[1]
    t = ATTN_T
    n = seq // t
    nt = (((1,), (1,)), ((), ()))
    keep = lax.broadcasted_iota(jnp.int32, (t, t), 1) <= lax.broadcasted_iota(jnp.int32, (t, t), 0)
    lane_head = lax.broadcasted_iota(jnp.int32, (t, ATTN_STEP_HEADS * MLA_V), 1) // MLA_V

    def scores(qi, c):
        rows = slice(qi * t, (qi + 1) * t)
        lanes = slice(c * HEAD_SLOT, (c + 1) * HEAD_SLOT)
        q = q_ref[0, rows, lanes]
        sd = lax.dot_general(q, k_ref[0, rows, lanes], nt, preferred_element_type=F32)
        sd = jnp.where(keep, sd, NEG_BIG)
        if qi == 0:
            return sd
        s = lax.dot_general(q, k_ref[0, :qi * t, lanes], nt, preferred_element_type=F32)
        return jnp.concatenate([s, sd], axis=1)

    def attend(qi, ss):
        rows = slice(qi * t, (qi + 1) * t)
        o = None
        for c, s in enumerate(ss):
            slabs = [s[:, j:j + 128] for j in range(0, s.shape[1], 128)]
            m = functools.reduce(jnp.maximum, slabs)
            m = jnp.max(m, axis=1, keepdims=True)
            lsum = jnp.zeros((t, 128), F32)
            ps = []
            for sl in slabs:
                pj = jnp.exp2(sl - m)
                lsum = lsum + pj
                ps.append(pj.astype(BF16))
            l = jnp.sum(lsum, axis=1, keepdims=True)
            p = jnp.concatenate(ps, axis=1)
            acc = jnp.dot(p, v_ref[0, :(qi + 1) * t, :], preferred_element_type=F32)
            oc = acc * (1.0 / l)
            o = oc if o is None else jnp.where(lane_head == c, oc, o)
        o_ref[0, rows, :] = (o * g_ref[0, rows, :].astype(F32)).astype(BF16)

    pending = None
    for qi in reversed(range(n)):
        ss = [scores(qi, c) for c in range(ATTN_STEP_HEADS)]
        if pending is not None:
            attend(*pending)
        pending = (qi, ss)
    attend(*pending)


def _attention(q, k, v, g, batch, seq):
    q = q.reshape(batch, seq, -1)
    k = k.reshape(batch, seq, -1)
    v = v.reshape(batch, seq, -1)
    g = g.reshape(batch, seq, -1)
    qk = pl.BlockSpec((1, seq, ATTN_STEP_HEADS * HEAD_SLOT), lambda b, st: (b, 0, st))
    vo = pl.BlockSpec((1, seq, ATTN_STEP_HEADS * MLA_V), lambda b, st: (b, 0, st))
    o = pl.pallas_call(
        _attn_kernel,
        grid=(batch, MLA_HEADS // ATTN_STEP_HEADS),
        in_specs=[qk, qk, vo, vo],
        out_specs=vo,
        out_shape=jax.ShapeDtypeStruct((batch, seq, MLA_WIDTH), BF16),
        compiler_params=_cparams(2),
        name="mla_attention",
    )(q, k, v, g)
    return o.reshape(batch * seq, MLA_WIDTH)


def _s5_matrices(a_re, a_im, log_dt, b_re, b_im, c_re, c_im):
    L, G, P, H = S5_L, S5_GROUPS, S5_STATE, S5_GROUP
    dt = jnp.exp(log_dt.astype(F32))[:, None]
    ar, ai = a_re.astype(F32), a_im.astype(F32)
    mag = jnp.exp(ar * dt)
    lr, li = mag * jnp.cos(ai * dt), mag * jnp.sin(ai * dt)
    den = ar * ar + ai * ai
    nr, ni = lr - 1.0, li
    zr = (nr * ar + ni * ai) / den
    zi = (ni * ar - nr * ai) / den
    br, bi = b_re.astype(F32), b_im.astype(F32)
    bbr = zr[..., None] * br - zi[..., None] * bi
    bbi = zr[..., None] * bi + zi[..., None] * br
    cr, ci = c_re.astype(F32), c_im.astype(F32)

    def powers(j):
        j = j.astype(F32)[:, None, None]
        pmag = jnp.exp(j * (ar * dt))
        return pmag * jnp.cos(j * (ai * dt)), pmag * jnp.sin(j * (ai * dt))

    gp, w = G * P, G * H
    pr, pi = powers(jnp.arange(L + 1))
    prow = jnp.concatenate([pr.reshape(L + 1, gp), pi.reshape(L + 1, gp)], axis=0)
    prow3 = prow.reshape(2 * (L + 1), 1, gp)
    pcol3 = prow.reshape(2 * (L + 1), gp, 1)

    same_b = (jnp.arange(w)[:, None] // H) == (jnp.arange(gp)[None, :] // P)
    same_c = (jnp.arange(gp)[:, None] // P) == (jnp.arange(w)[None, :] // H)
    bd_br = jnp.where(same_b, jnp.tile(bbr.transpose(2, 0, 1).reshape(H, gp), (G, 1)), 0.0)
    bd_bi = jnp.where(same_b, jnp.tile(bbi.transpose(2, 0, 1).reshape(H, gp), (G, 1)), 0.0)
    bd_cr = jnp.where(same_c, jnp.tile(cr.transpose(0, 2, 1).reshape(gp, H), (1, G)), 0.0)
    bd_ci = jnp.where(same_c, jnp.tile(ci.transpose(0, 2, 1).reshape(gp, H), (1, G)), 0.0)

    nh, hw = S5_HALVES, S5_WIDTH // S5_HALVES
    bmat, tmat, cmat = pl.pallas_call(
        _s5_prepare_kernel,
        grid=(L,),
        in_specs=[_const_spec((w, gp)), _const_spec((w, gp)), _const_spec((gp, w)), _const_spec((gp, w)),
                  _const_spec((2 * (L + 1), gp)),
                  pl.BlockSpec((1, 1, gp), lambda i: (L - 1 - i, 0, 0)),
                  pl.BlockSpec((1, 1, gp), lambda i: (2 * L - i, 0, 0)),
                  pl.BlockSpec((1, gp, 1), lambda i: (i + 1, 0, 0)),
                  pl.BlockSpec((1, gp, 1), lambda i: (L + 2 + i, 0, 0))],
        out_specs=[pl.BlockSpec((nh, hw, 2 * gp // nh), lambda i: (0, i, 0)),
                   pl.BlockSpec((nh, hw, L * hw), lambda i: (0, i, 0)),
                   pl.BlockSpec((nh, 2 * gp // nh, hw), lambda i: (0, 0, i))],
        out_shape=[jax.ShapeDtypeStruct((nh, L * hw, 2 * gp // nh), BF16),
                   jax.ShapeDtypeStruct((nh, L * hw, L * hw), BF16),
                   jax.ShapeDtypeStruct((nh, 2 * gp // nh, L * hw), BF16)],
        scratch_shapes=[pltpu.VMEM((L, nh, hw, hw), F32)],
        compiler_params=_cparams(1),
        name="s5_prepare",
    )(bd_br, bd_bi, bd_cr, bd_ci, prow, prow3, prow3, pcol3, pcol3)
    lam = jnp.stack([pr[L].reshape(gp), pi[L].reshape(gp)])
    return bmat, tmat, cmat, lam


def _s5_prepare_kernel(br_ref, bi_ref, cr_ref, ci_ref, prow_ref, rr_ref, ri_ref, cr_pow_ref, ci_pow_ref,
                       bmat_ref, tmat_ref, cmat_ref, k_scr):
    L = S5_L
    hw = S5_WIDTH // S5_HALVES
    hgp = S5_GROUPS * S5_STATE // S5_HALVES
    i = pl.program_id(0)
    br, bi = br_ref[...], bi_ref[...]
    cr, ci = cr_ref[...], ci_ref[...]

    halves = [(slice(h * hw, (h + 1) * hw), slice(h * hgp, (h + 1) * hgp)) for h in range(S5_HALVES)]

    @pl.when(i == 0)
    def _():
        for j in range(L):
            pr = prow_ref[j:j + 1, :]
            pi = prow_ref[L + 1 + j:L + 2 + j, :]
            pb_r, pb_i = br * pr - bi * pi, bi * pr + br * pi
            for h, (ch, st) in enumerate(halves):
                k_scr[j, h] = (jnp.dot(pb_r[ch, st], cr[st, ch], precision=lax.Precision.HIGHEST,
                                       preferred_element_type=F32)
                               - jnp.dot(pb_i[ch, st], ci[st, ch], precision=lax.Precision.HIGHEST,
                                         preferred_element_type=F32))

    pr, pi = rr_ref[0], ri_ref[0]
    pcr, pci = cr_pow_ref[0], ci_pow_ref[0]
    pb_r, pb_i = br * pr - bi * pi, bi * pr + br * pi
    g_r, g_i = cr * pcr - ci * pci, -(cr * pci + ci * pcr)
    lags = [k_scr[jnp.maximum(t - i, 0)] for t in range(L)]
    for h, (ch, st) in enumerate(halves):
        bmat_ref[h, :, :hgp] = pb_r[ch, st].astype(BF16)
        bmat_ref[h, :, hgp:] = pb_i[ch, st].astype(BF16)
        for t in range(L):
            tmat_ref[h, :, t * hw:(t + 1) * hw] = jnp.where(t >= i, lags[t][h], 0.0).astype(BF16)
        cmat_ref[h, :hgp, :] = g_r[st, ch].astype(BF16)
        cmat_ref[h, hgp:, :] = g_i[st, ch].astype(BF16)


def _gelu_tanh(y):
    return 0.5 * y * (1.0 + jnp.tanh(math.sqrt(2.0 / math.pi) * (y + 0.044715 * (y * y * y))))


def _s5_kernel(u_ref, gs_ref, bmat_ref, tmat_ref, cmat_ref, lam_ref, dskip_ref, wglu_ref, bglu_ref,
               o_ref, x_scr, h_scr):
    nh = S5_HALVES
    hw = S5_WIDTH // nh
    gp = S5_GROUPS * S5_STATE
    hgp = gp // nh
    hl = S5_L * hw
    rows = u_ref.shape[0]
    per_batch = rows // S5_NB
    for h in range(nh):
        xh = jnp.dot(u_ref[:, h * hl:(h + 1) * hl], bmat_ref[h], preferred_element_type=F32)
        x_scr[:, h * hgp:(h + 1) * hgp] = xh[:, :hgp]
        x_scr[:, gp + h * hgp:gp + (h + 1) * hgp] = xh[:, hgp:]
    lam_r = lam_ref[0:1, :]
    lam_i = lam_ref[1:2, :]

    def block(i, hs):
        out = []
        for b in range(S5_NB):
            hr, hi = hs[b]
            r0 = pl.multiple_of(b * per_batch + i * 8, 8)
            x8 = x_scr[pl.ds(r0, 8), :]
            starts_r, starts_i = [], []
            for jj in range(8):
                starts_r.append(hr)
                starts_i.append(hi)
                xr = x8[jj:jj + 1, :gp]
                xi = x8[jj:jj + 1, gp:]
                hr, hi = lam_r * hr - lam_i * hi + xr, lam_r * hi + lam_i * hr + xi
            h_scr[pl.ds(r0, 8), :gp] = jnp.concatenate(starts_r, axis=0)
            h_scr[pl.ds(r0, 8), gp:] = jnp.concatenate(starts_i, axis=0)
            out.append((hr, hi))
        return tuple(out)

    zero = jnp.zeros((1, gp), F32)
    lax.fori_loop(0, per_batch // 8, block, tuple((zero, zero) for _ in range(S5_NB)))

    hs = h_scr[...].astype(BF16)
    hs = [jnp.concatenate([hs[:, h * hgp:(h + 1) * hgp], hs[:, gp + h * hgp:gp + (h + 1) * hgp]], axis=1)
          for h in range(nh)]

    pair = 2 * hw
    ypairs = [[jnp.dot(u_ref[:, h * hl:h * hl + (tp + 1) * pair],
                       tmat_ref[h, :(tp + 1) * pair, tp * pair:(tp + 1) * pair], preferred_element_type=F32)
               + jnp.dot(hs[h], cmat_ref[h, :, tp * pair:(tp + 1) * pair], preferred_element_type=F32)
               for h in range(nh)] for tp in range(S5_L // 2)]

    def token(ref, t):
        return jnp.concatenate([ref[:, h * hl + t * hw:h * hl + (t + 1) * hw] for h in range(nh)], axis=1)

    ys = []
    for t in range(S5_L):
        e = (t % 2) * hw
        y = jnp.concatenate([yh[:, e:e + hw] for yh in ypairs[t // 2]], axis=1)
        ys.append(_gelu_tanh(y + dskip_ref[...] * token(u_ref, t).astype(F32)))
    zs = [jnp.dot(y.astype(BF16), wglu_ref[...], preferred_element_type=F32) for y in ys]
    for t, (y, z) in enumerate(zip(ys, zs)):
        o = (y * _sigmoid(z + bglu_ref[...])) * token(gs_ref, t).astype(F32)
        for h in range(nh):
            o_ref[h, pl.ds(t, rows, stride=S5_L), :] = o[:, h * hw:(h + 1) * hw]


def _s5(u, gs, batch, seq, a_re, a_im, log_dt, b_re, b_im, c_re, c_im, d_skip, w_glu, b_glu):
    t = batch * seq
    lw = S5_L * S5_WIDTH
    gp2 = 2 * S5_GROUPS * S5_STATE
    bmat, tmat, cmat, lam = _s5_matrices(a_re, a_im, log_dt, b_re, b_im, c_re, c_im)
    rows = S5_NB * seq // S5_L
    halves = S5_HALVES
    blk = pl.BlockSpec((rows, lw), lambda i: (i, 0))
    return pl.pallas_call(
        _s5_kernel,
        grid=(batch // S5_NB,),
        in_specs=[blk, blk, _const_spec(bmat.shape, True), _const_spec(tmat.shape, True),
                  _const_spec(cmat.shape, True), _const_spec((2, gp2 // 2)),
                  _const_spec((1, S5_WIDTH)), _const_spec((S5_WIDTH, S5_WIDTH)), _const_spec((1, S5_WIDTH))],
        out_specs=pl.BlockSpec((halves, rows * S5_L, 128), lambda i: (0, i, 0)),
        out_shape=jax.ShapeDtypeStruct((halves, t, 128), F32),
        scratch_shapes=[pltpu.VMEM((rows, gp2), F32), pltpu.VMEM((rows, gp2), F32)],
        compiler_params=_cparams(1),
        name="s5_mixer",
    )(u, gs, bmat, tmat, cmat, lam, d_skip.reshape(1, -1).astype(F32), w_glu.astype(BF16),
      b_glu.reshape(1, -1).astype(F32))


def _out_ln_kernel(n_act, then_odd_in, x_ref, *refs):
    acts = refs[:n_act]
    ws = refs[n_act:2 * n_act]
    g_ref, b_ref = refs[2 * n_act:2 * n_act + 2]
    rest = refs[2 * n_act + 2:]
    o_ref = rest[3] if then_odd_in else rest[0]

    def operand(ref, rows):
        if len(ref.shape) == 2:
            return ref[rows, :]
        return jnp.concatenate([ref[i, rows, :] for i in range(ref.shape[0])], axis=1).astype(BF16)

    tm = x_ref.shape[0]
    nblk = ROW_SPLIT
    blocks = [slice(r * tm // nblk, (r + 1) * tm // nblk) for r in range(nblk)]
    xns = []
    for rows in blocks:
        y = jnp.dot(operand(acts[0], rows), ws[0][...], preferred_element_type=F32)
        for a, w in zip(acts[1:], ws[1:]):
            y = y + jnp.dot(operand(a, rows), w[...], preferred_element_type=F32)
        z = DEEPNORM_ALPHA * x_ref[rows, :] + y
        mu = jnp.mean(z, axis=-1, keepdims=True)
        zc = z - mu
        var = jnp.mean(zc * zc, axis=-1, keepdims=True)
        xn = zc * lax.rsqrt(var + LN_EPS) * g_ref[...] + b_ref[...]
        o_ref[rows, :] = xn
        xns.append(xn.astype(BF16))
    if then_odd_in:
        for rows, xb in zip(blocks, xns):
            _odd_in_body(xb, rows, *rest[:3], *rest[4:])


def _out_ln(xf, acts, ws, ln_g, ln_b, odd_in=None):
    t = xf.shape[0]
    tm = TOKEN_TILE
    n = len(acts)

    def row(c):
        return pl.BlockSpec((tm, c), lambda i: (i, 0))

    def act_spec(a):
        if a.ndim == 2:
            return row(a.shape[1])
        return pl.BlockSpec((a.shape[0], tm, a.shape[2]), lambda i: (0, i, 0))

    in_specs = ([row(D_MODEL)] + [act_spec(a) for a in acts] + [_const_spec(w.shape) for w in ws]
                + [_const_spec((1, D_MODEL)), _const_spec((1, D_MODEL))])
    args = [xf, *acts, *ws, ln_g.reshape(1, -1), ln_b.reshape(1, -1)]
    out_specs = [row(D_MODEL)]
    out_shape = [jax.ShapeDtypeStruct((t, D_MODEL), F32)]
    if odd_in is not None:
        odd_args, odd_specs, odd_out_specs, odd_out_shape = _odd_in_operands(t, tm, *odd_in)
        in_specs += odd_specs
        args += odd_args
        out_specs += odd_out_specs
        out_shape += odd_out_shape
    outs = pl.pallas_call(
        functools.partial(_out_ln_kernel, n, odd_in is not None),
        grid=(t // tm,),
        in_specs=in_specs,
        out_specs=out_specs,
        out_shape=out_shape,
        compiler_params=_cparams(1),
        name="out_proj_layernorm" if odd_in is None else "out_proj_layernorm_odd_in_proj",
    )(*args)
    return outs[0] if odd_in is None else outs


OD_Q = 0
OD_K = OD_Q + GLA_KEY_WIDTH
OD_V = OD_K + GLA_KEY_WIDTH
OD_GL = OD_V + GLA_WIDTH
OD_G = OD_GL + 128
OD_COLS = OD_G + GLA_WIDTH


def _odd_in_kernel(x_ref, *refs):
    _odd_in_body(x_ref[...].astype(BF16), slice(None), *refs)


def _odd_in_body(xb, rows, w_ref, wgk_ref, bgk_ref, q_ref, k_ref, v_ref, la_ref, g_ref):
    def proj(lo, hi):
        return jnp.dot(xb, w_ref[:, lo:hi], preferred_element_type=F32)

    z = jnp.dot(proj(OD_GL, OD_G).astype(BF16), wgk_ref[...], preferred_element_type=F32) + bgk_ref[...]
    g = proj(OD_G, OD_COLS)
    log_sig = -(jnp.maximum(-z, 0.0) + jnp.log1p(jnp.exp(-jnp.abs(z))))
    la_ref[rows, :] = log_sig / GLA_GATE_NORM
    v_ref[rows, :] = proj(OD_V, OD_GL).astype(BF16)
    g_ref[rows, :] = _silu(g).astype(BF16)
    q_ref[rows, :] = proj(OD_Q, OD_K).astype(BF16)
    k_ref[rows, :] = proj(OD_K, OD_V).astype(BF16)


def _odd_in_operands(t, tm, w_in, w_gk2, b_gk):
    q, k, v, gl, g = jnp.split(w_in, [OD_K, OD_V, OD_GL, OD_GL + GLA_GATE_RANK], axis=1)
    gl = jnp.pad(gl, ((0, 0), (0, 128 - GLA_GATE_RANK)))
    w_p = jnp.concatenate([q, k, v, gl, g], axis=1).astype(BF16)
    wgk_p = jnp.pad(w_gk2, ((0, 128 - GLA_GATE_RANK), (0, 0))).astype(BF16)

    def row(c):
        return pl.BlockSpec((tm, c), lambda i: (i, 0))

    args = [w_p, wgk_p, b_gk.reshape(1, -1).astype(F32)]
    specs = [_const_spec((D_MODEL, OD_COLS)), _const_spec((128, GLA_KEY_WIDTH)), _const_spec((1, GLA_KEY_WIDTH))]
    out_specs = [row(GLA_KEY_WIDTH), row(GLA_KEY_WIDTH), row(GLA_WIDTH), row(GLA_KEY_WIDTH), row(GLA_WIDTH)]
    out_shape = [jax.ShapeDtypeStruct((t, GLA_KEY_WIDTH), BF16), jax.ShapeDtypeStruct((t, GLA_KEY_WIDTH), BF16),
                 jax.ShapeDtypeStruct((t, GLA_WIDTH), BF16), jax.ShapeDtypeStruct((t, GLA_KEY_WIDTH), F32),
                 jax.ShapeDtypeStruct((t, GLA_WIDTH), BF16)]
    return args, specs, out_specs, out_shape


def _odd_in(xf, w_in, w_gk2, b_gk):
    t = xf.shape[0]
    tm = TOKEN_TILE
    args, specs, out_specs, out_shape = _odd_in_operands(t, tm, w_in, w_gk2, b_gk)
    return pl.pallas_call(
        _odd_in_kernel,
        grid=(t // tm,),
        in_specs=[pl.BlockSpec((tm, D_MODEL), lambda i: (i, 0))] + specs,
        out_specs=out_specs,
        out_shape=out_shape,
        compiler_params=_cparams(1),
        name="odd_in_proj",
    )(xf, *args)


def _gla_kernel(q_ref, k_ref, v_ref, la_ref, g_ref, gn_ref, o_ref, bc_scr):
    c, sb = GLA_CHUNK, GLA_SUPER
    seq = q_ref.shape[1]
    ri = lax.broadcasted_iota(jnp.int32, (sb, sb), 0)
    ci = lax.broadcasted_iota(jnp.int32, (sb, sb), 1)
    valid = jnp.logical_and(ci <= ri, ci >= (ri // c) * c)
    tri = jnp.where(valid, 1.0, 0.0).astype(BF16)
    blocks = [slice(s * sb, (s + 1) * sb) for s in range(seq // sb)]
    chunks = [slice(n * c, (n + 1) * c) for n in range(seq // c)]
    heads = [_gla_head(hh, q_ref, k_ref, v_ref, la_ref, g_ref, gn_ref, o_ref, bc_scr, blocks, chunks, tri, valid)
             for hh in range(GLA_STEP_HEADS)]
    live = list(enumerate(heads))
    tick = 0
    while live:
        for i, gen in list(live):
            if tick >= i * GLA_STAGE_SKEW and next(gen, StopIteration) is StopIteration:
                live.remove((i, gen))
        tick += 1


def _gla_head(hh, q_ref, k_ref, v_ref, la_ref, g_ref, gn_ref, o_ref, bc_scr, blocks, chunks, tri, valid):
    c, sb = GLA_CHUNK, GLA_SUPER
    cps = sb // c
    nch = len(chunks)
    nt = (((1,), (1,)), ((), ()))
    tn = (((0,), (0,)), ((), ()))
    scale = GLA_DK ** -0.5
    kl = slice(hh * GLA_DK, (hh + 1) * GLA_DK)
    vl = slice(hh * GLA_DV, (hh + 1) * GLA_DV)

    bcs = []
    for rows in blocks:
        ga = la_ref[0, rows, kl]
        ga_hi = ga.astype(BF16)
        ga_lo = (ga - ga_hi.astype(F32)).astype(BF16)
        cs = jnp.dot(tri, jnp.concatenate([ga_hi, ga_lo], axis=1), preferred_element_type=F32)
        bc = cs[:, :GLA_DK] + cs[:, GLA_DK:]
        bc_scr[hh, rows, :] = bc
        bcs.append(bc.reshape(cps, c, GLA_DK))
    yield

    qds, k_invs, k_ends = [], [], []
    for rows, bc3 in zip(blocks, bcs):
        bl3 = bc3[:, c - 1:c, :]
        q3 = q_ref[0, rows, kl].astype(F32).reshape(cps, c, GLA_DK) * scale
        k3 = k_ref[0, rows, kl].astype(F32).reshape(cps, c, GLA_DK)
        qds.append((q3 * jnp.exp(bc3)).reshape(sb, GLA_DK).astype(BF16))
        k_invs.append((k3 * jnp.exp(-bc3)).reshape(sb, GLA_DK).astype(BF16))
        k_ends.append((k3 * jnp.exp(bl3 - bc3)).reshape(sb, GLA_DK).astype(BF16))
    yield

    atts = [lax.dot_general(qd, k_inv, nt, preferred_element_type=F32) for qd, k_inv in zip(qds, k_invs)]
    yield
    atts = [jnp.where(valid, att, 0.0).astype(BF16) for att in atts]
    o_intras = [jnp.dot(att, v_ref[0, rows, vl], preferred_element_type=F32)
                for att, rows in zip(atts, blocks)]
    yield

    def chunk_of(vals, n):
        s, j = divmod(n, cps)
        return vals[s][j * c:(j + 1) * c]

    kvs = [lax.dot_general(chunk_of(k_ends, n), v_ref[0, chunks[n], vl], tn, preferred_element_type=F32)
           for n in range(nch - 1)]
    yield

    b_last = bc_scr[hh, pl.ds(c - 1, nch, stride=c), :]
    dec = jnp.concatenate([jnp.exp(b_last), jnp.zeros((GLA_DK - nch, GLA_DK), F32)], axis=0)
    dec_t = dec.T

    st = jnp.zeros((GLA_DK, GLA_DV), F32)
    states = [st.astype(BF16)]
    for n in range(nch - 1):
        st = st * dec_t[:, n:n + 1] + kvs[n]
        states.append(st.astype(BF16))
    yield

    o_inters = [jnp.dot(chunk_of(qds, n), states[n], preferred_element_type=F32) for n in range(nch)]
    yield
    for n in range(nch):
        o = chunk_of(o_intras, n) + o_inters[n]
        o = _rms(o, gn_ref[...]) * g_ref[0, chunks[n], vl].astype(F32)
        o_ref[0, chunks[n], vl] = o.astype(BF16)


def _gla(q, k, v, la, g, g_norm, batch, seq):
    q = q.reshape(batch, seq, -1)
    k = k.reshape(batch, seq, -1)
    v = v.reshape(batch, seq, -1)
    la = la.reshape(batch, seq, -1)
    g = g.reshape(batch, seq, -1)

    def spec(c):
        return pl.BlockSpec((1, seq, GLA_STEP_HEADS * c), lambda b, h: (b, 0, h))

    o = pl.pallas_call(
        _gla_kernel,
        grid=(batch, GLA_HEADS // GLA_STEP_HEADS),
        in_specs=[spec(GLA_DK), spec(GLA_DK), spec(GLA_DV), spec(GLA_DK), spec(GLA_DV),
                  _const_spec((1, GLA_DV))],
        out_specs=spec(GLA_DV),
        out_shape=jax.ShapeDtypeStruct((batch, seq, GLA_WIDTH), BF16),
        scratch_shapes=[pltpu.VMEM((GLA_STEP_HEADS, seq, GLA_DK), F32)],
        compiler_params=_cparams(2),
        name="gla_mixer",
    )(q, k, v, la, g, g_norm.reshape(1, -1).astype(F32))
    return o.reshape(batch * seq, GLA_WIDTH)


def kernel(x, positions, ln_g, ln_b, even_w_in, mla_q_norm, mla_kv_norm, mla_w_uq, mla_w_ukv, s5_a_re, s5_a_im, s5_log_dt, s5_b_re, s5_b_im, s5_c_re, s5_c_im, s5_d, s5_w_glu, s5_b_glu, even_w_out, odd_w_in, gla_w_gk2, gla_b_gk, gla_g_norm, odd_w_out):
    batch, seq, _ = x.shape
    xf = x.reshape(batch * seq, D_MODEL)
    odd_inputs = None
    for layer in range(DEPTH):
        j = layer // 2
        if layer % 2 == 0:
            q, k, v, gm, u, gs = _even_in(xf, positions, even_w_in[j], mla_q_norm[j],
                                          mla_kv_norm[j], mla_w_uq[j], mla_w_ukv[j])
            o_mla = _attention(q, k, v, gm, batch, seq)
            o_s5 = _s5(u, gs, batch, seq, s5_a_re[j], s5_a_im[j], s5_log_dt[j], s5_b_re[j], s5_b_im[j],
                       s5_c_re[j], s5_c_im[j], s5_d[j], s5_w_glu[j], s5_b_glu[j])
            w_out = even_w_out[j].astype(BF16)
            nxt = (odd_w_in[j], gla_w_gk2[j], gla_b_gk[j]) if layer + 1 < DEPTH else None
            res = _out_ln(xf, [o_mla, o_s5], [w_out[:MLA_WIDTH], w_out[MLA_WIDTH:]], ln_g[layer], ln_b[layer],
                          odd_in=nxt)
            xf, odd_inputs = (res, None) if nxt is None else (res[0], res[1:])
        else:
            if odd_inputs is None:
                odd_inputs = _odd_in(xf, odd_w_in[j], gla_w_gk2[j], gla_b_gk[j])
            q, k, v, la, g = odd_inputs
            odd_inputs = None
            o = _gla(q, k, v, la, g, gla_g_norm[j], batch, seq)
            xf = _out_ln(xf, [o], [odd_w_out[j].astype(BF16)], ln_g[layer], ln_b[layer])
    return xf.reshape(batch, seq, D_MODEL)
```

```python
import functools
import itertools
import math

import jax
import jax.numpy as jnp
from jax import lax
from jax.experimental import pallas as pl
from jax.experimental.pallas import tpu as pltpu

F32 = jnp.float32
BF16 = jnp.bfloat16

D_MODEL = 1024
DEPTH = 2

MLA_HEADS = 8
MLA_NOPE = 64
MLA_ROPE = 32
MLA_V = 64
MLA_Q_RANK = 256
MLA_KV_RANK = 128
MLA_WIDTH = MLA_HEADS * MLA_V
ROPE_BASE = 10000.0
HEAD_SLOT = 128

S5_WIDTH = 256
S5_GROUP = 16
S5_GROUPS = S5_WIDTH // S5_GROUP
S5_STATE = 64
S5_L = 8
S5_NB = 2
S5_HALVES = 2

GLA_HEADS = 4
GLA_KEY_WIDTH = D_MODEL // 2
GLA_WIDTH = D_MODEL
GLA_DK = GLA_KEY_WIDTH // GLA_HEADS
GLA_DV = GLA_WIDTH // GLA_HEADS
GLA_GATE_RANK = 16
GLA_GATE_NORM = 16.0
GLA_CHUNK = 64
GLA_SUPER = 256
GLA_STEP_HEADS = 2
GLA_STAGE_SKEW = 3

DEEPNORM_ALPHA = (2 * DEPTH) ** 0.25
LN_EPS = 1e-5
RMS_EPS = 1e-6

TOKEN_TILE = 1024
ROW_SPLIT = 4
ATTN_T = 256
ATTN_STEP_HEADS = 2
NEG_BIG = -1e30
VMEM_LIMIT = 56 * 1024 * 1024


def _cparams(n_axes):
    return pltpu.CompilerParams(dimension_semantics=("arbitrary",) * n_axes,
                                vmem_limit_bytes=VMEM_LIMIT)


def _const_spec(shape, single=False):
    nd = len(shape)
    if single:
        return pl.BlockSpec(shape, lambda *_: (0,) * nd, pipeline_mode=pl.Buffered(1))
    return pl.BlockSpec(shape, lambda *_: (0,) * nd)


def _sigmoid(x):
    return 1.0 / (1.0 + jnp.exp(-x))


def _silu(x):
    return x * _sigmoid(x)


def _rms(x, g):
    return (x * lax.rsqrt(jnp.mean(x * x, axis=-1, keepdims=True) + RMS_EPS)) * g


ROPE_PACK = 128 // (MLA_ROPE // 2)


def _rope_tables(pos_ref, invf_ref, cos_ref, sin_ref):
    half = MLA_ROPE // 2
    rb = pos_ref.shape[0]
    pos = pos_ref[...].astype(F32)
    lane = lax.broadcasted_iota(jnp.int32, (rb, 128), 1)
    owner = lane // half
    posx = jnp.zeros((rb, 128), F32)
    for s in range(ROPE_PACK):
        posx = jnp.where(owner == s, pos[:, s:s + 1], posx)
    ang = posx * invf_ref[...]
    c = jnp.cos(ang)
    sn = jnp.sin(ang)
    first = (lane & 112) == MLA_NOPE
    second = (lane & 112) == MLA_NOPE + half
    for s in range(ROPE_PACK):
        k1 = (MLA_NOPE - half * s) % 128
        k2 = (MLA_NOPE + half - half * s) % 128
        cf = jnp.where(first, pltpu.roll(c, k1, 1), jnp.where(second, pltpu.roll(c, k2, 1), 1.0))
        sf = jnp.where(first, -pltpu.roll(sn, k1, 1), jnp.where(second, pltpu.roll(sn, k2, 1), 0.0))
        cos_ref[pl.ds(s, rb, stride=ROPE_PACK), :] = cf
        sin_ref[pl.ds(s, rb, stride=ROPE_PACK), :] = sf


EV_CQ = 0
EV_CKV = EV_CQ + MLA_Q_RANK
EV_KPE = EV_CKV + MLA_KV_RANK
EV_GM = EV_KPE + HEAD_SLOT
EV_U = EV_GM + MLA_WIDTH
EV_GS = EV_U + S5_WIDTH
EV_COLS = EV_GS + S5_WIDTH
Q_SCALE = (MLA_NOPE + MLA_ROPE) ** -0.5 * math.log2(math.e)


def _rotate_pairs(x):
    n = x.shape[-1]
    lane = lax.broadcasted_iota(jnp.int32, x.shape, 1)
    lo = (lane & 112) == 64
    return jnp.where(lo, pltpu.roll(x, n - MLA_ROPE // 2, 1), pltpu.roll(x, MLA_ROPE // 2, 1))


def _even_in_kernel(x_ref, pos_ref, invf_ref, w_in_ref, qn_ref, kvn_ref, wuq_ref, wk_ref, wv_ref,
                    q_ref, k_ref, v_ref, gm_ref, u_ref, gs_ref, fold_scr, cos_ref, sin_ref):
    _rope_tables(pos_ref, invf_ref, cos_ref, sin_ref)
    tm = x_ref.shape[0]
    nb = tm // ROW_SPLIT
    for r in range(ROW_SPLIT):
        rows = slice(r * nb, (r + 1) * nb)
        h = jnp.dot(x_ref[rows, :].astype(BF16), w_in_ref[...], preferred_element_type=F32)
        cos = cos_ref[rows, :]
        sin = sin_ref[rows, :]

        cqn = _rms(h[:, EV_CQ:EV_CKV], qn_ref[...]).astype(BF16)
        q = jnp.dot(cqn, wuq_ref[...], preferred_element_type=F32)
        q = q * jnp.tile(cos, (1, MLA_HEADS)) + _rotate_pairs(q) * jnp.tile(sin, (1, MLA_HEADS))
        q_ref[rows, :] = (q * Q_SCALE).astype(BF16)

        kpe = h[:, EV_KPE:EV_GM]
        kpe = kpe * cos + _rotate_pairs(kpe) * sin
        ckvn = _rms(h[:, EV_CKV:EV_KPE], kvn_ref[...]).astype(BF16)
        k = jnp.dot(ckvn, wk_ref[...], preferred_element_type=F32) + jnp.tile(kpe, (1, MLA_HEADS))
        k_ref[rows, :] = k.astype(BF16)
        v_ref[rows, :] = jnp.dot(ckvn, wv_ref[...], preferred_element_type=F32).astype(BF16)

        gm_ref[rows, :] = _silu(h[:, EV_GM:EV_U]).astype(BF16)
        fr = nb // S5_L
        folded = slice(r * fr, (r + 1) * fr)
        hw = S5_WIDTH // S5_HALVES
        for ref, val in ((u_ref, h[:, EV_U:EV_GS]), (gs_ref, _silu(h[:, EV_GS:EV_COLS]))):
            for half in range(S5_HALVES):
                fold_scr[half] = val[:, half * hw:(half + 1) * hw]
            for half in range(S5_HALVES):
                for s in range(S5_L):
                    lo = (half * S5_L + s) * hw
                    ref[folded, lo:lo + hw] = fold_scr[half, pl.ds(s, fr, stride=S5_L), :].astype(BF16)


def _even_in(xf, positions, w_in, q_norm, kv_norm, w_uq, w_ukv):
    t = xf.shape[0]
    tm = TOKEN_TILE
    pad = HEAD_SLOT - MLA_NOPE - MLA_ROPE
    half = MLA_ROPE // 2
    pos = positions.reshape(t // ROPE_PACK, ROPE_PACK)
    inv_freq = ROPE_BASE ** (-jnp.arange(half, dtype=F32) / half)
    invf = jnp.tile(inv_freq, ROPE_PACK).reshape(1, 128)
    cq, ckv, kr, gm, u, gs = jnp.split(
        w_in, [MLA_Q_RANK, MLA_Q_RANK + MLA_KV_RANK, MLA_Q_RANK + MLA_KV_RANK + MLA_ROPE,
               MLA_Q_RANK + MLA_KV_RANK + MLA_ROPE + MLA_WIDTH,
               MLA_Q_RANK + MLA_KV_RANK + MLA_ROPE + MLA_WIDTH + S5_WIDTH], axis=1)
    kr = jnp.pad(kr, ((0, 0), (MLA_NOPE, pad)))
    w_in_p = jnp.concatenate([cq, ckv, kr, gm, u, gs], axis=1).astype(BF16)
    wuq_p = jnp.pad(w_uq.reshape(MLA_Q_RANK, MLA_HEADS, MLA_NOPE + MLA_ROPE),
                    ((0, 0), (0, 0), (0, pad))).reshape(MLA_Q_RANK, MLA_HEADS * HEAD_SLOT).astype(BF16)
    wkv = w_ukv.reshape(MLA_KV_RANK, MLA_HEADS, MLA_NOPE + MLA_V)
    wk_p = jnp.pad(wkv[:, :, :MLA_NOPE], ((0, 0), (0, 0), (0, HEAD_SLOT - MLA_NOPE))
                   ).reshape(MLA_KV_RANK, MLA_HEADS * HEAD_SLOT).astype(BF16)
    wv_p = wkv[:, :, MLA_NOPE:].reshape(MLA_KV_RANK, MLA_WIDTH).astype(BF16)

    def row(n):
        return pl.BlockSpec((tm, n), lambda i: (i, 0))

    qk_w = MLA_HEADS * HEAD_SLOT
    lw = S5_L * S5_WIDTH
    folded = jax.ShapeDtypeStruct((t // S5_L, lw), BF16)
    folded_spec = pl.BlockSpec((tm // S5_L, lw), lambda i: (i, 0))
    outs = [jax.ShapeDtypeStruct((t, qk_w), BF16), jax.ShapeDtypeStruct((t, qk_w), BF16),
            jax.ShapeDtypeStruct((t, MLA_WIDTH), BF16), jax.ShapeDtypeStruct((t, MLA_WIDTH), BF16),
            folded, folded]
    return pl.pallas_call(
        _even_in_kernel,
        grid=(t // tm,),
        in_specs=[row(D_MODEL), pl.BlockSpec((tm // ROPE_PACK, ROPE_PACK), lambda i: (i, 0)),
                  _const_spec((1, 128)),
                  _const_spec((D_MODEL, EV_COLS)), _const_spec((1, MLA_Q_RANK)),
                  _const_spec((1, MLA_KV_RANK)), _const_spec((MLA_Q_RANK, qk_w)),
                  _const_spec((MLA_KV_RANK, qk_w)), _const_spec((MLA_KV_RANK, MLA_WIDTH))],
        out_specs=[row(qk_w), row(qk_w), row(MLA_WIDTH), row(MLA_WIDTH), folded_spec, folded_spec],
        out_shape=outs,
        scratch_shapes=[pltpu.VMEM((S5_HALVES, tm // ROW_SPLIT, S5_WIDTH // S5_HALVES), F32),
                        pltpu.VMEM((tm, HEAD_SLOT), F32), pltpu.VMEM((tm, HEAD_SLOT), F32)],
        compiler_params=_cparams(1),
        name="even_in_proj",
    )(xf, pos, invf, w_in_p, q_norm.reshape(1, -1), kv_norm.reshape(1, -1), wuq_p, wk_p, wv_p)


def _attn_kernel(q_ref, k_ref, v_ref, g_ref, o_ref):
    seq = q_ref.shape[1]
    t = ATTN_T
    n = seq // t
    nt = (((1,), (1,)), ((), ()))
    keep = lax.broadcasted_iota(jnp.int32, (t, t), 1) <= lax.broadcasted_iota(jnp.int32, (t, t), 0)
    lane_head = lax.broadcasted_iota(jnp.int32, (t, ATTN_STEP_HEADS * MLA_V), 1) // MLA_V

    def scores(qi, c):
        rows = slice(qi * t, (qi + 1) * t)
        lanes = slice(c * HEAD_SLOT, (c + 1) * HEAD_SLOT)
        q = q_ref[0, rows, lanes]
        sd = lax.dot_general(q, k_ref[0, rows, lanes], nt, preferred_element_type=F32)
        sd = jnp.where(keep, sd, NEG_BIG)
        if qi == 0:
            return sd
        s = lax.dot_general(q, k_ref[0, :qi * t, lanes], nt, preferred_element_type=F32)
        return jnp.concatenate([s, sd], axis=1)

    def attend(qi, ss):
        rows = slice(qi * t, (qi + 1) * t)
        o = None
        for c, s in enumerate(ss):
            slabs = [s[:, j:j + 128] for j in range(0, s.shape[1], 128)]
            m = functools.reduce(jnp.maximum, slabs)
            m = jnp.max(m, axis=1, keepdims=True)
            lsum = jnp.zeros((t, 128), F32)
            ps = []
            for sl in slabs:
                pj = jnp.exp2(sl - m)
                lsum = lsum + pj
                ps.append(pj.astype(BF16))
            l = jnp.sum(lsum, axis=1, keepdims=True)
            p = jnp.concatenate(ps, axis=1)
            acc = jnp.dot(p, v_ref[0, :(qi + 1) * t, :], preferred_element_type=F32)
            oc = acc * (1.0 / l)
            o = oc if o is None else jnp.where(lane_head == c, oc, o)
        o_ref[0, rows, :] = (o * g_ref[0, rows, :].astype(F32)).astype(BF16)

    pending = None
    for qi in reversed(range(n)):
        ss = [scores(qi, c) for c in range(ATTN_STEP_HEADS)]
        if pending is not None:
            attend(*pending)
        pending = (qi, ss)
    attend(*pending)


def _attention(q, k, v, g, batch, seq):
    q = q.reshape(batch, seq, -1)
    k = k.reshape(batch, seq, -1)
    v = v.reshape(batch, seq, -1)
    g = g.reshape(batch, seq, -1)
    qk = pl.BlockSpec((1, seq, ATTN_STEP_HEADS * HEAD_SLOT), lambda b, st: (b, 0, st))
    vo = pl.BlockSpec((1, seq, ATTN_STEP_HEADS * MLA_V), lambda b, st: (b, 0, st))
    o = pl.pallas_call(
        _attn_kernel,
        grid=(batch, MLA_HEADS // ATTN_STEP_HEADS),
        in_specs=[qk, qk, vo, vo],
        out_specs=vo,
        out_shape=jax.ShapeDtypeStruct((batch, seq, MLA_WIDTH), BF16),
        compiler_params=_cparams(2),
        name="mla_attention",
    )(q, k, v, g)
    return o.reshape(batch * seq, MLA_WIDTH)


def _s5_matrices(a_re, a_im, log_dt, b_re, b_im, c_re, c_im):
    L, G, P, H = S5_L, S5_GROUPS, S5_STATE, S5_GROUP
    dt = jnp.exp(log_dt.astype(F32))[:, None]
    ar, ai = a_re.astype(F32), a_im.astype(F32)
    mag = jnp.exp(ar * dt)
    lr, li = mag * jnp.cos(ai * dt), mag * jnp.sin(ai * dt)
    den = ar * ar + ai * ai
    nr, ni = lr - 1.0, li
    zr = (nr * ar + ni * ai) / den
    zi = (ni * ar - nr * ai) / den
    br, bi = b_re.astype(F32), b_im.astype(F32)
    bbr = zr[..., None] * br - zi[..., None] * bi
    bbi = zr[..., None] * bi + zi[..., None] * br
    cr, ci = c_re.astype(F32), c_im.astype(F32)

    def powers(j):
        j = j.astype(F32)[:, None, None]
        pmag = jnp.exp(j * (ar * dt))
        return pmag * jnp.cos(j * (ai * dt)), pmag * jnp.sin(j * (ai * dt))

    gp, w = G * P, G * H
    pr, pi = powers(jnp.arange(L + 1))
    prow = jnp.concatenate([pr.reshape(L + 1, gp), pi.reshape(L + 1, gp)], axis=0)
    prow3 = prow.reshape(2 * (L + 1), 1, gp)
    pcol3 = prow.reshape(2 * (L + 1), gp, 1)

    same_b = (jnp.arange(w)[:, None] // H) == (jnp.arange(gp)[None, :] // P)
    same_c = (jnp.arange(gp)[:, None] // P) == (jnp.arange(w)[None, :] // H)
    bd_br = jnp.where(same_b, jnp.tile(bbr.transpose(2, 0, 1).reshape(H, gp), (G, 1)), 0.0)
    bd_bi = jnp.where(same_b, jnp.tile(bbi.transpose(2, 0, 1).reshape(H, gp), (G, 1)), 0.0)
    bd_cr = jnp.where(same_c, jnp.tile(cr.transpose(0, 2, 1).reshape(gp, H), (1, G)), 0.0)
    bd_ci = jnp.where(same_c, jnp.tile(ci.transpose(0, 2, 1).reshape(gp, H), (1, G)), 0.0)

    nh, hw = S5_HALVES, S5_WIDTH // S5_HALVES
    bmat, tmat, cmat = pl.pallas_call(
        _s5_prepare_kernel,
        grid=(L,),
        in_specs=[_const_spec((w, gp)), _const_spec((w, gp)), _const_spec((gp, w)), _const_spec((gp, w)),
                  _const_spec((2 * (L + 1), gp)),
                  pl.BlockSpec((1, 1, gp), lambda i: (L - 1 - i, 0, 0)),
                  pl.BlockSpec((1, 1, gp), lambda i: (2 * L - i, 0, 0)),
                  pl.BlockSpec((1, gp, 1), lambda i: (i + 1, 0, 0)),
                  pl.BlockSpec((1, gp, 1), lambda i: (L + 2 + i, 0, 0))],
        out_specs=[pl.BlockSpec((nh, hw, 2 * gp // nh), lambda i: (0, i, 0)),
                   pl.BlockSpec((nh, hw, L * hw), lambda i: (0, i, 0)),
                   pl.BlockSpec((nh, 2 * gp // nh, hw), lambda i: (0, 0, i))],
        out_shape=[jax.ShapeDtypeStruct((nh, L * hw, 2 * gp // nh), BF16),
                   jax.ShapeDtypeStruct((nh, L * hw, L * hw), BF16),
                   jax.ShapeDtypeStruct((nh, 2 * gp // nh, L * hw), BF16)],
        scratch_shapes=[pltpu.VMEM((L, nh, hw, hw), F32)],
        compiler_params=_cparams(1),
        name="s5_prepare",
    )(bd_br, bd_bi, bd_cr, bd_ci, prow, prow3, prow3, pcol3, pcol3)
    lam = jnp.stack([pr[L].reshape(gp), pi[L].reshape(gp)])
    return bmat, tmat, cmat, lam


def _s5_prepare_kernel(br_ref, bi_ref, cr_ref, ci_ref, prow_ref, rr_ref, ri_ref, cr_pow_ref, ci_pow_ref,
                       bmat_ref, tmat_ref, cmat_ref, k_scr):
    L = S5_L
    hw = S5_WIDTH // S5_HALVES
    hgp = S5_GROUPS * S5_STATE // S5_HALVES
    i = pl.program_id(0)
    br, bi = br_ref[...], bi_ref[...]
    cr, ci = cr_ref[...], ci_ref[...]

    halves = [(slice(h * hw, (h + 1) * hw), slice(h * hgp, (h + 1) * hgp)) for h in range(S5_HALVES)]

    @pl.when(i == 0)
    def _():
        for j in range(L):
            pr = prow_ref[j:j + 1, :]
            pi = prow_ref[L + 1 + j:L + 2 + j, :]
            pb_r, pb_i = br * pr - bi * pi, bi * pr + br * pi
            for h, (ch, st) in enumerate(halves):
                k_scr[j, h] = (jnp.dot(pb_r[ch, st], cr[st, ch], precision=lax.Precision.HIGHEST,
                                       preferred_element_type=F32)
                               - jnp.dot(pb_i[ch, st], ci[st, ch], precision=lax.Precision.HIGHEST,
                                         preferred_element_type=F32))

    pr, pi = rr_ref[0], ri_ref[0]
    pcr, pci = cr_pow_ref[0], ci_pow_ref[0]
    pb_r, pb_i = br * pr - bi * pi, bi * pr + br * pi
    g_r, g_i = cr * pcr - ci * pci, -(cr * pci + ci * pcr)
    lags = [k_scr[jnp.maximum(t - i, 0)] for t in range(L)]
    for h, (ch, st) in enumerate(halves):
        bmat_ref[h, :, :hgp] = pb_r[ch, st].astype(BF16)
        bmat_ref[h, :, hgp:] = pb_i[ch, st].astype(BF16)
        for t in range(L):
            tmat_ref[h, :, t * hw:(t + 1) * hw] = jnp.where(t >= i, lags[t][h], 0.0).astype(BF16)
        cmat_ref[h, :hgp, :] = g_r[st, ch].astype(BF16)
        cmat_ref[h, hgp:, :] = g_i[st, ch].astype(BF16)


def _gelu_tanh(y):
    return 0.5 * y * (1.0 + jnp.tanh(math.sqrt(2.0 / math.pi) * (y + 0.044715 * (y * y * y))))


def _s5_kernel(u_ref, gs_ref, bmat_ref, tmat_ref, cmat_ref, lam_ref, dskip_ref, wglu_ref, bglu_ref,
               o_ref, x_scr, h_scr):
    nh = S5_HALVES
    hw = S5_WIDTH // nh
    gp = S5_GROUPS * S5_STATE
    hgp = gp // nh
    hl = S5_L * hw
    rows = u_ref.shape[0]
    per_batch = rows // S5_NB
    for h in range(nh):
        xh = jnp.dot(u_ref[:, h * hl:(h + 1) * hl], bmat_ref[h], preferred_element_type=F32)
        x_scr[:, h * hgp:(h + 1) * hgp] = xh[:, :hgp]
        x_scr[:, gp + h * hgp:gp + (h + 1) * hgp] = xh[:, hgp:]
    lam_r = lam_ref[0:1, :]
    lam_i = lam_ref[1:2, :]

    def block(i, hs):
        out = []
        for b in range(S5_NB):
            hr, hi = hs[b]
            r0 = pl.multiple_of(b * per_batch + i * 8, 8)
            x8 = x_scr[pl.ds(r0, 8), :]
            starts_r, starts_i = [], []
            for jj in range(8):
                starts_r.append(hr)
                starts_i.append(hi)
                xr = x8[jj:jj + 1, :gp]
                xi = x8[jj:jj + 1, gp:]
                hr, hi = lam_r * hr - lam_i * hi + xr, lam_r * hi + lam_i * hr + xi
            h_scr[pl.ds(r0, 8), :gp] = jnp.concatenate(starts_r, axis=0)
            h_scr[pl.ds(r0, 8), gp:] = jnp.concatenate(starts_i, axis=0)
            out.append((hr, hi))
        return tuple(out)

    zero = jnp.zeros((1, gp), F32)
    lax.fori_loop(0, per_batch // 8, block, tuple((zero, zero) for _ in range(S5_NB)))

    hs = h_scr[...].astype(BF16)
    hs = [jnp.concatenate([hs[:, h * hgp:(h + 1) * hgp], hs[:, gp + h * hgp:gp + (h + 1) * hgp]], axis=1)
          for h in range(nh)]

    pair = 2 * hw
    ypairs = [[jnp.dot(u_ref[:, h * hl:h * hl + (tp + 1) * pair],
                       tmat_ref[h, :(tp + 1) * pair, tp * pair:(tp + 1) * pair], preferred_element_type=F32)
               + jnp.dot(hs[h], cmat_ref[h, :, tp * pair:(tp + 1) * pair], preferred_element_type=F32)
               for h in range(nh)] for tp in range(S5_L // 2)]

    def token(ref, t):
        return jnp.concatenate([ref[:, h * hl + t * hw:h * hl + (t + 1) * hw] for h in range(nh)], axis=1)

    ys = []
    for t in range(S5_L):
        e = (t % 2) * hw
        y = jnp.concatenate([yh[:, e:e + hw] for yh in ypairs[t // 2]], axis=1)
        ys.append(_gelu_tanh(y + dskip_ref[...] * token(u_ref, t).astype(F32)))
    zs = [jnp.dot(y.astype(BF16), wglu_ref[...], preferred_element_type=F32) for y in ys]
    for t, (y, z) in enumerate(zip(ys, zs)):
        o = (y * _sigmoid(z + bglu_ref[...])) * token(gs_ref, t).astype(F32)
        for h in range(nh):
            o_ref[h, pl.ds(t, rows, stride=S5_L), :] = o[:, h * hw:(h + 1) * hw]


def _s5(u, gs, batch, seq, a_re, a_im, log_dt, b_re, b_im, c_re, c_im, d_skip, w_glu, b_glu):
    t = batch * seq
    lw = S5_L * S5_WIDTH
    gp2 = 2 * S5_GROUPS * S5_STATE
    bmat, tmat, cmat, lam = _s5_matrices(a_re, a_im, log_dt, b_re, b_im, c_re, c_im)
    rows = S5_NB * seq // S5_L
    halves = S5_HALVES
    blk = pl.BlockSpec((rows, lw), lambda i: (i, 0))
    return pl.pallas_call(
        _s5_kernel,
        grid=(batch // S5_NB,),
        in_specs=[blk, blk, _const_spec(bmat.shape, True), _const_spec(tmat.shape, True),
                  _const_spec(cmat.shape, True), _const_spec((2, gp2 // 2)),
                  _const_spec((1, S5_WIDTH)), _const_spec((S5_WIDTH, S5_WIDTH)), _const_spec((1, S5_WIDTH))],
        out_specs=pl.BlockSpec((halves, rows * S5_L, 128), lambda i: (0, i, 0)),
        out_shape=jax.ShapeDtypeStruct((halves, t, 128), F32),
        scratch_shapes=[pltpu.VMEM((rows, gp2), F32), pltpu.VMEM((rows, gp2), F32)],
        compiler_params=_cparams(1),
        name="s5_mixer",
    )(u, gs, bmat, tmat, cmat, lam, d_skip.reshape(1, -1).astype(F32), w_glu.astype(BF16),
      b_glu.reshape(1, -1).astype(F32))


def _out_ln_kernel(n_act, then_odd_in, x_ref, *refs):
    acts = refs[:n_act]
    ws = refs[n_act:2 * n_act]
    g_ref, b_ref = refs[2 * n_act:2 * n_act + 2]
    rest = refs[2 * n_act + 2:]
    o_ref = rest[ODD_IN_WEIGHTS] if then_odd_in else rest[0]

    def operand(ref, rows):
        if len(ref.shape) == 2:
            return ref[rows, :]
        return jnp.concatenate([ref[i, rows, :] for i in range(ref.shape[0])], axis=1).astype(BF16)

    tm = x_ref.shape[0]
    nblk = ROW_SPLIT
    blocks = [slice(r * tm // nblk, (r + 1) * tm // nblk) for r in range(nblk)]
    xns = []
    for rows in blocks:
        y = jnp.dot(operand(acts[0], rows), ws[0][...], preferred_element_type=F32)
        for a, w in zip(acts[1:], ws[1:]):
            y = y + jnp.dot(operand(a, rows), w[...], preferred_element_type=F32)
        z = DEEPNORM_ALPHA * x_ref[rows, :] + y
        mu = jnp.mean(z, axis=-1, keepdims=True)
        zc = z - mu
        var = jnp.mean(zc * zc, axis=-1, keepdims=True)
        xn = zc * lax.rsqrt(var + LN_EPS) * g_ref[...] + b_ref[...]
        o_ref[rows, :] = xn
        xns.append(xn.astype(BF16))
    if then_odd_in:
        for rows, xb in zip(blocks, xns):
            _odd_in_body(xb, rows, *rest[:ODD_IN_WEIGHTS], *rest[ODD_IN_WEIGHTS + 1:])


def _out_ln(xf, acts, ws, ln_g, ln_b, odd_in=None):
    t = xf.shape[0]
    tm = TOKEN_TILE if odd_in is not None else 2 * TOKEN_TILE
    n = len(acts)

    def row(c):
        return pl.BlockSpec((tm, c), lambda i: (i, 0))

    def act_spec(a):
        if a.ndim == 2:
            return row(a.shape[1])
        return pl.BlockSpec((a.shape[0], tm, a.shape[2]), lambda i: (0, i, 0))

    in_specs = ([row(D_MODEL)] + [act_spec(a) for a in acts] + [_const_spec(w.shape) for w in ws]
                + [_const_spec((1, D_MODEL)), _const_spec((1, D_MODEL))])
    args = [xf, *acts, *ws, ln_g.reshape(1, -1), ln_b.reshape(1, -1)]
    out_specs = [row(D_MODEL)]
    out_shape = [jax.ShapeDtypeStruct((t, D_MODEL), F32)]
    if odd_in is not None:
        odd_args, odd_specs, odd_out_specs, odd_out_shape = _odd_in_operands(t, tm, *odd_in)
        in_specs += odd_specs
        args += odd_args
        out_specs += odd_out_specs
        out_shape += odd_out_shape
    outs = pl.pallas_call(
        functools.partial(_out_ln_kernel, n, odd_in is not None),
        grid=(t // tm,),
        in_specs=in_specs,
        out_specs=out_specs,
        out_shape=out_shape,
        compiler_params=_cparams(1),
        name="out_proj_layernorm" if odd_in is None else "out_proj_layernorm_odd_in_proj",
    )(*args)
    return outs[0] if odd_in is None else outs


def _odd_in_kernel(x_ref, *refs):
    _odd_in_body(x_ref[...].astype(BF16), slice(None), *refs)


OD_Q = 0
OD_K = OD_Q + GLA_KEY_WIDTH
OD_V = OD_K + GLA_KEY_WIDTH
OD_GL = OD_V + GLA_WIDTH
OD_G = OD_GL + 128
OD_COLS = OD_G + GLA_WIDTH
ODD_IN_WEIGHTS = 3


def _odd_in_body(xb, rows, w_ref, wgk_ref, bgk_ref, q_ref, k_ref, v_ref, la_ref, g_ref):
    def proj(lo, hi):
        return jnp.dot(xb, w_ref[:, lo:hi], preferred_element_type=F32)

    z = jnp.dot(proj(OD_GL, OD_G).astype(BF16), wgk_ref[...], preferred_element_type=F32) + bgk_ref[...]
    g = proj(OD_G, OD_COLS)
    log_sig = -(jnp.maximum(-z, 0.0) + jnp.log1p(jnp.exp(-jnp.abs(z))))
    la_ref[rows, :] = log_sig / GLA_GATE_NORM
    v_ref[rows, :] = proj(OD_V, OD_GL).astype(BF16)
    g_ref[rows, :] = _silu(g).astype(BF16)
    q_ref[rows, :] = proj(OD_Q, OD_K).astype(BF16)
    k_ref[rows, :] = proj(OD_K, OD_V).astype(BF16)


def _odd_in_operands(t, tm, w_in, w_gk2, b_gk):
    q, k, v, gl, g = jnp.split(w_in, [OD_K, OD_V, OD_GL, OD_GL + GLA_GATE_RANK], axis=1)
    gl = jnp.pad(gl, ((0, 0), (0, 128 - GLA_GATE_RANK)))
    w_p = jnp.concatenate([q, k, v, gl, g], axis=1).astype(BF16)
    wgk_p = jnp.pad(w_gk2, ((0, 128 - GLA_GATE_RANK), (0, 0))).astype(BF16)

    def row(c):
        return pl.BlockSpec((tm, c), lambda i: (i, 0))

    args = [w_p, wgk_p, b_gk.reshape(1, -1).astype(F32)]
    specs = [_const_spec((D_MODEL, OD_COLS)), _const_spec((128, GLA_KEY_WIDTH)), _const_spec((1, GLA_KEY_WIDTH))]
    assert len(args) == ODD_IN_WEIGHTS
    out_specs = [row(GLA_KEY_WIDTH), row(GLA_KEY_WIDTH), row(GLA_WIDTH), row(GLA_KEY_WIDTH), row(GLA_WIDTH)]
    out_shape = [jax.ShapeDtypeStruct((t, GLA_KEY_WIDTH), BF16), jax.ShapeDtypeStruct((t, GLA_KEY_WIDTH), BF16),
                 jax.ShapeDtypeStruct((t, GLA_WIDTH), BF16), jax.ShapeDtypeStruct((t, GLA_KEY_WIDTH), F32),
                 jax.ShapeDtypeStruct((t, GLA_WIDTH), BF16)]
    return args, specs, out_specs, out_shape


def _odd_in(xf, w_in, w_gk2, b_gk):
    t = xf.shape[0]
    tm = TOKEN_TILE
    args, specs, out_specs, out_shape = _odd_in_operands(t, tm, w_in, w_gk2, b_gk)
    return pl.pallas_call(
        _odd_in_kernel,
        grid=(t // tm,),
        in_specs=[pl.BlockSpec((tm, D_MODEL), lambda i: (i, 0))] + specs,
        out_specs=out_specs,
        out_shape=out_shape,
        compiler_params=_cparams(1),
        name="odd_in_proj",
    )(xf, *args)


def _gla_kernel(q_ref, k_ref, v_ref, la_ref, g_ref, gn_ref, o_ref, bc_scr):
    c, sb = GLA_CHUNK, GLA_SUPER
    seq = q_ref.shape[1]
    ri = lax.broadcasted_iota(jnp.int32, (sb, sb), 0)
    ci = lax.broadcasted_iota(jnp.int32, (sb, sb), 1)
    valid = jnp.logical_and(ci <= ri, ci >= (ri // c) * c)
    tri = jnp.where(valid, 1.0, 0.0).astype(BF16)
    blocks = [slice(s * sb, (s + 1) * sb) for s in range(seq // sb)]
    chunks = [slice(n * c, (n + 1) * c) for n in range(seq // c)]
    heads = [_gla_head(hh, q_ref, k_ref, v_ref, la_ref, g_ref, gn_ref, o_ref, bc_scr, blocks, chunks, tri, valid)
             for hh in range(GLA_STEP_HEADS)]
    live = list(enumerate(heads))
    tick = 0
    while live:
        for i, gen in list(live):
            if tick >= i * GLA_STAGE_SKEW and next(gen, StopIteration) is StopIteration:
                live.remove((i, gen))
        tick += 1


def _gla_head(hh, q_ref, k_ref, v_ref, la_ref, g_ref, gn_ref, o_ref, bc_scr, blocks, chunks, tri, valid):
    c, sb = GLA_CHUNK, GLA_SUPER
    cps = sb // c
    nch = len(chunks)
    nt = (((1,), (1,)), ((), ()))
    tn = (((0,), (0,)), ((), ()))
    scale = GLA_DK ** -0.5
    kl = slice(hh * GLA_DK, (hh + 1) * GLA_DK)
    vl = slice(hh * GLA_DV, (hh + 1) * GLA_DV)

    bcs = []
    for rows in blocks:
        ga = la_ref[0, rows, kl]
        ga_hi = ga.astype(BF16)
        ga_lo = (ga - ga_hi.astype(F32)).astype(BF16)
        cs = jnp.dot(tri, jnp.concatenate([ga_hi, ga_lo], axis=1), preferred_element_type=F32)
        bc = cs[:, :GLA_DK] + cs[:, GLA_DK:]
        bc_scr[hh, rows, :] = bc
        bcs.append(bc.reshape(cps, c, GLA_DK))
    yield

    qds, k_invs, k_ends = [], [], []
    for rows, bc3 in zip(blocks, bcs):
        bl3 = bc3[:, c - 1:c, :]
        q3 = q_ref[0, rows, kl].astype(F32).reshape(cps, c, GLA_DK) * scale
        k3 = k_ref[0, rows, kl].astype(F32).reshape(cps, c, GLA_DK)
        qds.append((q3 * jnp.exp(bc3)).reshape(sb, GLA_DK).astype(BF16))
        k_invs.append((k3 * jnp.exp(-bc3)).reshape(sb, GLA_DK).astype(BF16))
        k_ends.append((k3 * jnp.exp(bl3 - bc3)).reshape(sb, GLA_DK).astype(BF16))
    yield

    atts = [lax.dot_general(qd, k_inv, nt, preferred_element_type=F32) for qd, k_inv in zip(qds, k_invs)]
    yield
    atts = [jnp.where(valid, att, 0.0).astype(BF16) for att in atts]
    o_intras = [jnp.dot(att, v_ref[0, rows, vl], preferred_element_type=F32)
                for att, rows in zip(atts, blocks)]
    yield

    def chunk_of(vals, n):
        s, j = divmod(n, cps)
        return vals[s][j * c:(j + 1) * c]

    kvs = [lax.dot_general(chunk_of(k_ends, n), v_ref[0, chunks[n], vl], tn, preferred_element_type=F32)
           for n in range(nch - 1)]
    yield

    b_last = bc_scr[hh, pl.ds(c - 1, nch, stride=c), :]
    dec = jnp.concatenate([jnp.exp(b_last), jnp.zeros((GLA_DK - nch, GLA_DK), F32)], axis=0)
    dec_t = dec.T

    st = jnp.zeros((GLA_DK, GLA_DV), F32)
    states = [st.astype(BF16)]
    for n in range(nch - 1):
        st = st * dec_t[:, n:n + 1] + kvs[n]
        states.append(st.astype(BF16))
    yield

    o_inters = [jnp.dot(chunk_of(qds, n), states[n], preferred_element_type=F32) for n in range(nch)]
    yield
    for n in range(nch):
        o = chunk_of(o_intras, n) + o_inters[n]
        o = _rms(o, gn_ref[...]) * g_ref[0, chunks[n], vl].astype(F32)
        o_ref[0, chunks[n], vl] = o.astype(BF16)


def _gla(q, k, v, la, g, g_norm, batch, seq):
    q = q.reshape(batch, seq, -1)
    k = k.reshape(batch, seq, -1)
    v = v.reshape(batch, seq, -1)
    la = la.reshape(batch, seq, -1)
    g = g.reshape(batch, seq, -1)

    def spec(c):
        return pl.BlockSpec((1, seq, GLA_STEP_HEADS * c), lambda b, h: (b, 0, h))

    o = pl.pallas_call(
        _gla_kernel,
        grid=(batch, GLA_HEADS // GLA_STEP_HEADS),
        in_specs=[spec(GLA_DK), spec(GLA_DK), spec(GLA_DV), spec(GLA_DK), spec(GLA_DV),
                  _const_spec((1, GLA_DV))],
        out_specs=spec(GLA_DV),
        out_shape=jax.ShapeDtypeStruct((batch, seq, GLA_WIDTH), BF16),
        scratch_shapes=[pltpu.VMEM((GLA_STEP_HEADS, seq, GLA_DK), F32)],
        compiler_params=_cparams(2),
        name="gla_mixer",
    )(q, k, v, la, g, g_norm.reshape(1, -1).astype(F32))
    return o.reshape(batch * seq, GLA_WIDTH)


def kernel(x, positions, ln_g, ln_b, even_w_in, mla_q_norm, mla_kv_norm, mla_w_uq, mla_w_ukv, s5_a_re, s5_a_im, s5_log_dt, s5_b_re, s5_b_im, s5_c_re, s5_c_im, s5_d, s5_w_glu, s5_b_glu, even_w_out, odd_w_in, gla_w_gk2, gla_b_gk, gla_g_norm, odd_w_out):
    batch, seq, _ = x.shape
    xf = x.reshape(batch * seq, D_MODEL)
    odd_inputs = None
    for layer in range(DEPTH):
        j = layer // 2
        if layer % 2 == 0:
            q, k, v, gm, u, gs = _even_in(xf, positions, even_w_in[j], mla_q_norm[j],
                                          mla_kv_norm[j], mla_w_uq[j], mla_w_ukv[j])
            o_mla = _attention(q, k, v, gm, batch, seq)
            o_s5 = _s5(u, gs, batch, seq, s5_a_re[j], s5_a_im[j], s5_log_dt[j], s5_b_re[j], s5_b_im[j],
                       s5_c_re[j], s5_c_im[j], s5_d[j], s5_w_glu[j], s5_b_glu[j])
            w_out = even_w_out[j].astype(BF16)
            nxt = (odd_w_in[j], gla_w_gk2[j], gla_b_gk[j]) if layer + 1 < DEPTH else None
            res = _out_ln(xf, [o_mla, o_s5], [w_out[:MLA_WIDTH], w_out[MLA_WIDTH:]], ln_g[layer], ln_b[layer],
                          odd_in=nxt)
            xf, odd_inputs = (res, None) if nxt is None else (res[0], res[1:])
        else:
            if odd_inputs is None:
                odd_inputs = _odd_in(xf, odd_w_in[j], gla_w_gk2[j], gla_b_gk[j])
            q, k, v, la, g = odd_inputs
            odd_inputs = None
            o = _gla(q, k, v, la, g, gla_g_norm[j], batch, seq)
            xf = _out_ln(xf, [o], [odd_w_out[j].astype(BF16)], ln_g[layer], ln_b[layer])
    return xf.reshape(batch, seq, D_MODEL)
```

```python
import functools
import itertools
import math

import jax
import jax.numpy as jnp
from jax import lax
from jax.experimental import pallas as pl
from jax.experimental.pallas import tpu as pltpu

F32 = jnp.float32
BF16 = jnp.bfloat16

D_MODEL = 1024
DEPTH = 2

MLA_HEADS = 8
MLA_NOPE = 64
MLA_ROPE = 32
MLA_V = 64
MLA_Q_RANK = 256
MLA_KV_RANK = 128
MLA_WIDTH = MLA_HEADS * MLA_V
ROPE_BASE = 10000.0
HEAD_SLOT = 128

S5_WIDTH = 256
S5_GROUP = 16
S5_GROUPS = S5_WIDTH // S5_GROUP
S5_STATE = 64
S5_L = 8
S5_NB = 2
S5_HALVES = 2

GLA_HEADS = 4
GLA_KEY_WIDTH = D_MODEL // 2
GLA_WIDTH = D_MODEL
GLA_DK = GLA_KEY_WIDTH // GLA_HEADS
GLA_DV = GLA_WIDTH // GLA_HEADS
GLA_GATE_RANK = 16
GLA_GATE_NORM = 16.0
GLA_CHUNK = 64
GLA_SUPER = 256
GLA_STEP_HEADS = 2
GLA_STAGE_SKEW = 3

DEEPNORM_ALPHA = (2 * DEPTH) ** 0.25
LN_EPS = 1e-5
RMS_EPS = 1e-6

TOKEN_TILE = 1024
ROW_SPLIT = 4
ATTN_T = 256
ATTN_STEP_HEADS = 2
NEG_BIG = -1e30
V7X_VMEM_BYTES = 64 * 1024 * 1024
VMEM_LIMIT = V7X_VMEM_BYTES * 7 // 8


def _cparams(n_axes):
    return pltpu.CompilerParams(dimension_semantics=("arbitrary",) * n_axes,
                                vmem_limit_bytes=VMEM_LIMIT)


def _const_spec(shape, single=False):
    nd = len(shape)
    if single:
        return pl.BlockSpec(shape, lambda *_: (0,) * nd, pipeline_mode=pl.Buffered(1))
    return pl.BlockSpec(shape, lambda *_: (0,) * nd)


def _sigmoid(x):
    return 1.0 / (1.0 + jnp.exp(-x))


def _silu(x):
    return x * _sigmoid(x)


def _rms(x, g):
    return (x * lax.rsqrt(jnp.mean(x * x, axis=-1, keepdims=True) + RMS_EPS)) * g


ROPE_PACK = 128 // (MLA_ROPE // 2)


def _rope_tables(pos_ref, invf_ref, cos_ref, sin_ref):
    half = MLA_ROPE // 2
    rb = pos_ref.shape[0]
    pos = pos_ref[...].astype(F32)
    lane = lax.broadcasted_iota(jnp.int32, (rb, 128), 1)
    owner = lane // half
    posx = jnp.zeros((rb, 128), F32)
    for s in range(ROPE_PACK):
        posx = jnp.where(owner == s, pos[:, s:s + 1], posx)
    ang = posx * invf_ref[...]
    c = jnp.cos(ang)
    sn = jnp.sin(ang)
    first = (lane & 112) == MLA_NOPE
    second = (lane & 112) == MLA_NOPE + half
    for s in range(ROPE_PACK):
        k1 = (MLA_NOPE - half * s) % 128
        k2 = (MLA_NOPE + half - half * s) % 128
        cf = jnp.where(first, pltpu.roll(c, k1, 1), jnp.where(second, pltpu.roll(c, k2, 1), 1.0))
        sf = jnp.where(first, -pltpu.roll(sn, k1, 1), jnp.where(second, pltpu.roll(sn, k2, 1), 0.0))
        cos_ref[pl.ds(s, rb, stride=ROPE_PACK), :] = cf
        sin_ref[pl.ds(s, rb, stride=ROPE_PACK), :] = sf


EV_CQ = 0
EV_CKV = EV_CQ + MLA_Q_RANK
EV_KPE = EV_CKV + MLA_KV_RANK
EV_GM = EV_KPE + HEAD_SLOT
EV_U = EV_GM + MLA_WIDTH
EV_GS = EV_U + S5_WIDTH
EV_COLS = EV_GS + S5_WIDTH
Q_SCALE = (MLA_NOPE + MLA_ROPE) ** -0.5 * math.log2(math.e)


def _rotate_pairs(x):
    n = x.shape[-1]
    lane = lax.broadcasted_iota(jnp.int32, x.shape, 1)
    lo = (lane & 112) == 64
    return jnp.where(lo, pltpu.roll(x, n - MLA_ROPE // 2, 1), pltpu.roll(x, MLA_ROPE // 2, 1))


def _even_in_kernel(x_ref, pos_ref, invf_ref, w_in_ref, qn_ref, kvn_ref, wuq_ref, wk_ref, wv_ref,
                    q_ref, k_ref, v_ref, gm_ref, u_ref, gs_ref, fold_scr, cos_ref, sin_ref):
    _rope_tables(pos_ref, invf_ref, cos_ref, sin_ref)
    tm = x_ref.shape[0]
    nb = tm // ROW_SPLIT
    for r in range(ROW_SPLIT):
        rows = slice(r * nb, (r + 1) * nb)
        h = jnp.dot(x_ref[rows, :].astype(BF16), w_in_ref[...], preferred_element_type=F32)
        cos = cos_ref[rows, :]
        sin = sin_ref[rows, :]

        cqn = _rms(h[:, EV_CQ:EV_CKV], qn_ref[...]).astype(BF16)
        q = jnp.dot(cqn, wuq_ref[...], preferred_element_type=F32)
        q = q * jnp.tile(cos, (1, MLA_HEADS)) + _rotate_pairs(q) * jnp.tile(sin, (1, MLA_HEADS))
        q_ref[rows, :] = (q * Q_SCALE).astype(BF16)

        kpe = h[:, EV_KPE:EV_GM]
        kpe = kpe * cos + _rotate_pairs(kpe) * sin
        ckvn = _rms(h[:, EV_CKV:EV_KPE], kvn_ref[...]).astype(BF16)
        k = jnp.dot(ckvn, wk_ref[...], preferred_element_type=F32) + jnp.tile(kpe, (1, MLA_HEADS))
        k_ref[rows, :] = k.astype(BF16)
        v_ref[rows, :] = jnp.dot(ckvn, wv_ref[...], preferred_element_type=F32).astype(BF16)

        gm_ref[rows, :] = _silu(h[:, EV_GM:EV_U]).astype(BF16)
        fr = nb // S5_L
        folded = slice(r * fr, (r + 1) * fr)
        hw = S5_WIDTH // S5_HALVES
        for ref, val in ((u_ref, h[:, EV_U:EV_GS]), (gs_ref, _silu(h[:, EV_GS:EV_COLS]))):
            for half in range(S5_HALVES):
                fold_scr[half] = val[:, half * hw:(half + 1) * hw]
            for half in range(S5_HALVES):
                for s in range(S5_L):
                    lo = (half * S5_L + s) * hw
                    ref[folded, lo:lo + hw] = fold_scr[half, pl.ds(s, fr, stride=S5_L), :].astype(BF16)


def _even_in(xf, positions, w_in, q_norm, kv_norm, w_uq, w_ukv):
    t = xf.shape[0]
    tm = TOKEN_TILE
    pad = HEAD_SLOT - MLA_NOPE - MLA_ROPE
    half = MLA_ROPE // 2
    pos = positions.reshape(t // ROPE_PACK, ROPE_PACK)
    inv_freq = ROPE_BASE ** (-jnp.arange(half, dtype=F32) / half)
    invf = jnp.tile(inv_freq, ROPE_PACK).reshape(1, 128)
    cq, ckv, kr, gm, u, gs = jnp.split(
        w_in, [MLA_Q_RANK, MLA_Q_RANK + MLA_KV_RANK, MLA_Q_RANK + MLA_KV_RANK + MLA_ROPE,
               MLA_Q_RANK + MLA_KV_RANK + MLA_ROPE + MLA_WIDTH,
               MLA_Q_RANK + MLA_KV_RANK + MLA_ROPE + MLA_WIDTH + S5_WIDTH], axis=1)
    kr = jnp.pad(kr, ((0, 0), (MLA_NOPE, pad)))
    w_in_p = jnp.concatenate([cq, ckv, kr, gm, u, gs], axis=1).astype(BF16)
    wuq_p = jnp.pad(w_uq.reshape(MLA_Q_RANK, MLA_HEADS, MLA_NOPE + MLA_ROPE),
                    ((0, 0), (0, 0), (0, pad))).reshape(MLA_Q_RANK, MLA_HEADS * HEAD_SLOT).astype(BF16)
    wkv = w_ukv.reshape(MLA_KV_RANK, MLA_HEADS, MLA_NOPE + MLA_V)
    wk_p = jnp.pad(wkv[:, :, :MLA_NOPE], ((0, 0), (0, 0), (0, HEAD_SLOT - MLA_NOPE))
                   ).reshape(MLA_KV_RANK, MLA_HEADS * HEAD_SLOT).astype(BF16)
    wv_p = wkv[:, :, MLA_NOPE:].reshape(MLA_KV_RANK, MLA_WIDTH).astype(BF16)

    def row(n):
        return pl.BlockSpec((tm, n), lambda i: (i, 0))

    qk_w = MLA_HEADS * HEAD_SLOT
    lw = S5_L * S5_WIDTH
    folded = jax.ShapeDtypeStruct((t // S5_L, lw), BF16)
    folded_spec = pl.BlockSpec((tm // S5_L, lw), lambda i: (i, 0))
    outs = [jax.ShapeDtypeStruct((t, qk_w), BF16), jax.ShapeDtypeStruct((t, qk_w), BF16),
            jax.ShapeDtypeStruct((t, MLA_WIDTH), BF16), jax.ShapeDtypeStruct((t, MLA_WIDTH), BF16),
            folded, folded]
    return pl.pallas_call(
        _even_in_kernel,
        grid=(t // tm,),
        in_specs=[row(D_MODEL), pl.BlockSpec((tm // ROPE_PACK, ROPE_PACK), lambda i: (i, 0)),
                  _const_spec((1, 128)),
                  _const_spec((D_MODEL, EV_COLS)), _const_spec((1, MLA_Q_RANK)),
                  _const_spec((1, MLA_KV_RANK)), _const_spec((MLA_Q_RANK, qk_w)),
                  _const_spec((MLA_KV_RANK, qk_w)), _const_spec((MLA_KV_RANK, MLA_WIDTH))],
        out_specs=[row(qk_w), row(qk_w), row(MLA_WIDTH), row(MLA_WIDTH), folded_spec, folded_spec],
        out_shape=outs,
        scratch_shapes=[pltpu.VMEM((S5_HALVES, tm // ROW_SPLIT, S5_WIDTH // S5_HALVES), F32),
                        pltpu.VMEM((tm, HEAD_SLOT), F32), pltpu.VMEM((tm, HEAD_SLOT), F32)],
        compiler_params=_cparams(1),
        name="even_in_proj",
    )(xf, pos, invf, w_in_p, q_norm.reshape(1, -1), kv_norm.reshape(1, -1), wuq_p, wk_p, wv_p)


def _attn_kernel(q_ref, k_ref, v_ref, g_ref, o_ref):
    seq = q_ref.shape[1]
    t = ATTN_T
    n = seq // t
    nt = (((1,), (1,)), ((), ()))
    keep = lax.broadcasted_iota(jnp.int32, (t, t), 1) <= lax.broadcasted_iota(jnp.int32, (t, t), 0)
    lane_head = lax.broadcasted_iota(jnp.int32, (t, ATTN_STEP_HEADS * MLA_V), 1) // MLA_V

    def scores(qi, c):
        rows = slice(qi * t, (qi + 1) * t)
        lanes = slice(c * HEAD_SLOT, (c + 1) * HEAD_SLOT)
        q = q_ref[0, rows, lanes]
        sd = lax.dot_general(q, k_ref[0, rows, lanes], nt, preferred_element_type=F32)
        sd = jnp.where(keep, sd, NEG_BIG)
        if qi == 0:
            return sd
        s = lax.dot_general(q, k_ref[0, :qi * t, lanes], nt, preferred_element_type=F32)
        return jnp.concatenate([s, sd], axis=1)

    def attend(qi, ss):
        rows = slice(qi * t, (qi + 1) * t)
        o = None
        for c, s in enumerate(ss):
            slabs = [s[:, j:j + 128] for j in range(0, s.shape[1], 128)]
            m = functools.reduce(jnp.maximum, slabs)
            m = jnp.max(m, axis=1, keepdims=True)
            lsum = jnp.zeros((t, 128), F32)
            ps = []
            for sl in slabs:
                pj = jnp.exp2(sl - m)
                lsum = lsum + pj
                ps.append(pj.astype(BF16))
            l = jnp.sum(lsum, axis=1, keepdims=True)
            p = jnp.concatenate(ps, axis=1)
            acc = jnp.dot(p, v_ref[0, :(qi + 1) * t, :], preferred_element_type=F32)
            oc = acc * (1.0 / l)
            o = oc if o is None else jnp.where(lane_head == c, oc, o)
        o_ref[0, rows, :] = (o * g_ref[0, rows, :].astype(F32)).astype(BF16)

    pending = None
    for qi in reversed(range(n)):
        ss = [scores(qi, c) for c in range(ATTN_STEP_HEADS)]
        if pending is not None:
            attend(*pending)
        pending = (qi, ss)
    attend(*pending)


def _attention(q, k, v, g, batch, seq):
    q = q.reshape(batch, seq, -1)
    k = k.reshape(batch, seq, -1)
    v = v.reshape(batch, seq, -1)
    g = g.reshape(batch, seq, -1)
    qk = pl.BlockSpec((1, seq, ATTN_STEP_HEADS * HEAD_SLOT), lambda b, st: (b, 0, st))
    vo = pl.BlockSpec((1, seq, ATTN_STEP_HEADS * MLA_V), lambda b, st: (b, 0, st))
    o = pl.pallas_call(
        _attn_kernel,
        grid=(batch, MLA_HEADS // ATTN_STEP_HEADS),
        in_specs=[qk, qk, vo, vo],
        out_specs=vo,
        out_shape=jax.ShapeDtypeStruct((batch, seq, MLA_WIDTH), BF16),
        compiler_params=_cparams(2),
        name="mla_attention",
    )(q, k, v, g)
    return o.reshape(batch * seq, MLA_WIDTH)


def _s5_matrices(a_re, a_im, log_dt, b_re, b_im, c_re, c_im):
    L, G, P, H = S5_L, S5_GROUPS, S5_STATE, S5_GROUP
    dt = jnp.exp(log_dt.astype(F32))[:, None]
    ar, ai = a_re.astype(F32), a_im.astype(F32)
    mag = jnp.exp(ar * dt)
    lr, li = mag * jnp.cos(ai * dt), mag * jnp.sin(ai * dt)
    den = ar * ar + ai * ai
    nr, ni = lr - 1.0, li
    zr = (nr * ar + ni * ai) / den
    zi = (ni * ar - nr * ai) / den
    br, bi = b_re.astype(F32), b_im.astype(F32)
    bbr = zr[..., None] * br - zi[..., None] * bi
    bbi = zr[..., None] * bi + zi[..., None] * br
    cr, ci = c_re.astype(F32), c_im.astype(F32)

    def powers(j):
        j = j.astype(F32)[:, None, None]
        pmag = jnp.exp(j * (ar * dt))
        return pmag * jnp.cos(j * (ai * dt)), pmag * jnp.sin(j * (ai * dt))

    gp, w = G * P, G * H
    pr, pi = powers(jnp.arange(L + 1))
    prow = jnp.concatenate([pr.reshape(L + 1, gp), pi.reshape(L + 1, gp)], axis=0)
    prow3 = prow.reshape(2 * (L + 1), 1, gp)
    pcol3 = prow.reshape(2 * (L + 1), gp, 1)

    same_b = (jnp.arange(w)[:, None] // H) == (jnp.arange(gp)[None, :] // P)
    same_c = (jnp.arange(gp)[:, None] // P) == (jnp.arange(w)[None, :] // H)
    bd_br = jnp.where(same_b, jnp.tile(bbr.transpose(2, 0, 1).reshape(H, gp), (G, 1)), 0.0)
    bd_bi = jnp.where(same_b, jnp.tile(bbi.transpose(2, 0, 1).reshape(H, gp), (G, 1)), 0.0)
    bd_cr = jnp.where(same_c, jnp.tile(cr.transpose(0, 2, 1).reshape(gp, H), (1, G)), 0.0)
    bd_ci = jnp.where(same_c, jnp.tile(ci.transpose(0, 2, 1).reshape(gp, H), (1, G)), 0.0)

    nh, hw = S5_HALVES, S5_WIDTH // S5_HALVES
    bmat, tmat, cmat = pl.pallas_call(
        _s5_prepare_kernel,
        grid=(L,),
        in_specs=[_const_spec((w, gp)), _const_spec((w, gp)), _const_spec((gp, w)), _const_spec((gp, w)),
                  _const_spec((2 * (L + 1), gp)),
                  pl.BlockSpec((1, 1, gp), lambda i: (L - 1 - i, 0, 0)),
                  pl.BlockSpec((1, 1, gp), lambda i: (2 * L - i, 0, 0)),
                  pl.BlockSpec((1, gp, 1), lambda i: (i + 1, 0, 0)),
                  pl.BlockSpec((1, gp, 1), lambda i: (L + 2 + i, 0, 0))],
        out_specs=[pl.BlockSpec((nh, hw, 2 * gp // nh), lambda i: (0, i, 0)),
                   pl.BlockSpec((nh, hw, L * hw), lambda i: (0, i, 0)),
                   pl.BlockSpec((nh, 2 * gp // nh, hw), lambda i: (0, 0, i))],
        out_shape=[jax.ShapeDtypeStruct((nh, L * hw, 2 * gp // nh), BF16),
                   jax.ShapeDtypeStruct((nh, L * hw, L * hw), BF16),
                   jax.ShapeDtypeStruct((nh, 2 * gp // nh, L * hw), BF16)],
        scratch_shapes=[pltpu.VMEM((L, nh, hw, hw), F32)],
        compiler_params=_cparams(1),
        name="s5_prepare",
    )(bd_br, bd_bi, bd_cr, bd_ci, prow, prow3, prow3, pcol3, pcol3)
    lam = jnp.stack([pr[L].reshape(gp), pi[L].reshape(gp)])
    return bmat, tmat, cmat, lam


def _s5_prepare_kernel(br_ref, bi_ref, cr_ref, ci_ref, prow_ref, rr_ref, ri_ref, cr_pow_ref, ci_pow_ref,
                       bmat_ref, tmat_ref, cmat_ref, k_scr):
    L = S5_L
    hw = S5_WIDTH // S5_HALVES
    hgp = S5_GROUPS * S5_STATE // S5_HALVES
    i = pl.program_id(0)
    br, bi = br_ref[...], bi_ref[...]
    cr, ci = cr_ref[...], ci_ref[...]

    halves = [(slice(h * hw, (h + 1) * hw), slice(h * hgp, (h + 1) * hgp)) for h in range(S5_HALVES)]

    @pl.when(i == 0)
    def _():
        for j in range(L):
            pr = prow_ref[j:j + 1, :]
            pi = prow_ref[L + 1 + j:L + 2 + j, :]
            pb_r, pb_i = br * pr - bi * pi, bi * pr + br * pi
            for h, (ch, st) in enumerate(halves):
                k_scr[j, h] = (jnp.dot(pb_r[ch, st], cr[st, ch], precision=lax.Precision.HIGHEST,
                                       preferred_element_type=F32)
                               - jnp.dot(pb_i[ch, st], ci[st, ch], precision=lax.Precision.HIGHEST,
                                         preferred_element_type=F32))

    pr, pi = rr_ref[0], ri_ref[0]
    pcr, pci = cr_pow_ref[0], ci_pow_ref[0]
    pb_r, pb_i = br * pr - bi * pi, bi * pr + br * pi
    g_r, g_i = cr * pcr - ci * pci, -(cr * pci + ci * pcr)
    lags = [k_scr[jnp.maximum(t - i, 0)] for t in range(L)]
    for h, (ch, st) in enumerate(halves):
        bmat_ref[h, :, :hgp] = pb_r[ch, st].astype(BF16)
        bmat_ref[h, :, hgp:] = pb_i[ch, st].astype(BF16)
        for t in range(L):
            tmat_ref[h, :, t * hw:(t + 1) * hw] = jnp.where(t >= i, lags[t][h], 0.0).astype(BF16)
        cmat_ref[h, :hgp, :] = g_r[st, ch].astype(BF16)
        cmat_ref[h, hgp:, :] = g_i[st, ch].astype(BF16)


def _gelu_tanh(y):
    return 0.5 * y * (1.0 + jnp.tanh(math.sqrt(2.0 / math.pi) * (y + 0.044715 * (y * y * y))))


def _s5_kernel(u_ref, gs_ref, bmat_ref, tmat_ref, cmat_ref, lam_ref, dskip_ref, wglu_ref, bglu_ref,
               o_ref, x_scr, h_scr):
    nh = S5_HALVES
    hw = S5_WIDTH // nh
    gp = S5_GROUPS * S5_STATE
    hgp = gp // nh
    hl = S5_L * hw
    rows = u_ref.shape[0]
    per_batch = rows // S5_NB
    for h in range(nh):
        xh = jnp.dot(u_ref[:, h * hl:(h + 1) * hl], bmat_ref[h], preferred_element_type=F32)
        x_scr[:, h * hgp:(h + 1) * hgp] = xh[:, :hgp]
        x_scr[:, gp + h * hgp:gp + (h + 1) * hgp] = xh[:, hgp:]
    lam_r = lam_ref[0:1, :]
    lam_i = lam_ref[1:2, :]

    def block(i, hs):
        out = []
        for b in range(S5_NB):
            hr, hi = hs[b]
            r0 = pl.multiple_of(b * per_batch + i * 8, 8)
            x8 = x_scr[pl.ds(r0, 8), :]
            starts_r, starts_i = [], []
            for jj in range(8):
                starts_r.append(hr)
                starts_i.append(hi)
                xr = x8[jj:jj + 1, :gp]
                xi = x8[jj:jj + 1, gp:]
                hr, hi = lam_r * hr - lam_i * hi + xr, lam_r * hi + lam_i * hr + xi
            h_scr[pl.ds(r0, 8), :gp] = jnp.concatenate(starts_r, axis=0)
            h_scr[pl.ds(r0, 8), gp:] = jnp.concatenate(starts_i, axis=0)
            out.append((hr, hi))
        return tuple(out)

    zero = jnp.zeros((1, gp), F32)
    lax.fori_loop(0, per_batch // 8, block, tuple((zero, zero) for _ in range(S5_NB)))

    hs = h_scr[...].astype(BF16)
    hs = [jnp.concatenate([hs[:, h * hgp:(h + 1) * hgp], hs[:, gp + h * hgp:gp + (h + 1) * hgp]], axis=1)
          for h in range(nh)]

    pair = 2 * hw
    ypairs = [[jnp.dot(u_ref[:, h * hl:h * hl + (tp + 1) * pair],
                       tmat_ref[h, :(tp + 1) * pair, tp * pair:(tp + 1) * pair], preferred_element_type=F32)
               + jnp.dot(hs[h], cmat_ref[h, :, tp * pair:(tp + 1) * pair], preferred_element_type=F32)
               for h in range(nh)] for tp in range(S5_L // 2)]

    def token(ref, t):
        return jnp.concatenate([ref[:, h * hl + t * hw:h * hl + (t + 1) * hw] for h in range(nh)], axis=1)

    ys = []
    for t in range(S5_L):
        e = (t % 2) * hw
        y = jnp.concatenate([yh[:, e:e + hw] for yh in ypairs[t // 2]], axis=1)
        ys.append(_gelu_tanh(y + dskip_ref[...] * token(u_ref, t).astype(F32)))
    zs = [jnp.dot(y.astype(BF16), wglu_ref[...], preferred_element_type=F32) for y in ys]
    for t, (y, z) in enumerate(zip(ys, zs)):
        o = (y * _sigmoid(z + bglu_ref[...])) * token(gs_ref, t).astype(F32)
        for h in range(nh):
            o_ref[h, pl.ds(t, rows, stride=S5_L), :] = o[:, h * hw:(h + 1) * hw]


def _s5(u, gs, batch, seq, a_re, a_im, log_dt, b_re, b_im, c_re, c_im, d_skip, w_glu, b_glu):
    t = batch * seq
    lw = S5_L * S5_WIDTH
    gp2 = 2 * S5_GROUPS * S5_STATE
    bmat, tmat, cmat, lam = _s5_matrices(a_re, a_im, log_dt, b_re, b_im, c_re, c_im)
    rows = S5_NB * seq // S5_L
    halves = S5_HALVES
    blk = pl.BlockSpec((rows, lw), lambda i: (i, 0))
    return pl.pallas_call(
        _s5_kernel,
        grid=(batch // S5_NB,),
        in_specs=[blk, blk, _const_spec(bmat.shape, True), _const_spec(tmat.shape, True),
                  _const_spec(cmat.shape, True), _const_spec((2, gp2 // 2)),
                  _const_spec((1, S5_WIDTH)), _const_spec((S5_WIDTH, S5_WIDTH)), _const_spec((1, S5_WIDTH))],
        out_specs=pl.BlockSpec((halves, rows * S5_L, 128), lambda i: (0, i, 0)),
        out_shape=jax.ShapeDtypeStruct((halves, t, 128), F32),
        scratch_shapes=[pltpu.VMEM((rows, gp2), F32), pltpu.VMEM((rows, gp2), F32)],
        compiler_params=_cparams(1),
        name="s5_mixer",
    )(u, gs, bmat, tmat, cmat, lam, d_skip.reshape(1, -1).astype(F32), w_glu.astype(BF16),
      b_glu.reshape(1, -1).astype(F32))


def _out_ln_kernel(n_act, then_odd_in, x_ref, *refs):
    acts = refs[:n_act]
    ws = refs[n_act:2 * n_act]
    g_ref, b_ref = refs[2 * n_act:2 * n_act + 2]
    rest = refs[2 * n_act + 2:]
    o_ref = rest[ODD_IN_WEIGHTS] if then_odd_in else rest[0]

    def operand(ref, rows):
        if len(ref.shape) == 2:
            return ref[rows, :]
        return jnp.concatenate([ref[i, rows, :] for i in range(ref.shape[0])], axis=1).astype(BF16)

    tm = x_ref.shape[0]
    nblk = ROW_SPLIT
    blocks = [slice(r * tm // nblk, (r + 1) * tm // nblk) for r in range(nblk)]
    xns = []
    for rows in blocks:
        y = jnp.dot(operand(acts[0], rows), ws[0][...], preferred_element_type=F32)
        for a, w in zip(acts[1:], ws[1:]):
            y = y + jnp.dot(operand(a, rows), w[...], preferred_element_type=F32)
        z = DEEPNORM_ALPHA * x_ref[rows, :] + y
        mu = jnp.mean(z, axis=-1, keepdims=True)
        zc = z - mu
        var = jnp.mean(zc * zc, axis=-1, keepdims=True)
        xn = zc * lax.rsqrt(var + LN_EPS) * g_ref[...] + b_ref[...]
        o_ref[rows, :] = xn
        xns.append(xn.astype(BF16))
    if then_odd_in:
        for rows, xb in zip(blocks, xns):
            _odd_in_body(xb, rows, *rest[:ODD_IN_WEIGHTS], *rest[ODD_IN_WEIGHTS + 1:])


def _out_ln(xf, acts, ws, ln_g, ln_b, odd_in=None):
    t = xf.shape[0]
    tm = TOKEN_TILE if odd_in is not None else 2 * TOKEN_TILE
    n = len(acts)

    def row(c):
        return pl.BlockSpec((tm, c), lambda i: (i, 0))

    def act_spec(a):
        if a.ndim == 2:
            return row(a.shape[1])
        return pl.BlockSpec((a.shape[0], tm, a.shape[2]), lambda i: (0, i, 0))

    in_specs = ([row(D_MODEL)] + [act_spec(a) for a in acts] + [_const_spec(w.shape) for w in ws]
                + [_const_spec((1, D_MODEL)), _const_spec((1, D_MODEL))])
    args = [xf, *acts, *ws, ln_g.reshape(1, -1), ln_b.reshape(1, -1)]
    out_specs = [row(D_MODEL)]
    out_shape = [jax.ShapeDtypeStruct((t, D_MODEL), F32)]
    if odd_in is not None:
        odd_args, odd_specs, odd_out_specs, odd_out_shape = _odd_in_operands(t, tm, *odd_in)
        in_specs += odd_specs
        args += odd_args
        out_specs += odd_out_specs
        out_shape += odd_out_shape
    outs = pl.pallas_call(
        functools.partial(_out_ln_kernel, n, odd_in is not None),
        grid=(t // tm,),
        in_specs=in_specs,
        out_specs=out_specs,
        out_shape=out_shape,
        compiler_params=_cparams(1),
        name="out_proj_layernorm" if odd_in is None else "out_proj_layernorm_odd_in_proj",
    )(*args)
    return outs[0] if odd_in is None else outs


def _odd_in_kernel(x_ref, *refs):
    _odd_in_body(x_ref[...].astype(BF16), slice(None), *refs)


OD_Q = 0
OD_K = OD_Q + GLA_KEY_WIDTH
OD_V = OD_K + GLA_KEY_WIDTH
OD_GL = OD_V + GLA_WIDTH
OD_G = OD_GL + 128
OD_COLS = OD_G + GLA_WIDTH
ODD_IN_WEIGHTS = 3


def _odd_in_body(xb, rows, w_ref, wgk_ref, bgk_ref, q_ref, k_ref, v_ref, la_ref, g_ref):
    def proj(lo, hi):
        return jnp.dot(xb, w_ref[:, lo:hi], preferred_element_type=F32)

    z = jnp.dot(proj(OD_GL, OD_G).astype(BF16), wgk_ref[...], preferred_element_type=F32) + bgk_ref[...]
    g = proj(OD_G, OD_COLS)
    log_sig = -(jnp.maximum(-z, 0.0) + jnp.log1p(jnp.exp(-jnp.abs(z))))
    la_ref[rows, :] = log_sig / GLA_GATE_NORM
    v_ref[rows, :] = proj(OD_V, OD_GL).astype(BF16)
    g_ref[rows, :] = _silu(g).astype(BF16)
    q_ref[rows, :] = proj(OD_Q, OD_K).astype(BF16)
    k_ref[rows, :] = proj(OD_K, OD_V).astype(BF16)


def _odd_in_operands(t, tm, w_in, w_gk2, b_gk):
    q, k, v, gl, g = jnp.split(w_in, [OD_K, OD_V, OD_GL, OD_GL + GLA_GATE_RANK], axis=1)
    gl = jnp.pad(gl, ((0, 0), (0, 128 - GLA_GATE_RANK)))
    w_p = jnp.concatenate([q, k, v, gl, g], axis=1).astype(BF16)
    wgk_p = jnp.pad(w_gk2, ((0, 128 - GLA_GATE_RANK), (0, 0))).astype(BF16)

    def row(c):
        return pl.BlockSpec((tm, c), lambda i: (i, 0))

    args = [w_p, wgk_p, b_gk.reshape(1, -1).astype(F32)]
    specs = [_const_spec((D_MODEL, OD_COLS)), _const_spec((128, GLA_KEY_WIDTH)), _const_spec((1, GLA_KEY_WIDTH))]
    assert len(args) == ODD_IN_WEIGHTS
    out_specs = [row(GLA_KEY_WIDTH), row(GLA_KEY_WIDTH), row(GLA_WIDTH), row(GLA_KEY_WIDTH), row(GLA_WIDTH)]
    out_shape = [jax.ShapeDtypeStruct((t, GLA_KEY_WIDTH), BF16), jax.ShapeDtypeStruct((t, GLA_KEY_WIDTH), BF16),
                 jax.ShapeDtypeStruct((t, GLA_WIDTH), BF16), jax.ShapeDtypeStruct((t, GLA_KEY_WIDTH), F32),
                 jax.ShapeDtypeStruct((t, GLA_WIDTH), BF16)]
    return args, specs, out_specs, out_shape


def _odd_in(xf, w_in, w_gk2, b_gk):
    t = xf.shape[0]
    tm = TOKEN_TILE
    args, specs, out_specs, out_shape = _odd_in_operands(t, tm, w_in, w_gk2, b_gk)
    return pl.pallas_call(
        _odd_in_kernel,
        grid=(t // tm,),
        in_specs=[pl.BlockSpec((tm, D_MODEL), lambda i: (i, 0))] + specs,
        out_specs=out_specs,
        out_shape=out_shape,
        compiler_params=_cparams(1),
        name="odd_in_proj",
    )(xf, *args)


def _gla_kernel(q_ref, k_ref, v_ref, la_ref, g_ref, gn_ref, o_ref, bc_scr):
    c, sb = GLA_CHUNK, GLA_SUPER
    seq = q_ref.shape[1]
    ri = lax.broadcasted_iota(jnp.int32, (sb, sb), 0)
    ci = lax.broadcasted_iota(jnp.int32, (sb, sb), 1)
    valid = jnp.logical_and(ci <= ri, ci >= (ri // c) * c)
    tri = jnp.where(valid, 1.0, 0.0).astype(BF16)
    blocks = [slice(s * sb, (s + 1) * sb) for s in range(seq // sb)]
    chunks = [slice(n * c, (n + 1) * c) for n in range(seq // c)]
    heads = [_gla_head(hh, q_ref, k_ref, v_ref, la_ref, g_ref, gn_ref, o_ref, bc_scr, blocks, chunks, tri, valid)
             for hh in range(GLA_STEP_HEADS)]
    live = list(enumerate(heads))
    tick = 0
    while live:
        for i, gen in list(live):
            if tick >= i * GLA_STAGE_SKEW and next(gen, StopIteration) is StopIteration:
                live.remove((i, gen))
        tick += 1


def _gla_head(hh, q_ref, k_ref, v_ref, la_ref, g_ref, gn_ref, o_ref, bc_scr, blocks, chunks, tri, valid):
    c, sb = GLA_CHUNK, GLA_SUPER
    cps = sb // c
    nch = len(chunks)
    nt = (((1,), (1,)), ((), ()))
    tn = (((0,), (0,)), ((), ()))
    scale = GLA_DK ** -0.5
    kl = slice(hh * GLA_DK, (hh + 1) * GLA_DK)
    vl = slice(hh * GLA_DV, (hh + 1) * GLA_DV)

    bcs = []
    for rows in blocks:
        ga = la_ref[0, rows, kl]
        ga_hi = ga.astype(BF16)
        ga_lo = (ga - ga_hi.astype(F32)).astype(BF16)
        cs = jnp.dot(tri, jnp.concatenate([ga_hi, ga_lo], axis=1), preferred_element_type=F32)
        bc = cs[:, :GLA_DK] + cs[:, GLA_DK:]
        bc_scr[hh, rows, :] = bc
        bcs.append(bc.reshape(cps, c, GLA_DK))
    yield

    qds, k_invs, k_ends = [], [], []
    for rows, bc3 in zip(blocks, bcs):
        bl3 = bc3[:, c - 1:c, :]
        q3 = q_ref[0, rows, kl].astype(F32).reshape(cps, c, GLA_DK) * scale
        k3 = k_ref[0, rows, kl].astype(F32).reshape(cps, c, GLA_DK)
        qds.append((q3 * jnp.exp(bc3)).reshape(sb, GLA_DK).astype(BF16))
        k_invs.append((k3 * jnp.exp(-bc3)).reshape(sb, GLA_DK).astype(BF16))
        k_ends.append((k3 * jnp.exp(bl3 - bc3)).reshape(sb, GLA_DK).astype(BF16))
    yield

    atts = [lax.dot_general(qd, k_inv, nt, preferred_element_type=F32) for qd, k_inv in zip(qds, k_invs)]
    yield
    atts = [jnp.where(valid, att, 0.0).astype(BF16) for att in atts]
    o_intras = [jnp.dot(att, v_ref[0, rows, vl], preferred_element_type=F32)
                for att, rows in zip(atts, blocks)]
    yield

    def chunk_of(vals, n):
        s, j = divmod(n, cps)
        return vals[s][j * c:(j + 1) * c]

    kvs = [lax.dot_general(chunk_of(k_ends, n), v_ref[0, chunks[n], vl], tn, preferred_element_type=F32)
           for n in range(nch - 1)]
    yield

    b_last = bc_scr[hh, pl.ds(c - 1, nch, stride=c), :]
    dec = jnp.concatenate([jnp.exp(b_last), jnp.zeros((GLA_DK - nch, GLA_DK), F32)], axis=0)
    dec_t = dec.T

    st = jnp.zeros((GLA_DK, GLA_DV), F32)
    states = [st.astype(BF16)]
    for n in range(nch - 1):
        st = st * dec_t[:, n:n + 1] + kvs[n]
        states.append(st.astype(BF16))
    yield

    o_inters = [jnp.dot(chunk_of(qds, n), states[n], preferred_element_type=F32) for n in range(nch)]
    yield
    for n in range(nch):
        o = chunk_of(o_intras, n) + o_inters[n]
        o = _rms(o, gn_ref[...]) * g_ref[0, chunks[n], vl].astype(F32)
        o_ref[0, chunks[n], vl] = o.astype(BF16)


def _gla(q, k, v, la, g, g_norm, batch, seq):
    q = q.reshape(batch, seq, -1)
    k = k.reshape(batch, seq, -1)
    v = v.reshape(batch, seq, -1)
    la = la.reshape(batch, seq, -1)
    g = g.reshape(batch, seq, -1)

    def spec(c):
        return pl.BlockSpec((1, seq, GLA_STEP_HEADS * c), lambda b, h: (b, 0, h))

    o = pl.pallas_call(
        _gla_kernel,
        grid=(batch, GLA_HEADS // GLA_STEP_HEADS),
        in_specs=[spec(GLA_DK), spec(GLA_DK), spec(GLA_DV), spec(GLA_DK), spec(GLA_DV),
                  _const_spec((1, GLA_DV))],
        out_specs=spec(GLA_DV),
        out_shape=jax.ShapeDtypeStruct((batch, seq, GLA_WIDTH), BF16),
        scratch_shapes=[pltpu.VMEM((GLA_STEP_HEADS, seq, GLA_DK), F32)],
        compiler_params=_cparams(2),
        name="gla_mixer",
    )(q, k, v, la, g, g_norm.reshape(1, -1).astype(F32))
    return o.reshape(batch * seq, GLA_WIDTH)


def kernel(x, positions, ln_g, ln_b, even_w_in, mla_q_norm, mla_kv_norm, mla_w_uq, mla_w_ukv, s5_a_re, s5_a_im, s5_log_dt, s5_b_re, s5_b_im, s5_c_re, s5_c_im, s5_d, s5_w_glu, s5_b_glu, even_w_out, odd_w_in, gla_w_gk2, gla_b_gk, gla_g_norm, odd_w_out):
    batch, seq, _ = x.shape
    xf = x.reshape(batch * seq, D_MODEL)
    odd_inputs = None
    for layer in range(DEPTH):
        j = layer // 2
        if layer % 2 == 0:
            q, k, v, gm, u, gs = _even_in(xf, positions, even_w_in[j], mla_q_norm[j],
                                          mla_kv_norm[j], mla_w_uq[j], mla_w_ukv[j])
            o_mla = _attention(q, k, v, gm, batch, seq)
            o_s5 = _s5(u, gs, batch, seq, s5_a_re[j], s5_a_im[j], s5_log_dt[j], s5_b_re[j], s5_b_im[j],
                       s5_c_re[j], s5_c_im[j], s5_d[j], s5_w_glu[j], s5_b_glu[j])
            w_out = even_w_out[j].astype(BF16)
            nxt = (odd_w_in[j], gla_w_gk2[j], gla_b_gk[j]) if layer + 1 < DEPTH else None
            res = _out_ln(xf, [o_mla, o_s5], [w_out[:MLA_WIDTH], w_out[MLA_WIDTH:]], ln_g[layer], ln_b[layer],
                          odd_in=nxt)
            xf, odd_inputs = (res, None) if nxt is None else (res[0], res[1:])
        else:
            if odd_inputs is None:
                odd_inputs = _odd_in(xf, odd_w_in[j], gla_w_gk2[j], gla_b_gk[j])
            q, k, v, la, g = odd_inputs
            odd_inputs = None
            o = _gla(q, k, v, la, g, gla_g_norm[j], batch, seq)
            xf = _out_ln(xf, [o], [odd_w_out[j].astype(BF16)], ln_g[layer], ln_b[layer])
    return xf.reshape(batch, seq, D_MODEL)
```

```python
import functools
import itertools
import math

import jax
import jax.numpy as jnp
from jax import lax
from jax.experimental import pallas as pl
from jax.experimental.pallas import tpu as pltpu

F32 = jnp.float32
BF16 = jnp.bfloat16

D_MODEL = 1024
DEPTH = 2

MLA_HEADS = 8
MLA_NOPE = 64
MLA_ROPE = 32
MLA_V = 64
MLA_Q_RANK = 256
MLA_KV_RANK = 128
MLA_WIDTH = MLA_HEADS * MLA_V
ROPE_BASE = 10000.0
HEAD_SLOT = 128

S5_WIDTH = 256
S5_GROUP = 16
S5_GROUPS = S5_WIDTH // S5_GROUP
S5_STATE = 64
S5_L = 8
S5_NB = 2
S5_HALVES = 2

GLA_HEADS = 4
GLA_KEY_WIDTH = D_MODEL // 2
GLA_WIDTH = D_MODEL
GLA_DK = GLA_KEY_WIDTH // GLA_HEADS
GLA_DV = GLA_WIDTH // GLA_HEADS
GLA_GATE_RANK = 16
GLA_GATE_NORM = 16.0
GLA_CHUNK = 64
GLA_SUPER = 256
GLA_STEP_HEADS = 2
GLA_STAGE_SKEW = 3

DEEPNORM_ALPHA = (2 * DEPTH) ** 0.25
LN_EPS = 1e-5
RMS_EPS = 1e-6

TOKEN_TILE = 1024
ROW_SPLIT = 4
ATTN_T = 256
ATTN_STEP_HEADS = 2
NEG_BIG = -1e30
V7X_VMEM_BYTES = 64 * 1024 * 1024
VMEM_LIMIT = V7X_VMEM_BYTES * 7 // 8


def _cparams(n_axes):
    return pltpu.CompilerParams(dimension_semantics=("arbitrary",) * n_axes,
                                vmem_limit_bytes=VMEM_LIMIT)


def _const_spec(shape, single=False):
    nd = len(shape)
    if single:
        return pl.BlockSpec(shape, lambda *_: (0,) * nd, pipeline_mode=pl.Buffered(1))
    return pl.BlockSpec(shape, lambda *_: (0,) * nd)


def _sigmoid(x):
    return 1.0 / (1.0 + jnp.exp(-x))


def _silu(x):
    return x * _sigmoid(x)


def _rms(x, g):
    return (x * lax.rsqrt(jnp.mean(x * x, axis=-1, keepdims=True) + RMS_EPS)) * g


ROPE_PACK = 128 // (MLA_ROPE // 2)


def _rope_tables(pos_ref, invf_ref, cos_ref, sin_ref):
    half = MLA_ROPE // 2
    rb = pos_ref.shape[0]
    pos = pos_ref[...].astype(F32)
    lane = lax.broadcasted_iota(jnp.int32, (rb, 128), 1)
    owner = lane // half
    posx = jnp.zeros((rb, 128), F32)
    for s in range(ROPE_PACK):
        posx = jnp.where(owner == s, pos[:, s:s + 1], posx)
    ang = posx * invf_ref[...]
    c = jnp.cos(ang)
    sn = jnp.sin(ang)
    first = (lane & 112) == MLA_NOPE
    second = (lane & 112) == MLA_NOPE + half
    for s in range(ROPE_PACK):
        k1 = (MLA_NOPE - half * s) % 128
        k2 = (MLA_NOPE + half - half * s) % 128
        cf = jnp.where(first, pltpu.roll(c, k1, 1), jnp.where(second, pltpu.roll(c, k2, 1), 1.0))
        sf = jnp.where(first, -pltpu.roll(sn, k1, 1), jnp.where(second, pltpu.roll(sn, k2, 1), 0.0))
        cos_ref[pl.ds(s, rb, stride=ROPE_PACK), :] = cf
        sin_ref[pl.ds(s, rb, stride=ROPE_PACK), :] = sf


EV_CQ = 0
EV_CKV = EV_CQ + MLA_Q_RANK
EV_KPE = EV_CKV + MLA_KV_RANK
EV_GM = EV_KPE + HEAD_SLOT
EV_U = EV_GM + MLA_WIDTH
EV_GS = EV_U + S5_WIDTH
EV_COLS = EV_GS + S5_WIDTH
Q_SCALE = (MLA_NOPE + MLA_ROPE) ** -0.5 * math.log2(math.e)


def _rotate_pairs(x):
    n = x.shape[-1]
    lane = lax.broadcasted_iota(jnp.int32, x.shape, 1)
    lo = (lane & 112) == 64
    return jnp.where(lo, pltpu.roll(x, n - MLA_ROPE // 2, 1), pltpu.roll(x, MLA_ROPE // 2, 1))


def _even_in_kernel(x_ref, pos_ref, invf_ref, w_in_ref, qn_ref, kvn_ref, wuq_ref, wk_ref, wv_ref,
                    q_ref, k_ref, v_ref, gm_ref, u_ref, gs_ref, fold_scr, cos_ref, sin_ref):
    _rope_tables(pos_ref, invf_ref, cos_ref, sin_ref)
    tm = x_ref.shape[0]
    nb = tm // ROW_SPLIT
    for r in range(ROW_SPLIT):
        rows = slice(r * nb, (r + 1) * nb)
        h = jnp.dot(x_ref[rows, :].astype(BF16), w_in_ref[...], preferred_element_type=F32)
        cos = cos_ref[rows, :]
        sin = sin_ref[rows, :]

        cqn = _rms(h[:, EV_CQ:EV_CKV], qn_ref[...]).astype(BF16)
        q = jnp.dot(cqn, wuq_ref[...], preferred_element_type=F32)
        q = q * jnp.tile(cos, (1, MLA_HEADS)) + _rotate_pairs(q) * jnp.tile(sin, (1, MLA_HEADS))
        q_ref[rows, :] = (q * Q_SCALE).astype(BF16)

        kpe = h[:, EV_KPE:EV_GM]
        kpe = kpe * cos + _rotate_pairs(kpe) * sin
        ckvn = _rms(h[:, EV_CKV:EV_KPE], kvn_ref[...]).astype(BF16)
        k = jnp.dot(ckvn, wk_ref[...], preferred_element_type=F32) + jnp.tile(kpe, (1, MLA_HEADS))
        k_ref[rows, :] = k.astype(BF16)
        v_ref[rows, :] = jnp.dot(ckvn, wv_ref[...], preferred_element_type=F32).astype(BF16)

        gm_ref[rows, :] = _silu(h[:, EV_GM:EV_U]).astype(BF16)
        fr = nb // S5_L
        folded = slice(r * fr, (r + 1) * fr)
        hw = S5_WIDTH // S5_HALVES
        for ref, val in ((u_ref, h[:, EV_U:EV_GS]), (gs_ref, _silu(h[:, EV_GS:EV_COLS]))):
            for half in range(S5_HALVES):
                fold_scr[half] = val[:, half * hw:(half + 1) * hw]
            for half in range(S5_HALVES):
                for s in range(S5_L):
                    lo = (half * S5_L + s) * hw
                    ref[folded, lo:lo + hw] = fold_scr[half, pl.ds(s, fr, stride=S5_L), :].astype(BF16)


def _even_in(xf, positions, w_in, q_norm, kv_norm, w_uq, w_ukv):
    t = xf.shape[0]
    tm = TOKEN_TILE
    pad = HEAD_SLOT - MLA_NOPE - MLA_ROPE
    half = MLA_ROPE // 2
    pos = positions.reshape(t // ROPE_PACK, ROPE_PACK)
    inv_freq = ROPE_BASE ** (-jnp.arange(half, dtype=F32) / half)
    invf = jnp.tile(inv_freq, ROPE_PACK).reshape(1, 128)
    cq, ckv, kr, gm, u, gs = jnp.split(
        w_in, [MLA_Q_RANK, MLA_Q_RANK + MLA_KV_RANK, MLA_Q_RANK + MLA_KV_RANK + MLA_ROPE,
               MLA_Q_RANK + MLA_KV_RANK + MLA_ROPE + MLA_WIDTH,
               MLA_Q_RANK + MLA_KV_RANK + MLA_ROPE + MLA_WIDTH + S5_WIDTH], axis=1)
    kr = jnp.pad(kr, ((0, 0), (MLA_NOPE, pad)))
    w_in_p = jnp.concatenate([cq, ckv, kr, gm, u, gs], axis=1).astype(BF16)
    wuq_p = jnp.pad(w_uq.reshape(MLA_Q_RANK, MLA_HEADS, MLA_NOPE + MLA_ROPE),
                    ((0, 0), (0, 0), (0, pad))).reshape(MLA_Q_RANK, MLA_HEADS * HEAD_SLOT).astype(BF16)
    wkv = w_ukv.reshape(MLA_KV_RANK, MLA_HEADS, MLA_NOPE + MLA_V)
    wk_p = jnp.pad(wkv[:, :, :MLA_NOPE], ((0, 0), (0, 0), (0, HEAD_SLOT - MLA_NOPE))
                   ).reshape(MLA_KV_RANK, MLA_HEADS * HEAD_SLOT).astype(BF16)
    wv_p = wkv[:, :, MLA_NOPE:].reshape(MLA_KV_RANK, MLA_WIDTH).astype(BF16)

    def row(n):
        return pl.BlockSpec((tm, n), lambda i: (i, 0))

    qk_w = MLA_HEADS * HEAD_SLOT
    lw = S5_L * S5_WIDTH
    folded = jax.ShapeDtypeStruct((t // S5_L, lw), BF16)
    folded_spec = pl.BlockSpec((tm // S5_L, lw), lambda i: (i, 0))
    outs = [jax.ShapeDtypeStruct((t, qk_w), BF16), jax.ShapeDtypeStruct((t, qk_w), BF16),
            jax.ShapeDtypeStruct((t, MLA_WIDTH), BF16), jax.ShapeDtypeStruct((t, MLA_WIDTH), BF16),
            folded, folded]
    return pl.pallas_call(
        _even_in_kernel,
        grid=(t // tm,),
        in_specs=[row(D_MODEL), pl.BlockSpec((tm // ROPE_PACK, ROPE_PACK), lambda i: (i, 0)),
                  _const_spec((1, 128)),
                  _const_spec((D_MODEL, EV_COLS)), _const_spec((1, MLA_Q_RANK)),
                  _const_spec((1, MLA_KV_RANK)), _const_spec((MLA_Q_RANK, qk_w)),
                  _const_spec((MLA_KV_RANK, qk_w)), _const_spec((MLA_KV_RANK, MLA_WIDTH))],
        out_specs=[row(qk_w), row(qk_w), row(MLA_WIDTH), row(MLA_WIDTH), folded_spec, folded_spec],
        out_shape=outs,
        scratch_shapes=[pltpu.VMEM((S5_HALVES, tm // ROW_SPLIT, S5_WIDTH // S5_HALVES), F32),
                        pltpu.VMEM((tm, HEAD_SLOT), F32), pltpu.VMEM((tm, HEAD_SLOT), F32)],
        compiler_params=_cparams(1),
        name="even_in_proj",
    )(xf, pos, invf, w_in_p, q_norm.reshape(1, -1), kv_norm.reshape(1, -1), wuq_p, wk_p, wv_p)


def _attn_kernel(q_ref, k_ref, v_ref, g_ref, o_ref):
    seq = q_ref.shape[1]
    t = ATTN_T
    n = seq // t
    nt = (((1,), (1,)), ((), ()))
    keep = lax.broadcasted_iota(jnp.int32, (t, t), 1) <= lax.broadcasted_iota(jnp.int32, (t, t), 0)
    lane_head = lax.broadcasted_iota(jnp.int32, (t, ATTN_STEP_HEADS * MLA_V), 1) // MLA_V

    def scores(qi, c):
        rows = slice(qi * t, (qi + 1) * t)
        lanes = slice(c * HEAD_SLOT, (c + 1) * HEAD_SLOT)
        q = q_ref[0, rows, lanes]
        sd = lax.dot_general(q, k_ref[0, rows, lanes], nt, preferred_element_type=F32)
        sd = jnp.where(keep, sd, NEG_BIG)
        if qi == 0:
            return sd
        s = lax.dot_general(q, k_ref[0, :qi * t, lanes], nt, preferred_element_type=F32)
        return jnp.concatenate([s, sd], axis=1)

    def attend(qi, ss):
        rows = slice(qi * t, (qi + 1) * t)
        o = None
        for c, s in enumerate(ss):
            slabs = [s[:, j:j + 128] for j in range(0, s.shape[1], 128)]
            m = functools.reduce(jnp.maximum, slabs)
            m = jnp.max(m, axis=1, keepdims=True)
            lsum = jnp.zeros((t, 128), F32)
            ps = []
            for sl in slabs:
                pj = jnp.exp2(sl - m)
                lsum = lsum + pj
                ps.append(pj.astype(BF16))
            l = jnp.sum(lsum, axis=1, keepdims=True)
            p = jnp.concatenate(ps, axis=1)
            acc = jnp.dot(p, v_ref[0, :(qi + 1) * t, :], preferred_element_type=F32)
            oc = acc * (1.0 / l)
            o = oc if o is None else jnp.where(lane_head == c, oc, o)
        o_ref[0, rows, :] = (o * g_ref[0, rows, :].astype(F32)).astype(BF16)

    pending = None
    for qi in reversed(range(n)):
        ss = [scores(qi, c) for c in range(ATTN_STEP_HEADS)]
        if pending is not None:
            attend(*pending)
        pending = (qi, ss)
    attend(*pending)


def _attention(q, k, v, g, batch, seq):
    q = q.reshape(batch, seq, -1)
    k = k.reshape(batch, seq, -1)
    v = v.reshape(batch, seq, -1)
    g = g.reshape(batch, seq, -1)
    qk = pl.BlockSpec((1, seq, ATTN_STEP_HEADS * HEAD_SLOT), lambda b, st: (b, 0, st))
    vo = pl.BlockSpec((1, seq, ATTN_STEP_HEADS * MLA_V), lambda b, st: (b, 0, st))
    o = pl.pallas_call(
        _attn_kernel,
        grid=(batch, MLA_HEADS // ATTN_STEP_HEADS),
        in_specs=[qk, qk, vo, vo],
        out_specs=vo,
        out_shape=jax.ShapeDtypeStruct((batch, seq, MLA_WIDTH), BF16),
        compiler_params=_cparams(2),
        name="mla_attention",
    )(q, k, v, g)
    return o.reshape(batch * seq, MLA_WIDTH)


def _s5_matrices(a_re, a_im, log_dt, b_re, b_im, c_re, c_im):
    L, G, P, H = S5_L, S5_GROUPS, S5_STATE, S5_GROUP
    dt = jnp.exp(log_dt.astype(F32))[:, None]
    ar, ai = a_re.astype(F32), a_im.astype(F32)
    mag = jnp.exp(ar * dt)
    lr, li = mag * jnp.cos(ai * dt), mag * jnp.sin(ai * dt)
    den = ar * ar + ai * ai
    nr, ni = lr - 1.0, li
    zr = (nr * ar + ni * ai) / den
    zi = (ni * ar - nr * ai) / den
    br, bi = b_re.astype(F32), b_im.astype(F32)
    bbr = zr[..., None] * br - zi[..., None] * bi
    bbi = zr[..., None] * bi + zi[..., None] * br
    cr, ci = c_re.astype(F32), c_im.astype(F32)

    def powers(j):
        j = j.astype(F32)[:, None, None]
        pmag = jnp.exp(j * (ar * dt))
        return pmag * jnp.cos(j * (ai * dt)), pmag * jnp.sin(j * (ai * dt))

    gp, w = G * P, G * H
    pr, pi = powers(jnp.arange(L + 1))
    prow = jnp.concatenate([pr.reshape(L + 1, gp), pi.reshape(L + 1, gp)], axis=0)
    prow3 = prow.reshape(2 * (L + 1), 1, gp)
    pcol3 = prow.reshape(2 * (L + 1), gp, 1)

    same_b = (jnp.arange(w)[:, None] // H) == (jnp.arange(gp)[None, :] // P)
    same_c = (jnp.arange(gp)[:, None] // P) == (jnp.arange(w)[None, :] // H)
    bd_br = jnp.where(same_b, jnp.tile(bbr.transpose(2, 0, 1).reshape(H, gp), (G, 1)), 0.0)
    bd_bi = jnp.where(same_b, jnp.tile(bbi.transpose(2, 0, 1).reshape(H, gp), (G, 1)), 0.0)
    bd_cr = jnp.where(same_c, jnp.tile(cr.transpose(0, 2, 1).reshape(gp, H), (1, G)), 0.0)
    bd_ci = jnp.where(same_c, jnp.tile(ci.transpose(0, 2, 1).reshape(gp, H), (1, G)), 0.0)

    nh, hw = S5_HALVES, S5_WIDTH // S5_HALVES
    bmat, tmat, cmat = pl.pallas_call(
        _s5_prepare_kernel,
        grid=(L,),
        in_specs=[_const_spec((w, gp)), _const_spec((w, gp)), _const_spec((gp, w)), _const_spec((gp, w)),
                  _const_spec((2 * (L + 1), gp)),
                  pl.BlockSpec((1, 1, gp), lambda i: (L - 1 - i, 0, 0)),
                  pl.BlockSpec((1, 1, gp), lambda i: (2 * L - i, 0, 0)),
                  pl.BlockSpec((1, gp, 1), lambda i: (i + 1, 0, 0)),
                  pl.BlockSpec((1, gp, 1), lambda i: (L + 2 + i, 0, 0))],
        out_specs=[pl.BlockSpec((nh, hw, 2 * gp // nh), lambda i: (0, i, 0)),
                   pl.BlockSpec((nh, hw, L * hw), lambda i: (0, i, 0)),
                   pl.BlockSpec((nh, 2 * gp // nh, hw), lambda i: (0, 0, i))],
        out_shape=[jax.ShapeDtypeStruct((nh, L * hw, 2 * gp // nh), BF16),
                   jax.ShapeDtypeStruct((nh, L * hw, L * hw), BF16),
                   jax.ShapeDtypeStruct((nh, 2 * gp // nh, L * hw), BF16)],
        scratch_shapes=[pltpu.VMEM((L, nh, hw, hw), F32)],
        compiler_params=_cparams(1),
        name="s5_prepare",
    )(bd_br, bd_bi, bd_cr, bd_ci, prow, prow3, prow3, pcol3, pcol3)
    lam = jnp.stack([pr[L].reshape(gp), pi[L].reshape(gp)])
    return bmat, tmat, cmat, lam


def _s5_prepare_kernel(br_ref, bi_ref, cr_ref, ci_ref, prow_ref, rr_ref, ri_ref, cr_pow_ref, ci_pow_ref,
                       bmat_ref, tmat_ref, cmat_ref, k_scr):
    L = S5_L
    hw = S5_WIDTH // S5_HALVES
    hgp = S5_GROUPS * S5_STATE // S5_HALVES
    i = pl.program_id(0)
    br, bi = br_ref[...], bi_ref[...]
    cr, ci = cr_ref[...], ci_ref[...]

    halves = [(slice(h * hw, (h + 1) * hw), slice(h * hgp, (h + 1) * hgp)) for h in range(S5_HALVES)]

    @pl.when(i == 0)
    def _():
        for j in range(L):
            pr = prow_ref[j:j + 1, :]
            pi = prow_ref[L + 1 + j:L + 2 + j, :]
            pb_r, pb_i = br * pr - bi * pi, bi * pr + br * pi
            for h, (ch, st) in enumerate(halves):
                k_scr[j, h] = (jnp.dot(pb_r[ch, st], cr[st, ch], precision=lax.Precision.HIGHEST,
                                       preferred_element_type=F32)
                               - jnp.dot(pb_i[ch, st], ci[st, ch], precision=lax.Precision.HIGHEST,
                                         preferred_element_type=F32))

    pr, pi = rr_ref[0], ri_ref[0]
    pcr, pci = cr_pow_ref[0], ci_pow_ref[0]
    pb_r, pb_i = br * pr - bi * pi, bi * pr + br * pi
    g_r, g_i = cr * pcr - ci * pci, -(cr * pci + ci * pcr)
    lags = [k_scr[jnp.maximum(t - i, 0)] for t in range(L)]
    for h, (ch, st) in enumerate(halves):
        bmat_ref[h, :, :hgp] = pb_r[ch, st].astype(BF16)
        bmat_ref[h, :, hgp:] = pb_i[ch, st].astype(BF16)
        for t in range(L):
            tmat_ref[h, :, t * hw:(t + 1) * hw] = jnp.where(t >= i, lags[t][h], 0.0).astype(BF16)
        cmat_ref[h, :hgp, :] = g_r[st, ch].astype(BF16)
        cmat_ref[h, hgp:, :] = g_i[st, ch].astype(BF16)


def _gelu_tanh(y):
    return 0.5 * y * (1.0 + jnp.tanh(math.sqrt(2.0 / math.pi) * (y + 0.044715 * (y * y * y))))


def _s5_kernel(u_ref, gs_ref, bmat_ref, tmat_ref, cmat_ref, lam_ref, dskip_ref, wglu_ref, bglu_ref,
               o_ref, x_scr, h_scr):
    nh = S5_HALVES
    hw = S5_WIDTH // nh
    gp = S5_GROUPS * S5_STATE
    hgp = gp // nh
    hl = S5_L * hw
    rows = u_ref.shape[0]
    per_batch = rows // S5_NB
    for h in range(nh):
        xh = jnp.dot(u_ref[:, h * hl:(h + 1) * hl], bmat_ref[h], preferred_element_type=F32)
        x_scr[:, h * hgp:(h + 1) * hgp] = xh[:, :hgp]
        x_scr[:, gp + h * hgp:gp + (h + 1) * hgp] = xh[:, hgp:]
    lam_r = lam_ref[0:1, :]
    lam_i = lam_ref[1:2, :]

    pair = 2 * hw
    intra = [[jnp.dot(u_ref[:, h * hl:h * hl + (tp + 1) * pair],
                      tmat_ref[h, :(tp + 1) * pair, tp * pair:(tp + 1) * pair], preferred_element_type=F32)
              for h in range(nh)] for tp in range(S5_L // 2)]

    def block(i, hs):
        out = []
        for b in range(S5_NB):
            hr, hi = hs[b]
            r0 = b * per_batch + i * 8
            x8 = x_scr[pl.ds(r0, 8), :]
            starts_r, starts_i = [], []
            for jj in range(8):
                starts_r.append(hr)
                starts_i.append(hi)
                xr = x8[jj:jj + 1, :gp]
                xi = x8[jj:jj + 1, gp:]
                hr, hi = lam_r * hr - lam_i * hi + xr, lam_r * hi + lam_i * hr + xi
            h_scr[pl.ds(r0, 8), :gp] = jnp.concatenate(starts_r, axis=0)
            h_scr[pl.ds(r0, 8), gp:] = jnp.concatenate(starts_i, axis=0)
            out.append((hr, hi))
        return tuple(out)

    zero = jnp.zeros((1, gp), F32)
    states = tuple((zero, zero) for _ in range(S5_NB))
    for i in range(per_batch // 8):
        states = block(i, states)

    hs = h_scr[...].astype(BF16)
    hs = [jnp.concatenate([hs[:, h * hgp:(h + 1) * hgp], hs[:, gp + h * hgp:gp + (h + 1) * hgp]], axis=1)
          for h in range(nh)]

    ypairs = [[intra[tp][h]
               + jnp.dot(hs[h], cmat_ref[h, :, tp * pair:(tp + 1) * pair], preferred_element_type=F32)
               for h in range(nh)] for tp in range(S5_L // 2)]

    def token(ref, t):
        return jnp.concatenate([ref[:, h * hl + t * hw:h * hl + (t + 1) * hw] for h in range(nh)], axis=1)

    ys = []
    for t in range(S5_L):
        e = (t % 2) * hw
        y = jnp.concatenate([yh[:, e:e + hw] for yh in ypairs[t // 2]], axis=1)
        ys.append(_gelu_tanh(y + dskip_ref[...] * token(u_ref, t).astype(F32)))
    zs = [jnp.dot(y.astype(BF16), wglu_ref[...], preferred_element_type=F32) for y in ys]
    for t, (y, z) in enumerate(zip(ys, zs)):
        o = (y * _sigmoid(z + bglu_ref[...])) * token(gs_ref, t).astype(F32)
        for h in range(nh):
            o_ref[h, pl.ds(t, rows, stride=S5_L), :] = o[:, h * hw:(h + 1) * hw]


def _s5(u, gs, batch, seq, a_re, a_im, log_dt, b_re, b_im, c_re, c_im, d_skip, w_glu, b_glu):
    t = batch * seq
    lw = S5_L * S5_WIDTH
    gp2 = 2 * S5_GROUPS * S5_STATE
    bmat, tmat, cmat, lam = _s5_matrices(a_re, a_im, log_dt, b_re, b_im, c_re, c_im)
    rows = S5_NB * seq // S5_L
    halves = S5_HALVES
    blk = pl.BlockSpec((rows, lw), lambda i: (i, 0))
    return pl.pallas_call(
        _s5_kernel,
        grid=(batch // S5_NB,),
        in_specs=[blk, blk, _const_spec(bmat.shape, True), _const_spec(tmat.shape, True),
                  _const_spec(cmat.shape, True), _const_spec((2, gp2 // 2)),
                  _const_spec((1, S5_WIDTH)), _const_spec((S5_WIDTH, S5_WIDTH)), _const_spec((1, S5_WIDTH))],
        out_specs=pl.BlockSpec((halves, rows * S5_L, 128), lambda i: (0, i, 0)),
        out_shape=jax.ShapeDtypeStruct((halves, t, 128), F32),
        scratch_shapes=[pltpu.VMEM((rows, gp2), F32), pltpu.VMEM((rows, gp2), F32)],
        compiler_params=_cparams(1),
        name="s5_mixer",
    )(u, gs, bmat, tmat, cmat, lam, d_skip.reshape(1, -1).astype(F32), w_glu.astype(BF16),
      b_glu.reshape(1, -1).astype(F32))


def _out_ln_kernel(n_act, then_odd_in, x_ref, *refs):
    acts = refs[:n_act]
    ws = refs[n_act:2 * n_act]
    g_ref, b_ref = refs[2 * n_act:2 * n_act + 2]
    rest = refs[2 * n_act + 2:]
    o_ref = rest[ODD_IN_WEIGHTS] if then_odd_in else rest[0]

    def operand(ref, rows):
        if len(ref.shape) == 2:
            return ref[rows, :]
        return jnp.concatenate([ref[i, rows, :] for i in range(ref.shape[0])], axis=1).astype(BF16)

    tm = x_ref.shape[0]
    nblk = ROW_SPLIT
    blocks = [slice(r * tm // nblk, (r + 1) * tm // nblk) for r in range(nblk)]
    xns = []
    for rows in blocks:
        y = jnp.dot(operand(acts[0], rows), ws[0][...], preferred_element_type=F32)
        for a, w in zip(acts[1:], ws[1:]):
            y = y + jnp.dot(operand(a, rows), w[...], preferred_element_type=F32)
        z = DEEPNORM_ALPHA * x_ref[rows, :] + y
        mu = jnp.mean(z, axis=-1, keepdims=True)
        zc = z - mu
        var = jnp.mean(zc * zc, axis=-1, keepdims=True)
        xn = zc * lax.rsqrt(var + LN_EPS) * g_ref[...] + b_ref[...]
        o_ref[rows, :] = xn
        xns.append(xn.astype(BF16))
    if then_odd_in:
        for rows, xb in zip(blocks, xns):
            _odd_in_body(xb, rows, *rest[:ODD_IN_WEIGHTS], *rest[ODD_IN_WEIGHTS + 1:])


def _out_ln(xf, acts, ws, ln_g, ln_b, odd_in=None):
    t = xf.shape[0]
    tm = TOKEN_TILE if odd_in is not None else 2 * TOKEN_TILE
    n = len(acts)

    def row(c):
        return pl.BlockSpec((tm, c), lambda i: (i, 0))

    def act_spec(a):
        if a.ndim == 2:
            return row(a.shape[1])
        return pl.BlockSpec((a.shape[0], tm, a.shape[2]), lambda i: (0, i, 0))

    in_specs = ([row(D_MODEL)] + [act_spec(a) for a in acts] + [_const_spec(w.shape) for w in ws]
                + [_const_spec((1, D_MODEL)), _const_spec((1, D_MODEL))])
    args = [xf, *acts, *ws, ln_g.reshape(1, -1), ln_b.reshape(1, -1)]
    out_specs = [row(D_MODEL)]
    out_shape = [jax.ShapeDtypeStruct((t, D_MODEL), F32)]
    if odd_in is not None:
        odd_args, odd_specs, odd_out_specs, odd_out_shape = _odd_in_operands(t, tm, *odd_in)
        in_specs += odd_specs
        args += odd_args
        out_specs += odd_out_specs
        out_shape += odd_out_shape
    outs = pl.pallas_call(
        functools.partial(_out_ln_kernel, n, odd_in is not None),
        grid=(t // tm,),
        in_specs=in_specs,
        out_specs=out_specs,
        out_shape=out_shape,
        compiler_params=_cparams(1),
        name="out_proj_layernorm" if odd_in is None else "out_proj_layernorm_odd_in_proj",
    )(*args)
    return outs[0] if odd_in is None else outs


def _odd_in_kernel(x_ref, *refs):
    _odd_in_body(x_ref[...].astype(BF16), slice(None), *refs)


OD_Q = 0
OD_K = OD_Q + GLA_KEY_WIDTH
OD_V = OD_K + GLA_KEY_WIDTH
OD_GL = OD_V + GLA_WIDTH
OD_G = OD_GL + 128
OD_COLS = OD_G + GLA_WIDTH
ODD_IN_WEIGHTS = 3


def _odd_in_body(xb, rows, w_ref, wgk_ref, bgk_ref, q_ref, k_ref, v_ref, la_ref, g_ref):
    def proj(lo, hi):
        return jnp.dot(xb, w_ref[:, lo:hi], preferred_element_type=F32)

    z = jnp.dot(proj(OD_GL, OD_G).astype(BF16), wgk_ref[...], preferred_element_type=F32) + bgk_ref[...]
    g = proj(OD_G, OD_COLS)
    log_sig = -(jnp.maximum(-z, 0.0) + jnp.log1p(jnp.exp(-jnp.abs(z))))
    la_ref[rows, :] = log_sig / GLA_GATE_NORM
    v_ref[rows, :] = proj(OD_V, OD_GL).astype(BF16)
    g_ref[rows, :] = _silu(g).astype(BF16)
    q_ref[rows, :] = proj(OD_Q, OD_K).astype(BF16)
    k_ref[rows, :] = proj(OD_K, OD_V).astype(BF16)


def _odd_in_operands(t, tm, w_in, w_gk2, b_gk):
    q, k, v, gl, g = jnp.split(w_in, [OD_K, OD_V, OD_GL, OD_GL + GLA_GATE_RANK], axis=1)
    gl = jnp.pad(gl, ((0, 0), (0, 128 - GLA_GATE_RANK)))
    w_p = jnp.concatenate([q, k, v, gl, g], axis=1).astype(BF16)
    wgk_p = jnp.pad(w_gk2, ((0, 128 - GLA_GATE_RANK), (0, 0))).astype(BF16)

    def row(c):
        return pl.BlockSpec((tm, c), lambda i: (i, 0))

    args = [w_p, wgk_p, b_gk.reshape(1, -1).astype(F32)]
    specs = [_const_spec((D_MODEL, OD_COLS)), _const_spec((128, GLA_KEY_WIDTH)), _const_spec((1, GLA_KEY_WIDTH))]
    assert len(args) == ODD_IN_WEIGHTS
    out_specs = [row(GLA_KEY_WIDTH), row(GLA_KEY_WIDTH), row(GLA_WIDTH), row(GLA_KEY_WIDTH), row(GLA_WIDTH)]
    out_shape = [jax.ShapeDtypeStruct((t, GLA_KEY_WIDTH), BF16), jax.ShapeDtypeStruct((t, GLA_KEY_WIDTH), BF16),
                 jax.ShapeDtypeStruct((t, GLA_WIDTH), BF16), jax.ShapeDtypeStruct((t, GLA_KEY_WIDTH), F32),
                 jax.ShapeDtypeStruct((t, GLA_WIDTH), BF16)]
    return args, specs, out_specs, out_shape


def _odd_in(xf, w_in, w_gk2, b_gk):
    t = xf.shape[0]
    tm = TOKEN_TILE
    args, specs, out_specs, out_shape = _odd_in_operands(t, tm, w_in, w_gk2, b_gk)
    return pl.pallas_call(
        _odd_in_kernel,
        grid=(t // tm,),
        in_specs=[pl.BlockSpec((tm, D_MODEL), lambda i: (i, 0))] + specs,
        out_specs=out_specs,
        out_shape=out_shape,
        compiler_params=_cparams(1),
        name="odd_in_proj",
    )(xf, *args)


def _gla_kernel(q_ref, k_ref, v_ref, la_ref, g_ref, gn_ref, o_ref, bc_scr):
    c, sb = GLA_CHUNK, GLA_SUPER
    seq = q_ref.shape[1]
    ri = lax.broadcasted_iota(jnp.int32, (sb, sb), 0)
    ci = lax.broadcasted_iota(jnp.int32, (sb, sb), 1)
    valid = jnp.logical_and(ci <= ri, ci >= (ri // c) * c)
    tri = jnp.where(valid, 1.0, 0.0).astype(BF16)
    blocks = [slice(s * sb, (s + 1) * sb) for s in range(seq // sb)]
    chunks = [slice(n * c, (n + 1) * c) for n in range(seq // c)]
    heads = [_gla_head(hh, q_ref, k_ref, v_ref, la_ref, g_ref, gn_ref, o_ref, bc_scr, blocks, chunks, tri, valid)
             for hh in range(GLA_STEP_HEADS)]
    live = list(enumerate(heads))
    tick = 0
    while live:
        for i, gen in list(live):
            if tick >= i * GLA_STAGE_SKEW and next(gen, StopIteration) is StopIteration:
                live.remove((i, gen))
        tick += 1


def _gla_head(hh, q_ref, k_ref, v_ref, la_ref, g_ref, gn_ref, o_ref, bc_scr, blocks, chunks, tri, valid):
    c, sb = GLA_CHUNK, GLA_SUPER
    cps = sb // c
    nch = len(chunks)
    nt = (((1,), (1,)), ((), ()))
    tn = (((0,), (0,)), ((), ()))
    scale = GLA_DK ** -0.5
    kl = slice(hh * GLA_DK, (hh + 1) * GLA_DK)
    vl = slice(hh * GLA_DV, (hh + 1) * GLA_DV)

    bcs = []
    for rows in blocks:
        ga = la_ref[0, rows, kl]
        ga_hi = ga.astype(BF16)
        ga_lo = (ga - ga_hi.astype(F32)).astype(BF16)
        cs = jnp.dot(tri, jnp.concatenate([ga_hi, ga_lo], axis=1), preferred_element_type=F32)
        bc = cs[:, :GLA_DK] + cs[:, GLA_DK:]
        bc_scr[hh, rows, :] = bc
        bcs.append(bc.reshape(cps, c, GLA_DK))
    yield

    qds, k_invs, k_ends = [], [], []
    for rows, bc3 in zip(blocks, bcs):
        bl3 = bc3[:, c - 1:c, :]
        q3 = q_ref[0, rows, kl].astype(F32).reshape(cps, c, GLA_DK) * scale
        k3 = k_ref[0, rows, kl].astype(F32).reshape(cps, c, GLA_DK)
        qds.append((q3 * jnp.exp(bc3)).reshape(sb, GLA_DK).astype(BF16))
        k_invs.append((k3 * jnp.exp(-bc3)).reshape(sb, GLA_DK).astype(BF16))
        k_ends.append((k3 * jnp.exp(bl3 - bc3)).reshape(sb, GLA_DK).astype(BF16))
    yield

    atts = [lax.dot_general(qd, k_inv, nt, preferred_element_type=F32) for qd, k_inv in zip(qds, k_invs)]
    yield
    atts = [jnp.where(valid, att, 0.0).astype(BF16) for att in atts]
    o_intras = [jnp.dot(att, v_ref[0, rows, vl], preferred_element_type=F32)
                for att, rows in zip(atts, blocks)]
    yield

    def chunk_of(vals, n):
        s, j = divmod(n, cps)
        return vals[s][j * c:(j + 1) * c]

    kvs = [lax.dot_general(chunk_of(k_ends, n), v_ref[0, chunks[n], vl], tn, preferred_element_type=F32)
           for n in range(nch - 1)]
    yield

    b_last = bc_scr[hh, pl.ds(c - 1, nch, stride=c), :]
    dec = jnp.concatenate([jnp.exp(b_last), jnp.zeros((GLA_DK - nch, GLA_DK), F32)], axis=0)
    dec_t = dec.T

    st = jnp.zeros((GLA_DK, GLA_DV), F32)
    states = [st.astype(BF16)]
    for n in range(nch - 1):
        st = st * dec_t[:, n:n + 1] + kvs[n]
        states.append(st.astype(BF16))
    yield

    o_inters = [jnp.dot(chunk_of(qds, n), states[n], preferred_element_type=F32) for n in range(nch)]
    yield
    for n in range(nch):
        o = chunk_of(o_intras, n) + o_inters[n]
        o = _rms(o, gn_ref[...]) * g_ref[0, chunks[n], vl].astype(F32)
        o_ref[0, chunks[n], vl] = o.astype(BF16)


def _gla(q, k, v, la, g, g_norm, batch, seq):
    q = q.reshape(batch, seq, -1)
    k = k.reshape(batch, seq, -1)
    v = v.reshape(batch, seq, -1)
    la = la.reshape(batch, seq, -1)
    g = g.reshape(batch, seq, -1)

    def spec(c):
        return pl.BlockSpec((1, seq, GLA_STEP_HEADS * c), lambda b, h: (b, 0, h))

    o = pl.pallas_call(
        _gla_kernel,
        grid=(batch, GLA_HEADS // GLA_STEP_HEADS),
        in_specs=[spec(GLA_DK), spec(GLA_DK), spec(GLA_DV), spec(GLA_DK), spec(GLA_DV),
                  _const_spec((1, GLA_DV))],
        out_specs=spec(GLA_DV),
        out_shape=jax.ShapeDtypeStruct((batch, seq, GLA_WIDTH), BF16),
        scratch_shapes=[pltpu.VMEM((GLA_STEP_HEADS, seq, GLA_DK), F32)],
        compiler_params=_cparams(2),
        name="gla_mixer",
    )(q, k, v, la, g, g_norm.reshape(1, -1).astype(F32))
    return o.reshape(batch * seq, GLA_WIDTH)


def kernel(x, positions, ln_g, ln_b, even_w_in, mla_q_norm, mla_kv_norm, mla_w_uq, mla_w_ukv, s5_a_re, s5_a_im, s5_log_dt, s5_b_re, s5_b_im, s5_c_re, s5_c_im, s5_d, s5_w_glu, s5_b_glu, even_w_out, odd_w_in, gla_w_gk2, gla_b_gk, gla_g_norm, odd_w_out):
    batch, seq, _ = x.shape
    xf = x.reshape(batch * seq, D_MODEL)
    odd_inputs = None
    for layer in range(DEPTH):
        j = layer // 2
        if layer % 2 == 0:
            q, k, v, gm, u, gs = _even_in(xf, positions, even_w_in[j], mla_q_norm[j],
                                          mla_kv_norm[j], mla_w_uq[j], mla_w_ukv[j])
            o_mla = _attention(q, k, v, gm, batch, seq)
            o_s5 = _s5(u, gs, batch, seq, s5_a_re[j], s5_a_im[j], s5_log_dt[j], s5_b_re[j], s5_b_im[j],
                       s5_c_re[j], s5_c_im[j], s5_d[j], s5_w_glu[j], s5_b_glu[j])
            w_out = even_w_out[j].astype(BF16)
            nxt = (odd_w_in[j], gla_w_gk2[j], gla_b_gk[j]) if layer + 1 < DEPTH else None
            res = _out_ln(xf, [o_mla, o_s5], [w_out[:MLA_WIDTH], w_out[MLA_WIDTH:]], ln_g[layer], ln_b[layer],
                          odd_in=nxt)
            xf, odd_inputs = (res, None) if nxt is None else (res[0], res[1:])
        else:
            if odd_inputs is None:
                odd_inputs = _odd_in(xf, odd_w_in[j], gla_w_gk2[j], gla_b_gk[j])
            q, k, v, la, g = odd_inputs
            odd_inputs = None
            o = _gla(q, k, v, la, g, gla_g_norm[j], batch, seq)
            xf = _out_ln(xf, [o], [odd_w_out[j].astype(BF16)], ln_g[layer], ln_b[layer])
    return xf.reshape(batch, seq, D_MODEL)
```
